```python
import jax, jax.numpy as jnp
from jax import lax
import numpy as np

D_MODEL = 1024
BATCH = 2
SEQ = 8192
DEPTH = 4

CHUNK = 64
Q_BLOCK = 128
ROPE_THETA = 10000.0
NORM_EPS = 1e-6

SB_HEADS = 6
SB_DIM = 64
MLA_HEADS = 6
MLA_NOPE = 64
MLA_ROPE = 32
MLA_V = 64
MLA_Q_RANK = 256
MLA_KV_RANK = 128
DSA_HEADS = 4
DSA_DIM = 64
IDX_HEADS = 8
IDX_DIM = 32
DSA_TOPK_MAX = 256
N_BRANCHES = 3
N_EXPERTS = 32
TOP_K = 4
D_EXPERT = D_MODEL
SWIGLU_LIMIT = 7.0
SWIGLU_ALPHA = 1.702

IN_SPLIT_SIZES = (
    SB_HEADS * SB_DIM, SB_HEADS * SB_DIM, SB_HEADS * SB_DIM,
    MLA_Q_RANK, MLA_KV_RANK, MLA_ROPE,
    DSA_HEADS * DSA_DIM, DSA_HEADS * DSA_DIM, DSA_HEADS * DSA_DIM,
    IDX_HEADS * IDX_DIM, IDX_DIM, IDX_HEADS,
    N_BRANCHES * D_MODEL,
)
D_IN = (3 * SB_HEADS * SB_DIM + MLA_Q_RANK + MLA_KV_RANK + MLA_ROPE
        + 3 * DSA_HEADS * DSA_DIM + IDX_HEADS * IDX_DIM + IDX_DIM + IDX_HEADS
        + N_BRANCHES * D_MODEL)

kernel_name = "hybrid_sb_mla_dsa_moe_adaln"


def rms_norm(x, g):
    xf = x.astype(jnp.float32)
    y = xf * lax.rsqrt(jnp.mean(xf * xf, axis=-1, keepdims=True) + NORM_EPS)
    return (y * g.astype(jnp.float32)).astype(x.dtype)


def rope_tables(positions, dim):
    inv_freq = ROPE_THETA ** (-jnp.arange(0, dim, 2, dtype=jnp.float32) / dim)
    ang = positions.astype(jnp.float32)[..., None] * inv_freq
    return jnp.cos(ang), jnp.sin(ang)


def apply_rope(x, cos, sin):
    c = cos[:, :, None, :]
    s = sin[:, :, None, :]
    x1, x2 = jnp.split(x, 2, axis=-1)
    return jnp.concatenate([x1 * c - x2 * s, x2 * c + x1 * s], axis=-1)


def sweep_query_blocks(block_fn, seq_len):
    out = lax.map(block_fn, jnp.arange(seq_len // Q_BLOCK))
    nb, b, qb, w = out.shape
    return jnp.swapaxes(out, 0, 1).reshape(b, nb * qb, w)


def chunk_mask(t_ids, key_ids):
    return (key_ids // CHUNK)[None, :] <= (t_ids // CHUNK)[:, None]


def stick_breaking_attention(q, k, v):
    b, s_len, h, d = q.shape
    key_ids = jnp.arange(s_len)
    scale = d ** -0.5

    def block(i):
        t_ids = i * Q_BLOCK + jnp.arange(Q_BLOCK)
        qb = lax.dynamic_slice_in_dim(q, i * Q_BLOCK, Q_BLOCK, axis=1)
        z = jnp.einsum('bqhd,bkhd->bhqk', qb, k).astype(jnp.float32) * scale
        causal = key_ids[None, :] < t_ids[:, None]
        log_stay = jnp.where(causal, jax.nn.log_sigmoid(-z), 0.0)
        log_remain = lax.cumsum(log_stay, axis=3, reverse=True) - log_stay
        a = jnp.where(causal, jnp.exp(jax.nn.log_sigmoid(z) + log_remain), 0.0)
        o = jnp.einsum('bhqk,bkhd->bqhd', a.astype(v.dtype), v)
        return o.reshape(b, Q_BLOCK, h * d)

    return sweep_query_blocks(block, s_len)


def mla_attention(q_nope, q_rope, k_nope, k_rope, v):
    b, s_len, h, _ = q_nope.shape
    key_ids = jnp.arange(s_len)
    scale = (MLA_NOPE + MLA_ROPE) ** -0.5

    def block(i):
        t_ids = i * Q_BLOCK + jnp.arange(Q_BLOCK)
        qn = lax.dynamic_slice_in_dim(q_nope, i * Q_BLOCK, Q_BLOCK, axis=1)
        qr = lax.dynamic_slice_in_dim(q_rope, i * Q_BLOCK, Q_BLOCK, axis=1)
        sc = (jnp.einsum('bqhd,bkhd->bhqk', qn, k_nope)
              + jnp.einsum('bqhr,bkr->bhqk', qr, k_rope)).astype(jnp.float32) * scale
        sc = jnp.where(chunk_mask(t_ids, key_ids)[None, None], sc, -jnp.inf)
        p = jax.nn.softmax(sc, axis=-1)
        o = jnp.einsum('bhqk,bkhd->bqhd', p.astype(v.dtype), v)
        return o.reshape(b, Q_BLOCK, h * MLA_V)

    return sweep_query_blocks(block, s_len)


def dsa_attention(q, k, v, q_idx, k_idx, w_idx, n_select):
    b, s_len, h, d = q.shape
    key_ids = jnp.arange(s_len)
    scale = d ** -0.5
    gather = jax.vmap(lambda arr, ids: arr[ids])

    def block(i):
        t_ids = i * Q_BLOCK + jnp.arange(Q_BLOCK)
        qi = lax.dynamic_slice_in_dim(q_idx, i * Q_BLOCK, Q_BLOCK, axis=1)
        wi = lax.dynamic_slice_in_dim(w_idx, i * Q_BLOCK, Q_BLOCK, axis=1)
        rel = jax.nn.relu(jnp.einsum('bqhd,bkd->bqhk', qi, k_idx).astype(jnp.float32) * IDX_DIM ** -0.5)
        score = jnp.einsum('bqhk,bqh->bqk', rel, wi.astype(jnp.float32)) * IDX_HEADS ** -0.5
        score = jnp.where(chunk_mask(t_ids, key_ids)[None], score, -jnp.inf)
        _, sel = lax.top_k(score, n_select)
        valid = (sel // CHUNK) <= (t_ids // CHUNK)[None, :, None]
        kg = gather(k, sel)
        vg = gather(v, sel)
        qb = lax.dynamic_slice_in_dim(q, i * Q_BLOCK, Q_BLOCK, axis=1)
        sc = jnp.einsum('bqhd,bqkhd->bhqk', qb, kg).astype(jnp.float32) * scale
        sc = jnp.where(valid[:, None], sc, -jnp.inf)
        p = jax.nn.softmax(sc, axis=-1)
        o = jnp.einsum('bhqk,bqkhd->bqhd', p.astype(vg.dtype), vg)
        return o.reshape(b, Q_BLOCK, h * d)

    return sweep_query_blocks(block, s_len)


def hybrid_mixer(h, w_in, q_norm_g, kv_norm_g, w_uq, w_ukv, w_sb_out, w_mla_out, w_dsa_out,
                 w_out, rope_dsa, rope_mla, rope_idx, n_select):
    b, s_len, _ = h.shape
    split_points = np.cumsum(IN_SPLIT_SIZES)[:-1].tolist()
    (sb_q, sb_k, sb_v, mla_cq, mla_ckv, mla_kr, dsa_q, dsa_k, dsa_v,
     idx_q, idx_k, idx_w, gate_logits) = jnp.split(h @ w_in, split_points, axis=-1)

    def heads(t, n):
        return t.reshape(b, s_len, n, -1)

    y_sb = stick_breaking_attention(heads(sb_q, SB_HEADS), heads(sb_k, SB_HEADS),
                                    heads(sb_v, SB_HEADS)) @ w_sb_out

    q_mla = heads(rms_norm(mla_cq, q_norm_g) @ w_uq, MLA_HEADS)
    q_nope, q_rope = jnp.split(q_mla, [MLA_NOPE], axis=-1)
    q_rope = apply_rope(q_rope, *rope_mla)
    kv = heads(rms_norm(mla_ckv, kv_norm_g) @ w_ukv, MLA_HEADS)
    k_nope, v_mla = jnp.split(kv, [MLA_NOPE], axis=-1)
    k_rope = apply_rope(mla_kr[:, :, None, :], *rope_mla)[:, :, 0, :]
    y_mla = mla_attention(q_nope, q_rope, k_nope, k_rope, v_mla) @ w_mla_out

    qd = apply_rope(heads(dsa_q, DSA_HEADS), *rope_dsa)
    kd = apply_rope(heads(dsa_k, DSA_HEADS), *rope_dsa)
    vd = heads(dsa_v, DSA_HEADS)
    qi = apply_rope(heads(idx_q, IDX_HEADS), *rope_idx)
    ki = apply_rope(idx_k[:, :, None, :], *rope_idx)[:, :, 0, :]
    y_dsa = dsa_attention(qd, kd, vd, qi, ki, idx_w, n_select) @ w_dsa_out

    g_sb, g_mla, g_dsa = jnp.split(jax.nn.sigmoid(gate_logits), N_BRANCHES, axis=-1)
    return (g_sb * y_sb + g_mla * y_mla + g_dsa * y_dsa) @ w_out


def clamped_swiglu(gu):
    gate, up = jnp.split(gu, 2, axis=-1)
    gate = jnp.minimum(gate, SWIGLU_LIMIT)
    up = jnp.clip(up, -SWIGLU_LIMIT, SWIGLU_LIMIT)
    return (up + 1.0) * (gate * jax.nn.sigmoid(SWIGLU_ALPHA * gate))


def moe_ffn(h, router_w, router_b, w_gu, b_gu, w_down, b_down):
    b, s_len, d = h.shape
    tok = h.reshape(b * s_len, d)
    logits = (tok @ router_w + router_b).astype(jnp.float32)
    top_v, top_i = lax.top_k(logits, TOP_K)
    top_w = jax.nn.softmax(top_v, axis=-1)
    combine = jnp.einsum('tk,tke->te', top_w, jax.nn.one_hot(top_i, N_EXPERTS, dtype=jnp.float32))
    y = jnp.zeros((b * s_len, d), jnp.float32)
    for e in range(N_EXPERTS):
        out_e = clamped_swiglu(tok @ w_gu[e] + b_gu[e]) @ w_down[e] + b_down[e]
        y = y + combine[:, e:e + 1] * out_e
    return y.reshape(b, s_len, d).astype(h.dtype)


def setup_inputs(seed: int = 0) -> dict:
    key = jax.random.key(seed)
    ks = jax.random.split(key, 24)

    def nrm(k, shape, scale):
        return jax.random.normal(k, shape, jnp.float32) * scale

    def gain(k, shape):
        return 1.0 + nrm(k, shape, 0.05)

    d_sb = SB_HEADS * SB_DIM
    d_mla = MLA_HEADS * MLA_V
    d_dsa = DSA_HEADS * DSA_DIM
    offsets = jax.random.randint(ks[2], (BATCH,), 0, 64, dtype=jnp.int32) * CHUNK
    positions = offsets[:, None] + jnp.arange(SEQ, dtype=jnp.int32)[None, :]
    return {
        "x": nrm(ks[0], (BATCH, SEQ, D_MODEL), 1.0),
        "c": nrm(ks[1], (BATCH, D_MODEL), 1.0),
        "positions": positions,
        "ada_w": nrm(ks[3], (DEPTH, D_MODEL, 6 * D_MODEL), 0.5 * D_MODEL ** -0.5),
        "ada_b": nrm(ks[4], (DEPTH, 6 * D_MODEL), 0.02),
        "norm_mix_g": gain(ks[5], (DEPTH, D_MODEL)),
        "w_in": nrm(ks[6], (DEPTH, D_MODEL, D_IN), D_MODEL ** -0.5),
        "mla_q_norm_g": gain(ks[7], (DEPTH, MLA_Q_RANK)),
        "mla_kv_norm_g": gain(ks[8], (DEPTH, MLA_KV_RANK)),
        "mla_w_uq": nrm(ks[9], (DEPTH, MLA_Q_RANK, MLA_HEADS * (MLA_NOPE + MLA_ROPE)), MLA_Q_RANK ** -0.5),
        "mla_w_ukv": nrm(ks[10], (DEPTH, MLA_KV_RANK, MLA_HEADS * (MLA_NOPE + MLA_V)), MLA_KV_RANK ** -0.5),
        "w_sb_out": nrm(ks[11], (DEPTH, d_sb, D_MODEL), d_sb ** -0.5),
        "w_mla_out": nrm(ks[12], (DEPTH, d_mla, D_MODEL), d_mla ** -0.5),
        "w_dsa_out": nrm(ks[13], (DEPTH, d_dsa, D_MODEL), d_dsa ** -0.5),
        "w_out": nrm(ks[14], (DEPTH, D_MODEL, D_MODEL), D_MODEL ** -0.5),
        "norm_ffn_g": gain(ks[15], (DEPTH, D_MODEL)),
        "router_w": nrm(ks[16], (DEPTH, D_MODEL, N_EXPERTS), D_MODEL ** -0.5),
        "router_b": nrm(ks[17], (DEPTH, N_EXPERTS), 0.01),
        "expert_w_gu": nrm(ks[18], (DEPTH, N_EXPERTS, D_MODEL, 2 * D_EXPERT), D_MODEL ** -0.5),
        "expert_b_gu": nrm(ks[19], (DEPTH, N_EXPERTS, 2 * D_EXPERT), 0.02),
        "expert_w_down": nrm(ks[20], (DEPTH, N_EXPERTS, D_EXPERT, D_MODEL), D_EXPERT ** -0.5),
        "expert_b_down": nrm(ks[21], (DEPTH, N_EXPERTS, D_MODEL), 0.02),
        "final_norm_g": gain(ks[22], (D_MODEL,)),
    }


def reference(x, c, positions, ada_w, ada_b, norm_mix_g, w_in, mla_q_norm_g, mla_kv_norm_g,
              mla_w_uq, mla_w_ukv, w_sb_out, w_mla_out, w_dsa_out, w_out, norm_ffn_g,
              router_w, router_b, expert_w_gu, expert_b_gu, expert_w_down, expert_b_down,
              final_norm_g):
    seq_len = x.shape[1]
    n_select = min(DSA_TOPK_MAX, seq_len // 4)
    rope_dsa = rope_tables(positions, DSA_DIM)
    rope_mla = rope_tables(positions, MLA_ROPE)
    rope_idx = rope_tables(positions, IDX_DIM)
    silu_c = jax.nn.silu(c)
    for l in range(DEPTH):
        mod = silu_c @ ada_w[l] + ada_b[l]
        sh_m, sc_m, g_m, sh_f, sc_f, g_f = [m[:, None, :] for m in jnp.split(mod, 6, axis=-1)]
        h = rms_norm(x, norm_mix_g[l]) * (1.0 + sc_m) + sh_m
        x = x + g_m * hybrid_mixer(h, w_in[l], mla_q_norm_g[l], mla_kv_norm_g[l], mla_w_uq[l],
                                   mla_w_ukv[l], w_sb_out[l], w_mla_out[l], w_dsa_out[l], w_out[l],
                                   rope_dsa, rope_mla, rope_idx, n_select)
        h = rms_norm(x, norm_ffn_g[l]) * (1.0 + sc_f) + sh_f
        x = x + g_f * moe_ffn(h, router_w[l], router_b[l], expert_w_gu[l], expert_b_gu[l],
                              expert_w_down[l], expert_b_down[l])
    return rms_norm(x, final_norm_g)
```

```python
import functools
import math

import numpy as np
import jax
import jax.numpy as jnp
from jax import lax
from jax.experimental import pallas as pl
from jax.experimental.pallas import tpu as pltpu

F32 = jnp.float32
BF16 = jnp.bfloat16
I32 = jnp.int32

D_MODEL = 1024
CHUNK = 64
CHUNK_SHIFT = 6
ROPE_THETA = 10000.0
NORM_EPS = 1e-6
SB_HEADS, SB_DIM = 6, 64
MLA_HEADS, MLA_NOPE, MLA_ROPE, MLA_V = 6, 64, 32, 64
MLA_Q_RANK, MLA_KV_RANK = 256, 128
DSA_HEADS, DSA_DIM = 4, 64
IDX_HEADS, IDX_DIM = 8, 32
DSA_TOPK_MAX = 256
N_EXPERTS, TOP_K = 32, 4
D_EXPERT = D_MODEL
SWIGLU_LIMIT = 7.0
SWIGLU_ALPHA = 1.702

LANES = 128
VMEM_LIMIT = 56 * 1024 * 1024

INT_MIN = -(2 ** 31)
NEG_BIG = -1e30
SB_UNDERFLOW = 104.0

_SPLIT = (384, 384, 384, 256, 128, 32, 256, 256, 256, 256, 32, 8, 3072)
_OFF = np.concatenate([[0], np.cumsum(_SPLIT)]).astype(np.int64)
(O_SBQ, O_SBK, O_SBV, O_CQ, O_CKV, O_KR, O_DQ, O_DK, O_DV, O_IQ, O_IK, O_IW, O_GATE, D_IN) = [
    int(v) for v in _OFF]

N_A = 1408
N_C = 384
N_R = 1152
N_W = 128
N_G = 3072
N_IN = N_A + N_C + 2 * N_R + N_W + N_G
_ROPE_KIND = (0, 0, 0, 0, 1, 1, 1, 1, 2)


def _rot_cols(base, n_heads, d):
    half = d // 2
    idx, sgn = [], []
    for h in range(n_heads):
        for j in range(d):
            if j < half:
                idx.append(base + h * d + j + half)
                sgn.append(-1.0)
            else:
                idx.append(base + h * d + j - half)
                sgn.append(1.0)
    return idx, sgn


def _in_layout():
    idx, sgn = [], []

    def plain(base, n):
        idx.extend(range(base, base + n))
        sgn.extend([1.0] * n)

    def pad(n):
        idx.extend([0] * n)
        sgn.extend([0.0] * n)

    plain(O_DV, 256); plain(O_SBQ, 384); plain(O_SBK, 384); plain(O_SBV, 384)
    plain(O_CQ, 256); plain(O_CKV, 128)
    plain(O_DQ, 256); plain(O_DK, 256); plain(O_IQ, 256)
    for _ in range(IDX_HEADS):
        plain(O_IK, IDX_DIM)
    pad(64); plain(O_KR, 32); pad(32)
    for base, nh, d in ((O_DQ, DSA_HEADS, DSA_DIM), (O_DK, DSA_HEADS, DSA_DIM), (O_IQ, IDX_HEADS, IDX_DIM)):
        i, s = _rot_cols(base, nh, d)
        idx.extend(i); sgn.extend(s)
    i, s = _rot_cols(O_IK, 1, IDX_DIM)
    for _ in range(IDX_HEADS):
        idx.extend(i); sgn.extend(s)
    pad(64)
    i, s = _rot_cols(O_KR, 1, MLA_ROPE)
    idx.extend(i); sgn.extend(s)
    pad(32)
    plain(O_IW, IDX_HEADS); pad(N_W - IDX_HEADS)
    plain(O_GATE, N_G)
    assert len(idx) == N_IN
    return np.asarray(idx, np.int32), np.asarray(sgn, np.float32)


_IN_IDX, _IN_SGN = _in_layout()


def _uq_layout():
    per = MLA_NOPE + MLA_ROPE
    idx, sgn = [], []
    for h in range(MLA_HEADS):
        idx.extend(range(h * per, h * per + per)); sgn.extend([1.0] * per)
        idx.extend([0] * 32); sgn.extend([0.0] * 32)
    for h in range(MLA_HEADS):
        idx.extend([0] * MLA_NOPE); sgn.extend([0.0] * MLA_NOPE)
        i, s = _rot_cols(h * per + MLA_NOPE, 1, MLA_ROPE)
        idx.extend(i); sgn.extend(s)
        idx.extend([0] * 32); sgn.extend([0.0] * 32)
    return np.asarray(idx, np.int32), np.asarray(sgn, np.float32)


def _ukv_layout():
    per = MLA_NOPE + MLA_V
    idx, sgn = [], []
    for h in range(MLA_HEADS):
        idx.extend(range(h * per, h * per + MLA_NOPE)); sgn.extend([1.0] * MLA_NOPE)
        idx.extend([0] * 64); sgn.extend([0.0] * 64)
    for h in range(MLA_HEADS):
        idx.extend(range(h * per + MLA_NOPE, h * per + per)); sgn.extend([1.0] * MLA_V)
    return np.asarray(idx, np.int32), np.asarray(sgn, np.float32)


_UQ_IDX, _UQ_SGN = _uq_layout()
_UKV_IDX, _UKV_SGN = _ukv_layout()
N_QM = MLA_HEADS * LANES
N_VM = MLA_HEADS * MLA_V


def _cparams(sem):
    return pltpu.CompilerParams(dimension_semantics=sem, vmem_limit_bytes=VMEM_LIMIT)


def _dot(a, b):
    return jnp.dot(a, b, preferred_element_type=F32)


def _dot_t(a, b):
    return lax.dot_general(a, b, (((1,), (1,)), ((), ())), preferred_element_type=F32)


def _rms(x):
    return x * lax.rsqrt(jnp.mean(x * x, axis=-1, keepdims=True) + NORM_EPS)


def _ada_kernel(c_ref, w_ref, b_ref, o_ref):
    c = c_ref[...]
    sc = c * (1.0 / (1.0 + jnp.exp(-c)))
    o_ref[...] = jnp.dot(sc, w_ref[...], preferred_element_type=F32,
                         precision=lax.Precision.HIGHEST) + b_ref[...]


def _ada_mod(c, ada_w, ada_b):
    depth, d, n = ada_w.shape
    b = c.shape[0]
    rows = 8
    cp = jnp.zeros((rows, d), F32).at[:b].set(c)
    tn = 2048
    out = pl.pallas_call(
        _ada_kernel,
        grid=(depth, n // tn),
        in_specs=[
            pl.BlockSpec((rows, d), lambda l, j: (0, 0)),
            pl.BlockSpec((None, d, tn), lambda l, j: (l, 0, j)),
            pl.BlockSpec((None, 1, tn), lambda l, j: (l, 0, j)),
        ],
        out_specs=pl.BlockSpec((None, rows, tn), lambda l, j: (l, 0, j)),
        out_shape=jax.ShapeDtypeStruct((depth, rows, n), F32),
        compiler_params=_cparams(("arbitrary", "arbitrary")),
        name="ada_mod",
    )(cp, ada_w, ada_b.reshape(depth, 1, n))
    return out[:, :b]


def _inproj_kernel(x_ref, sc_ref, sh_ref, w_ref, cos_ref, sin_ref,
                   a_ref, c_ref, r_ref, w8_ref, g_ref):
    h = (_rms(x_ref[...]) * sc_ref[...] + sh_ref[...]).astype(BF16)
    o = 0
    for c0 in range(0, N_A, 256):
        c1 = min(c0 + 256, N_A)
        a_ref[:, c0:c1] = _dot(h, w_ref[:, o + c0:o + c1]).astype(BF16)
    o += N_A
    c_ref[...] = _dot(h, w_ref[:, o:o + N_C])
    o += N_C
    for j, kind in enumerate(_ROPE_KIND):
        lo = j * LANES
        y = _dot(h, w_ref[:, o + lo:o + lo + LANES])
        yr = _dot(h, w_ref[:, o + N_R + lo:o + N_R + lo + LANES])
        cs = cos_ref[:, kind * LANES:(kind + 1) * LANES]
        sn = sin_ref[:, kind * LANES:(kind + 1) * LANES]
        r_ref[:, lo:lo + LANES] = (y * cs + yr * sn).astype(BF16)
    o += 2 * N_R
    w8_ref[...] = _dot(h, w_ref[:, o:o + N_W]) * (IDX_DIM ** -0.5 * IDX_HEADS ** -0.5)
    o += N_W
    for c0 in range(0, N_G, 512):
        z = _dot(h, w_ref[:, o + c0:o + c0 + 512])
        g_ref[:, c0:c0 + 512] = (1.0 / (1.0 + jnp.exp(-z))).astype(BF16)


def _inproj(x, scale, shift, w, cos_t, sin_t, layer, seq):
    t, d = x.shape
    tm = 512
    per = seq // tm
    return pl.pallas_call(
        _inproj_kernel,
        grid=(t // tm,),
        in_specs=[
            pl.BlockSpec((tm, d), lambda i: (i, 0)),
            pl.BlockSpec((None, None, 1, d), lambda i: (layer, i // per, 0, 0)),
            pl.BlockSpec((None, None, 1, d), lambda i: (layer, i // per, 0, 0)),
            pl.BlockSpec((None, d, N_IN), lambda i: (layer, 0, 0), pipeline_mode=pl.Buffered(1)),
            pl.BlockSpec((tm, 3 * LANES), lambda i: (i, 0)),
            pl.BlockSpec((tm, 3 * LANES), lambda i: (i, 0)),
        ],
        out_specs=[
            pl.BlockSpec((tm, N_A), lambda i: (i, 0)),
            pl.BlockSpec((tm, N_C), lambda i: (i, 0)),
            pl.BlockSpec((tm, N_R), lambda i: (i, 0)),
            pl.BlockSpec((tm, N_W), lambda i: (i, 0)),
            pl.BlockSpec((tm, N_G), lambda i: (i, 0)),
        ],
        out_shape=[
            jax.ShapeDtypeStruct((t, N_A), BF16),
            jax.ShapeDtypeStruct((t, N_C), F32),
            jax.ShapeDtypeStruct((t, N_R), BF16),
            jax.ShapeDtypeStruct((t, N_W), F32),
            jax.ShapeDtypeStruct((t, N_G), BF16),
        ],
        compiler_params=_cparams(("arbitrary",)),
        name="inproj",
    )(x, scale, shift, w, cos_t, sin_t)


def _mla_up_kernel(c_ref, gq_ref, gkv_ref, wq_ref, wkv_ref, cos_ref, sin_ref, kr_ref,
                   q_ref, k_ref, v_ref):
    c = c_ref[...]
    nq = (_rms(c[:, :MLA_Q_RANK]) * gq_ref[...]).astype(BF16)
    nkv = (_rms(c[:, MLA_Q_RANK:]) * gkv_ref[...]).astype(BF16)
    scale = (MLA_NOPE + MLA_ROPE) ** -0.5
    cs = cos_ref[...] * scale
    sn = sin_ref[...] * scale
    kr = kr_ref[...].astype(F32)
    for h in range(MLA_HEADS):
        lo = h * LANES
        y = _dot(nq, wq_ref[:, lo:lo + LANES])
        yr = _dot(nq, wq_ref[:, N_QM + lo:N_QM + lo + LANES])
        q_ref[:, lo:lo + LANES] = (y * cs + yr * sn).astype(BF16)
        k_ref[:, lo:lo + LANES] = (_dot(nkv, wkv_ref[:, lo:lo + LANES]) + kr).astype(BF16)
    v_ref[...] = _dot(nkv, wkv_ref[:, N_QM:N_QM + N_VM]).astype(BF16)


def _mla_up(cl, gq, gkv, wq, wkv, cos_t, sin_t, rp, layer):
    t = cl.shape[0]
    tm = 512
    return pl.pallas_call(
        _mla_up_kernel,
        grid=(t // tm,),
        in_specs=[
            pl.BlockSpec((tm, N_C), lambda i: (i, 0)),
            pl.BlockSpec((None, 1, MLA_Q_RANK), lambda i: (layer, 0, 0)),
            pl.BlockSpec((None, 1, MLA_KV_RANK), lambda i: (layer, 0, 0)),
            pl.BlockSpec((None, MLA_Q_RANK, 2 * N_QM), lambda i: (layer, 0, 0)),
            pl.BlockSpec((None, MLA_KV_RANK, N_QM + N_VM), lambda i: (layer, 0, 0)),
            pl.BlockSpec((tm, LANES), lambda i: (i, 2)),
            pl.BlockSpec((tm, LANES), lambda i: (i, 2)),
            pl.BlockSpec((tm, LANES), lambda i: (i, 8)),
        ],
        out_specs=[
            pl.BlockSpec((tm, N_QM), lambda i: (i, 0)),
            pl.BlockSpec((tm, N_QM), lambda i: (i, 0)),
            pl.BlockSpec((tm, N_VM), lambda i: (i, 0)),
        ],
        out_shape=[
            jax.ShapeDtypeStruct((t, N_QM), BF16),
            jax.ShapeDtypeStruct((t, N_QM), BF16),
            jax.ShapeDtypeStruct((t, N_VM), BF16),
        ],
        compiler_params=_cparams(("arbitrary",)),
        name="mla_up",
    )(cl, gq, gkv, wq, wkv, cos_t, sin_t, rp)


def _sb_kernel(q_ref, k_ref, v_ref, o_ref, *, tq):
    qi = pl.program_id(2)
    lane = lax.broadcasted_iota(I32, (1, LANES), 1)
    row = lax.broadcasted_iota(I32, (tq, tq), 0)
    col = lax.broadcasted_iota(I32, (tq, tq), 1)
    causal = col < row
    tri = jnp.where(row > col, 1.0, 0.0).astype(BF16)
    q = q_ref[...]
    outs = []
    for h in range(2):
        head = (lane >= h * SB_DIM) & (lane < (h + 1) * SB_DIM)
        qh = jnp.where(head, q, jnp.zeros_like(q)) * jnp.asarray(SB_DIM ** -0.5, BF16)

        def block(j, remain, acc, diagonal, qh=qh):
            start = pl.multiple_of(j * tq, tq)
            kb = k_ref[pl.ds(start, tq), :]
            vb = v_ref[pl.ds(start, tq), :]
            z = _dot_t(qh, kb)
            soft = jnp.log1p(jnp.exp(-jnp.abs(z)))
            log_stay = -(jnp.maximum(z, 0.0) + soft)
            if diagonal:
                log_stay = jnp.where(causal, log_stay, 0.0)
            hi = log_stay.astype(BF16)
            lo = (log_stay - hi.astype(F32)).astype(BF16)
            later = _dot(hi, tri) + _dot(lo, tri)
            log_a = (z + log_stay) + later + remain
            a = jnp.exp(log_a)
            if diagonal:
                a = jnp.where(causal, a, 0.0)
            acc = acc + _dot(a.astype(BF16), vb)
            remain = remain + jnp.sum(log_stay, axis=-1, keepdims=True)
            return remain, acc

        remain, acc = block(qi, jnp.zeros((tq, 1), F32), jnp.zeros((tq, LANES), F32), True)

        def cond(carry):
            j, remain, _ = carry
            return jnp.logical_and(j >= 0, jnp.max(remain) > -SB_UNDERFLOW)

        def body(carry, block=block):
            j, remain, acc = carry
            remain, acc = block(j, remain, acc, False)
            return j - 1, remain, acc

        _, _, acc = lax.while_loop(cond, body, (qi - 1, remain, acc))
        outs.append(acc)
    o_ref[...] = jnp.where(lane < SB_DIM, outs[0], outs[1]).astype(BF16)


def _sb_attention(a, batch, seq):
    t = a.shape[0]
    tq = 256
    nq = seq // tq
    pairs = SB_HEADS // 2
    return pl.pallas_call(
        functools.partial(_sb_kernel, tq=tq),
        grid=(batch, pairs, nq),
        in_specs=[
            pl.BlockSpec((tq, LANES), lambda b, p, i: (b * nq + i, 2 + p)),
            pl.BlockSpec((seq, LANES), lambda b, p, i: (b, 2 + pairs + p)),
            pl.BlockSpec((seq, LANES), lambda b, p, i: (b, 2 + 2 * pairs + p)),
        ],
        out_specs=pl.BlockSpec((tq, LANES), lambda b, p, i: (b * nq + i, p)),
        out_shape=jax.ShapeDtypeStruct((t, SB_HEADS * SB_DIM), BF16),
        compiler_params=_cparams(("arbitrary", "arbitrary", "arbitrary")),
        name="sb_attention",
    )(a, a, a)


def _mla_kernel(q_ref, k_ref, v_ref, o_ref, *, tq):
    qi = pl.program_id(2)
    lane = lax.broadcasted_iota(I32, (1, LANES), 1)
    row = lax.broadcasted_iota(I32, (tq, tq), 0)
    col = lax.broadcasted_iota(I32, (tq, tq), 1)
    visible = (col >> CHUNK_SHIFT) <= (row >> CHUNK_SHIFT)
    q = q_ref[...]

    def step(j, carry, diagonal):
        start = pl.multiple_of(j * tq, tq)
        kb = k_ref[pl.ds(start, tq), :]
        vb = v_ref[pl.ds(start, tq), :]
        new = []
        for h in range(2):
            m, l, acc = carry[h]
            s = _dot_t(q[:, h * LANES:(h + 1) * LANES], kb[:, h * LANES:(h + 1) * LANES])
            if diagonal:
                s = jnp.where(visible, s, NEG_BIG)
            m_new = jnp.maximum(m, jnp.max(s, axis=-1, keepdims=True))
            alpha = jnp.exp(m - m_new)
            p = jnp.exp(s - m_new)
            l = alpha * l + jnp.sum(p, axis=-1, keepdims=True)
            acc = alpha * acc + _dot(p.astype(BF16), vb)
            new.append((m_new, l, acc))
        return tuple(new)

    init = tuple((jnp.full((tq, 1), NEG_BIG, F32), jnp.zeros((tq, 1), F32),
                  jnp.zeros((tq, LANES), F32)) for _ in range(2))
    carry = lax.fori_loop(0, qi, lambda j, c: step(j, c, False), init)
    carry = step(qi, carry, True)
    outs = [acc / l for (_, l, acc) in carry]
    o_ref[...] = jnp.where(lane < MLA_V, outs[0], outs[1]).astype(BF16)


def _mla_attention(qm, km, vm, batch, seq):
    t = qm.shape[0]
    tq = 256
    nq = seq // tq
    pairs = MLA_HEADS // 2
    return pl.pallas_call(
        functools.partial(_mla_kernel, tq=tq),
        grid=(batch, pairs, nq),
        in_specs=[
            pl.BlockSpec((tq, 2 * LANES), lambda b, p, i: (b * nq + i, p)),
            pl.BlockSpec((seq, 2 * LANES), lambda b, p, i: (b, p)),
            pl.BlockSpec((seq, LANES), lambda b, p, i: (b, p)),
        ],
        out_specs=pl.BlockSpec((tq, LANES), lambda b, p, i: (b * nq + i, p)),
        out_shape=jax.ShapeDtypeStruct((t, N_VM), BF16),
        compiler_params=_cparams(("arbitrary", "arbitrary", "arbitrary")),
        name="mla_attention",
    )(qm, km, vm)


def _dsa_kernel(qd_ref, qx_ref, w_ref, kx_ref, kd_ref, vd_ref, o_ref, key_ref, *, tq, tk, n_select):
    it = pl.program_id(1)
    t0 = it * tq
    last = t0 // tk
    lane = lax.broadcasted_iota(I32, (1, LANES), 1)
    lane2 = lax.broadcasted_iota(I32, (1, 2 * LANES), 1)
    rowid = t0 + lax.broadcasted_iota(I32, (tq, 1), 0)
    row_chunk = rowid >> CHUNK_SHIFT

    qx = qx_ref[...]
    w = w_ref[...]
    qx_heads = []
    w_heads = []
    for h in range(IDX_HEADS):
        head = (lane2 >= h * IDX_DIM) & (lane2 < (h + 1) * IDX_DIM)
        qx_heads.append(jnp.where(head, qx, jnp.zeros_like(qx)))
        w_heads.append(w[:, h:h + 1])

    def score_block(j, diagonal):
        start = pl.multiple_of(j * tk, tk)
        kx = kx_ref[pl.ds(start, tk), :]
        score = jnp.zeros((tq, tk), F32)
        for h in range(IDX_HEADS):
            score = score + w_heads[h] * jnp.maximum(_dot_t(qx_heads[h], kx), 0.0)
        score = jnp.where(score == 0.0, 0.0, score)
        bits = pltpu.bitcast(score, I32)
        key = jnp.where(bits < 0, bits ^ jnp.int32(0x7FFFFFFF), bits)
        if diagonal:
            col_chunk = (start + lax.broadcasted_iota(I32, (1, tk), 1)) >> CHUNK_SHIFT
            key = jnp.where(col_chunk <= row_chunk, key, INT_MIN)
        key_ref[j] = key

    def score_body(j, c):
        score_block(j, False)
        return c

    lax.fori_loop(0, last, score_body, 0)
    score_block(last, True)
    n_blocks = last + 1

    def count_ge(cand):
        def body(j, acc):
            ge = jnp.where(key_ref[j] >= cand, 1, 0)
            part = ge[:, 0:LANES]
            for c in range(1, tk // LANES):
                part = part + ge[:, c * LANES:(c + 1) * LANES]
            return acc + part
        acc = lax.fori_loop(0, n_blocks, body, jnp.zeros((tq, LANES), I32))
        return jnp.sum(acc, axis=-1, keepdims=True)

    n_admissible = (row_chunk + 1) * CHUNK
    theta0 = jnp.full((tq, 1), INT_MIN, I32)
    count0 = jnp.zeros((tq, 1), I32) + n_blocks * tk
    done0 = jnp.where(n_admissible <= n_select, 1, 0)

    def search_cond(carry):
        bit, _, _, done = carry
        return jnp.logical_and(bit >= 0, jnp.min(done) == 0)

    def search_body(carry):
        bit, theta, count, done = carry
        cand = theta + lax.shift_left(jnp.int32(1), bit)
        c = count_ge(cand)
        accept = jnp.logical_and(c >= n_select, done == 0)
        theta = jnp.where(accept, cand, theta)
        count = jnp.where(accept, c, count)
        done = jnp.where(jnp.logical_and(accept, c == n_select), 1, done)
        return bit - 1, theta, count, done

    _, theta, count, done = lax.while_loop(search_cond, search_body, (jnp.int32(31), theta0, count0, done0))

    tied = jnp.logical_and(done == 0, jnp.logical_and(count > n_select, theta > INT_MIN))

    @pl.when(jnp.max(jnp.where(tied, 1, 0)) > 0)
    def _():
        above = count_ge(theta + 1)
        need = (n_select - above).astype(F32)
        r = lax.broadcasted_iota(I32, (tk, tk), 0)
        c = lax.broadcasted_iota(I32, (tk, tk), 1)
        before = jnp.where(r < c, 1.0, 0.0).astype(BF16)

        def body(j, seen):
            key = key_ref[j]
            eq = jnp.logical_and(key == theta, tied)
            eqf = jnp.where(eq, 1.0, 0.0)
            rank = seen + _dot(eqf.astype(BF16), before)
            key_ref[j] = jnp.where(jnp.logical_and(eq, rank >= need), INT_MIN, key)
            return seen + jnp.sum(eqf, axis=-1, keepdims=True)

        lax.fori_loop(0, n_blocks, body, jnp.zeros((tq, 1), F32))

    theta = jnp.maximum(theta, INT_MIN + 1)

    qd = qd_ref[...] * jnp.asarray(DSA_DIM ** -0.5, BF16)
    q_heads = []
    for h in range(DSA_HEADS):
        blk = qd[:, (h // 2) * LANES:(h // 2 + 1) * LANES]
        head = (lane >= (h % 2) * DSA_DIM) & (lane < (h % 2 + 1) * DSA_DIM)
        q_heads.append(jnp.where(head, blk, jnp.zeros_like(blk)))

    def attend(j, carry):
        start = pl.multiple_of(j * tk, tk)
        sel = key_ref[j] >= theta
        kb = kd_ref[pl.ds(start, tk), :]
        vb = vd_ref[pl.ds(start, tk), :]
        new = []
        for h in range(DSA_HEADS):
            m, l, acc = carry[h]
            p0 = (h // 2) * LANES
            s = jnp.where(sel, _dot_t(q_heads[h], kb[:, p0:p0 + LANES]), NEG_BIG)
            m_new = jnp.maximum(m, jnp.max(s, axis=-1, keepdims=True))
            alpha = jnp.exp(m - m_new)
            p = jnp.exp(s - m_new)
            l = alpha * l + jnp.sum(p, axis=-1, keepdims=True)
            acc = alpha * acc + _dot(p.astype(BF16), vb[:, p0:p0 + LANES])
            new.append((m_new, l, acc))
        return tuple(new)

    init = tuple((jnp.full((tq, 1), NEG_BIG, F32), jnp.zeros((tq, 1), F32),
                  jnp.zeros((tq, LANES), F32)) for _ in range(DSA_HEADS))
    carry = lax.fori_loop(0, n_blocks, attend, init)
    outs = [acc / l for (_, l, acc) in carry]
    for p in range(DSA_HEADS // 2):
        o_ref[:, p * LANES:(p + 1) * LANES] = jnp.where(lane < DSA_DIM, outs[2 * p], outs[2 * p + 1]).astype(BF16)


def _dsa_attention(a, rp, w8, batch, seq, n_select):
    t = a.shape[0]
    tq = 128
    tk = min(512, seq)
    nq = seq // tq
    width = DSA_HEADS * DSA_DIM
    return pl.pallas_call(
        functools.partial(_dsa_kernel, tq=tq, tk=tk, n_select=n_select),
        grid=(batch, nq),
        in_specs=[
            pl.BlockSpec((tq, width), lambda b, i: (b * nq + i, 0)),
            pl.BlockSpec((tq, width), lambda b, i: (b * nq + i, 2)),
            pl.BlockSpec((tq, N_W), lambda b, i: (b * nq + i, 0)),
            pl.BlockSpec((seq, width), lambda b, i: (b, 3)),
            pl.BlockSpec((seq, width), lambda b, i: (b, 1)),
            pl.BlockSpec((seq, width), lambda b, i: (b, 0)),
        ],
        out_specs=pl.BlockSpec((tq, width), lambda b, i: (b * nq + i, 0)),
        out_shape=jax.ShapeDtypeStruct((t, width), BF16),
        scratch_shapes=[pltpu.VMEM((seq // tk, tq, tk), I32)],
        compiler_params=_cparams(("arbitrary", "arbitrary")),
        name="dsa_attention",
    )(rp, rp, w8, rp, rp, a)


def _merge_kernel(x_ref, osb_ref, omla_ref, odsa_ref, g_ref, gm_ref, wsb_ref, wmla_ref, wdsa_ref, wout_ref,
                  o_ref):
    d = D_MODEL
    merged = (g_ref[:, 0:d].astype(F32) * _dot(osb_ref[...], wsb_ref[...])
              + g_ref[:, d:2 * d].astype(F32) * _dot(omla_ref[...], wmla_ref[...])
              + g_ref[:, 2 * d:3 * d].astype(F32) * _dot(odsa_ref[...], wdsa_ref[...]))
    o_ref[...] = x_ref[...] + gm_ref[...] * _dot(merged.astype(BF16), wout_ref[...])


def _merge(x, osb, omla, odsa, gates, gm, wsb, wmla, wdsa, wout, layer, seq):
    t, d = x.shape
    tm = 512
    per = seq // tm
    row = lambda i: (i, 0)
    wspec = lambda k: pl.BlockSpec((None, k, d), lambda i: (layer, 0, 0))
    return pl.pallas_call(
        _merge_kernel,
        grid=(t // tm,),
        in_specs=[
            pl.BlockSpec((tm, d), row),
            pl.BlockSpec((tm, osb.shape[1]), row),
            pl.BlockSpec((tm, omla.shape[1]), row),
            pl.BlockSpec((tm, odsa.shape[1]), row),
            pl.BlockSpec((tm, N_G), row),
            pl.BlockSpec((None, None, 1, d), lambda i: (layer, i // per, 0, 0)),
            wspec(osb.shape[1]), wspec(omla.shape[1]), wspec(odsa.shape[1]), wspec(d),
        ],
        out_specs=pl.BlockSpec((tm, d), row),
        out_shape=jax.ShapeDtypeStruct((t, d), F32),
        compiler_params=_cparams(("arbitrary",)),
        name="merge",
    )(x, osb, omla, odsa, gates, gm, wsb, wmla, wdsa, wout)


def _router_kernel(x_ref, sc_ref, sh_ref, rw_ref, rb_ref, h_ref, ti_ref, tw_ref, tr_ref, cnt_ref, run_ref, *, tm):
    @pl.when(pl.program_id(0) == 0)
    def _():
        run_ref[...] = jnp.zeros_like(run_ref)

    h = _rms(x_ref[...]) * sc_ref[...] + sh_ref[...]
    h_ref[...] = h.astype(BF16)
    logits = jnp.dot(h, rw_ref[...], preferred_element_type=F32, precision=lax.Precision.HIGHEST) + rb_ref[...]
    lane = lax.broadcasted_iota(I32, (tm, LANES), 1)
    work = logits
    vals, hots = [], []
    for _ in range(TOP_K):
        m = jnp.max(work, axis=-1, keepdims=True)
        first = jnp.min(jnp.where(work == m, lane, LANES), axis=-1, keepdims=True)
        hot = lane == first
        vals.append(m)
        hots.append(hot)
        work = jnp.where(hot, -jnp.inf, work)
    exps = [jnp.exp(v - vals[0]) for v in vals]
    denom = exps[0] + exps[1] + exps[2] + exps[3]
    chosen = jnp.zeros((tm, LANES), F32)
    for hot in hots:
        chosen = chosen + jnp.where(hot, 1.0, 0.0)
    r = lax.broadcasted_iota(I32, (tm, tm), 0)
    c = lax.broadcasted_iota(I32, (tm, tm), 1)
    earlier = jnp.where(c < r, 1.0, 0.0).astype(BF16)
    rank_all = _dot(earlier, chosen.astype(BF16)) + run_ref[...]
    ti = jnp.zeros((tm, LANES), I32)
    tw = jnp.zeros((tm, LANES), F32)
    tr = jnp.zeros((tm, LANES), I32)
    for k in range(TOP_K):
        e_k = jnp.sum(jnp.where(hots[k], lane, 0), axis=-1, keepdims=True)
        r_k = jnp.sum(jnp.where(hots[k], rank_all, 0.0), axis=-1, keepdims=True).astype(I32)
        ti = jnp.where(lane == k, e_k, ti)
        tw = jnp.where(lane == k, exps[k] / denom, tw)
        tr = jnp.where(lane == k, r_k, tr)
    ti_ref[...] = ti
    tw_ref[...] = tw
    tr_ref[...] = tr
    run_ref[...] = run_ref[...] + jnp.sum(chosen, axis=0, keepdims=True)
    cnt_ref[...] = run_ref[...]


def _router(x, scale, shift, rw, rb, layer, seq):
    t, d = x.shape
    tm = 256
    per = seq // tm
    row = lambda i: (i, 0)
    return pl.pallas_call(
        functools.partial(_router_kernel, tm=tm),
        grid=(t // tm,),
        in_specs=[
            pl.BlockSpec((tm, d), row),
            pl.BlockSpec((None, None, 1, d), lambda i: (layer, i // per, 0, 0)),
            pl.BlockSpec((None, None, 1, d), lambda i: (layer, i // per, 0, 0)),
            pl.BlockSpec((None, d, LANES), lambda i: (layer, 0, 0)),
            pl.BlockSpec((None, 1, LANES), lambda i: (layer, 0, 0)),
        ],
        out_specs=[
            pl.BlockSpec((tm, d), row),
            pl.BlockSpec((tm, LANES), row),
            pl.BlockSpec((tm, LANES), row),
            pl.BlockSpec((tm, LANES), row),
            pl.BlockSpec((1, LANES), lambda i: (0, 0)),
        ],
        out_shape=[
            jax.ShapeDtypeStruct((t, d), BF16),
            jax.ShapeDtypeStruct((t, LANES), I32),
            jax.ShapeDtypeStruct((t, LANES), F32),
            jax.ShapeDtypeStruct((t, LANES), I32),
            jax.ShapeDtypeStruct((1, LANES), F32),
        ],
        scratch_shapes=[pltpu.VMEM((1, LANES), F32)],
        compiler_params=_cparams(("arbitrary",)),
        name="router",
    )(x, scale, shift, rw, rb)


def _expert_kernel(te_ref, tf_ref, nv_ref, x_ref, rw_ref, wgu_ref, bgu_ref, wd_ref, bd_ref, o_ref,
                   wgu_bf, wd_bf):
    i = pl.program_id(0)

    @pl.when(i >= nv_ref[0])
    def _():
        o_ref[...] = jnp.zeros_like(o_ref)

    @pl.when(i < nv_ref[0])
    def _():
        @pl.when(tf_ref[i] == 1)
        def _():
            wgu_bf[...] = wgu_ref[...].astype(BF16)
            wd_bf[...] = wd_ref[...].astype(BF16)

        gu = _dot(x_ref[...], wgu_bf[...]) + bgu_ref[...]
        gate = jnp.minimum(gu[:, :D_EXPERT], SWIGLU_LIMIT)
        up = jnp.clip(gu[:, D_EXPERT:], -SWIGLU_LIMIT, SWIGLU_LIMIT)
        act = (up + 1.0) * (gate * (1.0 / (1.0 + jnp.exp(-SWIGLU_ALPHA * gate))))
        out = _dot(act.astype(BF16), wd_bf[...]) + bd_ref[...]
        o_ref[...] = rw_ref[...] * out


def _experts(xs, row_w, tile_e, tile_first, n_valid, wgu, bgu, wd, bd, layer, tm):
    p, d = xs.shape
    n_tiles = p // tm
    grid_spec = pltpu.PrefetchScalarGridSpec(
        num_scalar_prefetch=3,
        grid=(n_tiles,),
        in_specs=[
            pl.BlockSpec((tm, d), lambda i, te, tf, nv: (i, 0)),
            pl.BlockSpec((tm, 1), lambda i, te, tf, nv: (i, 0)),
            pl.BlockSpec((None, None, d, 2 * D_EXPERT), lambda i, te, tf, nv: (layer, te[i], 0, 0)),
            pl.BlockSpec((None, None, 1, 2 * D_EXPERT), lambda i, te, tf, nv: (layer, te[i], 0, 0)),
            pl.BlockSpec((None, None, D_EXPERT, d), lambda i, te, tf, nv: (layer, te[i], 0, 0)),
            pl.BlockSpec((None, None, 1, d), lambda i, te, tf, nv: (layer, te[i], 0, 0)),
        ],
        out_specs=pl.BlockSpec((tm, d), lambda i, te, tf, nv: (i, 0)),
        scratch_shapes=[pltpu.VMEM((d, 2 * D_EXPERT), BF16), pltpu.VMEM((D_EXPERT, d), BF16)],
    )
    return pl.pallas_call(
        _expert_kernel,
        grid_spec=grid_spec,
        out_shape=jax.ShapeDtypeStruct((p, d), F32),
        compiler_params=_cparams(("arbitrary",)),
        name="experts",
    )(tile_e, tile_first, n_valid, xs, row_w, wgu, bgu, wd, bd)


def _combine_kernel(x_ref, y_ref, g_ref, fg_ref, o_ref, *, final):
    d = D_MODEL
    y = y_ref[:, 0:d] + y_ref[:, d:2 * d] + y_ref[:, 2 * d:3 * d] + y_ref[:, 3 * d:4 * d]
    x = x_ref[...] + g_ref[...] * y
    if final:
        x = _rms(x) * fg_ref[...]
    o_ref[...] = x


def _combine(x, yg, gf, final_g, layer, seq, final):
    t, d = x.shape
    tm = 256
    per = seq // tm
    return pl.pallas_call(
        functools.partial(_combine_kernel, final=final),
        grid=(t // tm,),
        in_specs=[
            pl.BlockSpec((tm, d), lambda i: (i, 0)),
            pl.BlockSpec((tm, TOP_K * d), lambda i: (i, 0)),
            pl.BlockSpec((None, None, 1, d), lambda i: (layer, i // per, 0, 0)),
            pl.BlockSpec((1, d), lambda i: (0, 0)),
        ],
        out_specs=pl.BlockSpec((tm, d), lambda i: (i, 0)),
        out_shape=jax.ShapeDtypeStruct((t, d), F32),
        compiler_params=_cparams(("arbitrary",)),
        name="combine",
    )(x, yg, gf, final_g)


def _rope_tables(positions):
    pos = positions.reshape(-1).astype(F32)

    def cs(dim):
        inv_freq = ROPE_THETA ** (-jnp.arange(0, dim, 2, dtype=F32) / dim)
        ang = pos[:, None] * inv_freq
        return jnp.cos(ang), jnp.sin(ang)

    c_d, s_d = cs(DSA_DIM)
    c_i, s_i = cs(IDX_DIM)
    c_m, s_m = cs(MLA_ROPE)
    t = pos.shape[0]
    ones, zeros = jnp.ones((t, 64), F32), jnp.zeros((t, 64), F32)
    cos_t = jnp.concatenate([jnp.tile(c_d, (1, 4)), jnp.tile(c_i, (1, 8)),
                             ones, c_m, c_m, ones[:, :32]], axis=1)
    sin_t = jnp.concatenate([jnp.tile(s_d, (1, 4)), jnp.tile(s_i, (1, 8)),
                             zeros, s_m, s_m, zeros[:, :32]], axis=1)
    return cos_t, sin_t


def _permute_cols(w, idx, sgn):
    return (jnp.take(w, jnp.asarray(idx), axis=-1) * jnp.asarray(sgn)).astype(BF16)


def _dispatch_tables(ti, tr, tw, counts, tm):
    t = ti.shape[0]
    p = t * TOP_K + N_EXPERTS * tm
    n_tiles = p // tm
    cnt = counts.astype(I32)
    padded = ((cnt + tm - 1) // tm) * tm
    ends = jnp.cumsum(padded)
    starts = ends - padded
    pos = starts[ti] + tr
    flat = pos.reshape(-1)
    tok = jnp.repeat(jnp.arange(t, dtype=I32), TOP_K)
    row_src = jnp.zeros((p,), I32).at[flat].set(tok)
    row_w = jnp.zeros((p,), F32).at[flat].set(tw.reshape(-1))
    tile_start = jnp.arange(n_tiles, dtype=I32) * tm
    tile_e = jnp.minimum(jnp.searchsorted(ends, tile_start, side="right"), N_EXPERTS - 1).astype(I32)
    n_valid = (ends[-1] // tm).astype(I32).reshape(1)
    last_e = tile_e[jnp.maximum(n_valid[0] - 1, 0)]
    tile_e = jnp.where(jnp.arange(n_tiles) < n_valid[0], tile_e, last_e)
    tile_first = jnp.concatenate([jnp.ones((1,), I32), (tile_e[1:] != tile_e[:-1]).astype(I32)])
    return pos, row_src, row_w.reshape(p, 1), tile_e, tile_first, n_valid


def kernel(x, c, positions, ada_w, ada_b, norm_mix_g, w_in, mla_q_norm_g, mla_kv_norm_g, mla_w_uq, mla_w_ukv,
           w_sb_out, w_mla_out, w_dsa_out, w_out, norm_ffn_g, router_w, router_b, expert_w_gu, expert_b_gu,
           expert_w_down, expert_b_down, final_norm_g):
    batch, seq, d = x.shape
    depth = ada_w.shape[0]
    t = batch * seq
    n_select = min(DSA_TOPK_MAX, seq // 4)
    tm_e = 256

    mod = _ada_mod(c, ada_w, ada_b)
    sh_m, sc_m, g_m, sh_f, sc_f, g_f = [m[:, :, None, :] for m in jnp.split(mod, 6, axis=-1)]
    scale_m = norm_mix_g[:, None, None, :] * (1.0 + sc_m)
    scale_f = norm_ffn_g[:, None, None, :] * (1.0 + sc_f)

    cos_t, sin_t = _rope_tables(positions)
    w_in_p = _permute_cols(w_in, _IN_IDX, _IN_SGN)
    w_uq_p = _permute_cols(mla_w_uq, _UQ_IDX, _UQ_SGN)
    w_ukv_p = _permute_cols(mla_w_ukv, _UKV_IDX, _UKV_SGN)
    w_sb_b, w_mla_b, w_dsa_b, w_out_b = [w.astype(BF16) for w in (w_sb_out, w_mla_out, w_dsa_out, w_out)]
    gq = mla_q_norm_g[:, None, :]
    gkv = mla_kv_norm_g[:, None, :]
    rw_p = jnp.zeros((depth, d, LANES), F32).at[:, :, :N_EXPERTS].set(router_w)
    rb_p = jnp.full((depth, 1, LANES), -jnp.inf, F32).at[:, 0, :N_EXPERTS].set(router_b)
    bgu = expert_b_gu[:, :, None, :]
    bd = expert_b_down[:, :, None, :]
    fg = final_norm_g[None, :]

    xf = x.reshape(t, d)
    for l in range(depth):
        a, cl, rp, w8, gates = _inproj(xf, scale_m, sh_m, w_in_p, cos_t, sin_t, l, seq)
        o_sb = _sb_attention(a, batch, seq)
        qm, km, vm = _mla_up(cl, gq, gkv, w_uq_p, w_ukv_p, cos_t, sin_t, rp, l)
        o_mla = _mla_attention(qm, km, vm, batch, seq)
        o_dsa = _dsa_attention(a, rp, w8, batch, seq, n_select)
        xf = _merge(xf, o_sb, o_mla, o_dsa, gates, g_m, w_sb_b, w_mla_b, w_dsa_b, w_out_b, l, seq)

        hf, ti, tw, tr, counts = _router(xf, scale_f, sh_f, rw_p, rb_p, l, seq)
        pos, row_src, row_w, tile_e, tile_first, n_valid = _dispatch_tables(
            ti[:, :TOP_K], tr[:, :TOP_K], tw[:, :TOP_K], counts[0, :N_EXPERTS], tm_e)
        xs = jnp.take(hf, row_src, axis=0)
        ys = _experts(xs, row_w, tile_e, tile_first, n_valid, expert_w_gu, bgu, expert_w_down, bd, l, tm_e)
        yg = jnp.take(ys, pos.reshape(-1), axis=0).reshape(t, TOP_K * d)
        xf = _combine(xf, yg, g_f, fg, l, seq, l == depth - 1)
    return xf.reshape(batch, seq, d)
```

```python
import functools
import math

import numpy as np
import jax
import jax.numpy as jnp
from jax import lax
from jax.experimental import pallas as pl
from jax.experimental.pallas import tpu as pltpu

F32 = jnp.float32
BF16 = jnp.bfloat16
I32 = jnp.int32

D_MODEL = 1024
CHUNK = 64
CHUNK_SHIFT = 6
ROPE_THETA = 10000.0
NORM_EPS = 1e-6
SB_HEADS, SB_DIM = 6, 64
MLA_HEADS, MLA_NOPE, MLA_ROPE, MLA_V = 6, 64, 32, 64
MLA_Q_RANK, MLA_KV_RANK = 256, 128
DSA_HEADS, DSA_DIM = 4, 64
IDX_HEADS, IDX_DIM = 8, 32
DSA_TOPK_MAX = 256
N_EXPERTS, TOP_K = 32, 4
D_EXPERT = D_MODEL
SWIGLU_LIMIT = 7.0
SWIGLU_ALPHA = 1.702

LANES = 128
VMEM_LIMIT = 56 * 1024 * 1024

INT_MIN = -(2 ** 31)
NEG_BIG = -1e30
SB_UNDERFLOW = 104.0
LOG2E = math.log2(math.e)

_SPLIT = (384, 384, 384, 256, 128, 32, 256, 256, 256, 256, 32, 8, 3072)
_OFF = np.concatenate([[0], np.cumsum(_SPLIT)]).astype(np.int64)
(O_SBQ, O_SBK, O_SBV, O_CQ, O_CKV, O_KR, O_DQ, O_DK, O_DV, O_IQ, O_IK, O_IW, O_GATE, D_IN) = [
    int(v) for v in _OFF]

N_A = 1408
N_C = 384
N_R = 1152
N_W = 128
N_G = 3072
N_IN = N_A + N_C + 2 * N_R + N_W + N_G
_ROPE_KIND = (0, 0, 0, 0, 1, 1, 1, 1, 2)
_ROPE_SCALE = (DSA_DIM ** -0.5 * LOG2E,) * 2 + (1.0,) * 7


def _rot_cols(base, n_heads, d):
    half = d // 2
    idx, sgn = [], []
    for h in range(n_heads):
        for j in range(d):
            if j < half:
                idx.append(base + h * d + j + half)
                sgn.append(-1.0)
            else:
                idx.append(base + h * d + j - half)
                sgn.append(1.0)
    return idx, sgn


def _in_layout():
    idx, sgn = [], []

    def plain(base, n):
        idx.extend(range(base, base + n))
        sgn.extend([1.0] * n)

    def pad(n):
        idx.extend([0] * n)
        sgn.extend([0.0] * n)

    plain(O_DV, 256); plain(O_SBQ, 384); plain(O_SBK, 384); plain(O_SBV, 384)
    plain(O_CQ, 256); plain(O_CKV, 128)
    plain(O_DQ, 256); plain(O_DK, 256); plain(O_IQ, 256)
    for _ in range(IDX_HEADS):
        plain(O_IK, IDX_DIM)
    pad(64); plain(O_KR, 32); pad(32)
    for base, nh, d in ((O_DQ, DSA_HEADS, DSA_DIM), (O_DK, DSA_HEADS, DSA_DIM), (O_IQ, IDX_HEADS, IDX_DIM)):
        i, s = _rot_cols(base, nh, d)
        idx.extend(i); sgn.extend(s)
    i, s = _rot_cols(O_IK, 1, IDX_DIM)
    for _ in range(IDX_HEADS):
        idx.extend(i); sgn.extend(s)
    pad(64)
    i, s = _rot_cols(O_KR, 1, MLA_ROPE)
    idx.extend(i); sgn.extend(s)
    pad(32)
    plain(O_IW, IDX_HEADS); pad(N_W - IDX_HEADS)
    plain(O_GATE, N_G)
    assert len(idx) == N_IN
    return np.asarray(idx, np.int32), np.asarray(sgn, np.float32)


_IN_IDX, _IN_SGN = _in_layout()


def _uq_layout():
    per = MLA_NOPE + MLA_ROPE
    idx, sgn = [], []
    for h in range(MLA_HEADS):
        idx.extend(range(h * per, h * per + per)); sgn.extend([1.0] * per)
        idx.extend([0] * 32); sgn.extend([0.0] * 32)
    for h in range(MLA_HEADS):
        idx.extend([0] * MLA_NOPE); sgn.extend([0.0] * MLA_NOPE)
        i, s = _rot_cols(h * per + MLA_NOPE, 1, MLA_ROPE)
        idx.extend(i); sgn.extend(s)
        idx.extend([0] * 32); sgn.extend([0.0] * 32)
    return np.asarray(idx, np.int32), np.asarray(sgn, np.float32)


def _ukv_layout():
    per = MLA_NOPE + MLA_V
    idx, sgn = [], []
    for h in range(MLA_HEADS):
        idx.extend(range(h * per, h * per + MLA_NOPE)); sgn.extend([1.0] * MLA_NOPE)
        idx.extend([0] * 64); sgn.extend([0.0] * 64)
    for h in range(MLA_HEADS):
        idx.extend(range(h * per + MLA_NOPE, h * per + per)); sgn.extend([1.0] * MLA_V)
    return np.asarray(idx, np.int32), np.asarray(sgn, np.float32)


_UQ_IDX, _UQ_SGN = _uq_layout()
_UKV_IDX, _UKV_SGN = _ukv_layout()
N_QM = MLA_HEADS * LANES
N_VM = MLA_HEADS * MLA_V


def _cparams(sem):
    return pltpu.CompilerParams(dimension_semantics=sem, vmem_limit_bytes=VMEM_LIMIT)


def _dot(a, b):
    return jnp.dot(a, b, preferred_element_type=F32)


def _dot_t(a, b):
    return lax.dot_general(a, b, (((1,), (1,)), ((), ())), preferred_element_type=F32)


def _rms(x):
    return x * lax.rsqrt(jnp.mean(x * x, axis=-1, keepdims=True) + NORM_EPS)


def _ada_kernel(c_ref, w_ref, b_ref, o_ref):
    c = c_ref[...]
    sc = c * (1.0 / (1.0 + jnp.exp(-c)))
    o_ref[...] = jnp.dot(sc, w_ref[...], preferred_element_type=F32,
                         precision=lax.Precision.HIGHEST) + b_ref[...]


def _ada_mod(c, ada_w, ada_b):
    depth, d, n = ada_w.shape
    b = c.shape[0]
    rows = 8
    cp = jnp.zeros((rows, d), F32).at[:b].set(c)
    tn = 2048
    out = pl.pallas_call(
        _ada_kernel,
        grid=(depth, n // tn),
        in_specs=[
            pl.BlockSpec((rows, d), lambda l, j: (0, 0)),
            pl.BlockSpec((None, d, tn), lambda l, j: (l, 0, j)),
            pl.BlockSpec((None, 1, tn), lambda l, j: (l, 0, j)),
        ],
        out_specs=pl.BlockSpec((None, rows, tn), lambda l, j: (l, 0, j)),
        out_shape=jax.ShapeDtypeStruct((depth, rows, n), F32),
        compiler_params=_cparams(("arbitrary", "arbitrary")),
        name="ada_mod",
    )(cp, ada_w, ada_b.reshape(depth, 1, n))
    return out[:, :b]


def _inproj_kernel(x_ref, sc_ref, sh_ref, w_ref, cos_ref, sin_ref,
                   a_ref, c_ref, r_ref, w8_ref, g_ref):
    h = (_rms(x_ref[...]) * sc_ref[...] + sh_ref[...]).astype(BF16)
    o = 0
    for c0 in range(0, N_A, 256):
        c1 = min(c0 + 256, N_A)
        a_ref[:, c0:c1] = _dot(h, w_ref[:, o + c0:o + c1]).astype(BF16)
    o += N_A
    c_ref[...] = _dot(h, w_ref[:, o:o + N_C])
    o += N_C
    for j, kind in enumerate(_ROPE_KIND):
        lo = j * LANES
        y = _dot(h, w_ref[:, o + lo:o + lo + LANES])
        yr = _dot(h, w_ref[:, o + N_R + lo:o + N_R + lo + LANES])
        cs = cos_ref[:, kind * LANES:(kind + 1) * LANES]
        sn = sin_ref[:, kind * LANES:(kind + 1) * LANES]
        r = y * cs + yr * sn
        if _ROPE_SCALE[j] != 1.0:
            r = r * _ROPE_SCALE[j]
        r_ref[:, lo:lo + LANES] = r.astype(BF16)
    o += 2 * N_R
    w8_ref[...] = _dot(h, w_ref[:, o:o + N_W]) * (IDX_DIM ** -0.5 * IDX_HEADS ** -0.5)
    o += N_W
    for c0 in range(0, N_G, 512):
        z = _dot(h, w_ref[:, o + c0:o + c0 + 512])
        g_ref[:, c0:c0 + 512] = (1.0 / (1.0 + jnp.exp(-z))).astype(BF16)


def _inproj(x, scale, shift, w, cos_t, sin_t, layer, seq):
    t, d = x.shape
    tm = 512
    per = seq // tm
    return pl.pallas_call(
        _inproj_kernel,
        grid=(t // tm,),
        in_specs=[
            pl.BlockSpec((tm, d), lambda i: (i, 0)),
            pl.BlockSpec((None, None, 1, d), lambda i: (layer, i // per, 0, 0)),
            pl.BlockSpec((None, None, 1, d), lambda i: (layer, i // per, 0, 0)),
            pl.BlockSpec((None, d, N_IN), lambda i: (layer, 0, 0), pipeline_mode=pl.Buffered(1)),
            pl.BlockSpec((tm, 3 * LANES), lambda i: (i, 0)),
            pl.BlockSpec((tm, 3 * LANES), lambda i: (i, 0)),
        ],
        out_specs=[
            pl.BlockSpec((tm, N_A), lambda i: (i, 0)),
            pl.BlockSpec((tm, N_C), lambda i: (i, 0)),
            pl.BlockSpec((tm, N_R), lambda i: (i, 0)),
            pl.BlockSpec((tm, N_W), lambda i: (i, 0)),
            pl.BlockSpec((tm, N_G), lambda i: (i, 0)),
        ],
        out_shape=[
            jax.ShapeDtypeStruct((t, N_A), BF16),
            jax.ShapeDtypeStruct((t, N_C), F32),
            jax.ShapeDtypeStruct((t, N_R), BF16),
            jax.ShapeDtypeStruct((t, N_W), F32),
            jax.ShapeDtypeStruct((t, N_G), BF16),
        ],
        compiler_params=_cparams(("arbitrary",)),
        name="inproj",
    )(x, scale, shift, w, cos_t, sin_t)


def _mla_up_kernel(c_ref, gq_ref, gkv_ref, wq_ref, wkv_ref, cos_ref, sin_ref, kr_ref,
                   q_ref, k_ref, v_ref):
    c = c_ref[...]
    nq = (_rms(c[:, :MLA_Q_RANK]) * gq_ref[...]).astype(BF16)
    nkv = (_rms(c[:, MLA_Q_RANK:]) * gkv_ref[...]).astype(BF16)
    scale = (MLA_NOPE + MLA_ROPE) ** -0.5 * LOG2E
    cs = cos_ref[...] * scale
    sn = sin_ref[...] * scale
    kr = kr_ref[...].astype(F32)
    for h in range(MLA_HEADS):
        lo = h * LANES
        y = _dot(nq, wq_ref[:, lo:lo + LANES])
        yr = _dot(nq, wq_ref[:, N_QM + lo:N_QM + lo + LANES])
        q_ref[:, lo:lo + LANES] = (y * cs + yr * sn).astype(BF16)
        k_ref[:, lo:lo + LANES] = (_dot(nkv, wkv_ref[:, lo:lo + LANES]) + kr).astype(BF16)
    v_ref[...] = _dot(nkv, wkv_ref[:, N_QM:N_QM + N_VM]).astype(BF16)


def _mla_up(cl, gq, gkv, wq, wkv, cos_t, sin_t, rp, layer):
    t = cl.shape[0]
    tm = 512
    return pl.pallas_call(
        _mla_up_kernel,
        grid=(t // tm,),
        in_specs=[
            pl.BlockSpec((tm, N_C), lambda i: (i, 0)),
            pl.BlockSpec((None, 1, MLA_Q_RANK), lambda i: (layer, 0, 0)),
            pl.BlockSpec((None, 1, MLA_KV_RANK), lambda i: (layer, 0, 0)),
            pl.BlockSpec((None, MLA_Q_RANK, 2 * N_QM), lambda i: (layer, 0, 0)),
            pl.BlockSpec((None, MLA_KV_RANK, N_QM + N_VM), lambda i: (layer, 0, 0)),
            pl.BlockSpec((tm, LANES), lambda i: (i, 2)),
            pl.BlockSpec((tm, LANES), lambda i: (i, 2)),
            pl.BlockSpec((tm, LANES), lambda i: (i, 8)),
        ],
        out_specs=[
            pl.BlockSpec((tm, N_QM), lambda i: (i, 0)),
            pl.BlockSpec((tm, N_QM), lambda i: (i, 0)),
            pl.BlockSpec((tm, N_VM), lambda i: (i, 0)),
        ],
        out_shape=[
            jax.ShapeDtypeStruct((t, N_QM), BF16),
            jax.ShapeDtypeStruct((t, N_QM), BF16),
            jax.ShapeDtypeStruct((t, N_VM), BF16),
        ],
        compiler_params=_cparams(("arbitrary",)),
        name="mla_up",
    )(cl, gq, gkv, wq, wkv, cos_t, sin_t, rp)


def _sb_kernel(q_ref, k_ref, v_ref, o_ref, *, tq):
    qi = pl.program_id(2)
    lane = lax.broadcasted_iota(I32, (1, LANES), 1)
    row = lax.broadcasted_iota(I32, (tq, tq), 0)
    col = lax.broadcasted_iota(I32, (tq, tq), 1)
    causal = col < row
    tri = jnp.where(row > col, 1.0, 0.0).astype(BF16)
    q = q_ref[...]
    outs = []
    for h in range(2):
        head = (lane >= h * SB_DIM) & (lane < (h + 1) * SB_DIM)
        qh = jnp.where(head, q, jnp.zeros_like(q)) * jnp.asarray(SB_DIM ** -0.5, BF16)

        def block(j, remain, acc, diagonal, qh=qh):
            start = pl.multiple_of(j * tq, tq)
            kb = k_ref[pl.ds(start, tq), :]
            vb = v_ref[pl.ds(start, tq), :]
            z = _dot_t(qh, kb)
            soft = jnp.log1p(jnp.exp(-jnp.abs(z)))
            log_stay = -(jnp.maximum(z, 0.0) + soft)
            if diagonal:
                log_stay = jnp.where(causal, log_stay, 0.0)
            hi = log_stay.astype(BF16)
            lo = (log_stay - hi.astype(F32)).astype(BF16)
            later = _dot(hi, tri) + _dot(lo, tri)
            log_a = (z + log_stay) + later + remain
            a = jnp.exp(log_a)
            if diagonal:
                a = jnp.where(causal, a, 0.0)
            acc = acc + _dot(a.astype(BF16), vb)
            remain = remain + jnp.sum(log_stay, axis=-1, keepdims=True)
            return remain, acc

        remain, acc = block(qi, jnp.zeros((tq, 1), F32), jnp.zeros((tq, LANES), F32), True)

        def cond(carry):
            j, remain, _ = carry
            return jnp.logical_and(j >= 0, jnp.max(remain) > -SB_UNDERFLOW)

        def body(carry, block=block):
            j, remain, acc = carry
            remain, acc = block(j, remain, acc, False)
            return j - 1, remain, acc

        _, _, acc = lax.while_loop(cond, body, (qi - 1, remain, acc))
        outs.append(acc)
    o_ref[...] = jnp.where(lane < SB_DIM, outs[0], outs[1]).astype(BF16)


def _sb_attention(a, batch, seq):
    t = a.shape[0]
    tq = 256
    nq = seq // tq
    pairs = SB_HEADS // 2
    return pl.pallas_call(
        functools.partial(_sb_kernel, tq=tq),
        grid=(batch, pairs, nq),
        in_specs=[
            pl.BlockSpec((tq, LANES), lambda b, p, i: (b * nq + i, 2 + p)),
            pl.BlockSpec((seq, LANES), lambda b, p, i: (b, 2 + pairs + p)),
            pl.BlockSpec((seq, LANES), lambda b, p, i: (b, 2 + 2 * pairs + p)),
        ],
        out_specs=pl.BlockSpec((tq, LANES), lambda b, p, i: (b * nq + i, p)),
        out_shape=jax.ShapeDtypeStruct((t, SB_HEADS * SB_DIM), BF16),
        compiler_params=_cparams(("arbitrary", "arbitrary", "arbitrary")),
        name="sb_attention",
    )(a, a, a)


def _softmax_step(s, vb, h, acc_ref, l_ref, m_ref):
    reps = s.shape[1] // LANES
    m_old = m_ref[h]
    m_new = jnp.maximum(m_old, jnp.max(s, axis=-1, keepdims=True))
    alpha = jnp.exp2(m_old - m_new)
    p = jnp.exp2(s - jnp.concatenate([m_new] * reps, axis=1))
    part = p[:, 0:LANES]
    for c in range(1, reps):
        part = part + p[:, c * LANES:(c + 1) * LANES]
    l_ref[h] = alpha * l_ref[h] + part
    acc_ref[h] = alpha * acc_ref[h] + _dot(p.astype(BF16), vb)
    m_ref[h] = m_new


def _softmax_init(acc_ref, l_ref, m_ref):
    acc_ref[...] = jnp.zeros_like(acc_ref)
    l_ref[...] = jnp.zeros_like(l_ref)
    m_ref[...] = jnp.full_like(m_ref, NEG_BIG)


def _softmax_out(h, acc_ref, l_ref):
    return acc_ref[h] / jnp.sum(l_ref[h], axis=-1, keepdims=True)


def _mla_kernel(q_ref, k_ref, v_ref, o_ref, acc_ref, l_ref, m_ref, *, tq, tkb):
    qi = pl.program_id(2)
    lane = lax.broadcasted_iota(I32, (1, LANES), 1)
    row = lax.broadcasted_iota(I32, (tq, tq), 0)
    col = lax.broadcasted_iota(I32, (tq, tq), 1)
    visible = (col >> CHUNK_SHIFT) <= (row >> CHUNK_SHIFT)
    q = q_ref[...]
    _softmax_init(acc_ref, l_ref, m_ref)

    def step(start, tk, diagonal):
        kb = k_ref[pl.ds(start, tk), :]
        vb = v_ref[pl.ds(start, tk), :]
        for h in range(2):
            s = _dot_t(q[:, h * LANES:(h + 1) * LANES], kb[:, h * LANES:(h + 1) * LANES])
            if diagonal:
                s = jnp.where(visible, s, NEG_BIG)
            _softmax_step(s, vb, h, acc_ref, l_ref, m_ref)

    n_wide = (qi * tq) // tkb

    def wide(j, c):
        step(pl.multiple_of(j * tkb, tkb), tkb, False)
        return c

    def narrow(j, c):
        step(pl.multiple_of(j * tq, tq), tq, False)
        return c

    lax.fori_loop(0, n_wide, wide, 0)
    lax.fori_loop(n_wide * (tkb // tq), qi, narrow, 0)
    step(pl.multiple_of(qi * tq, tq), tq, True)
    o_ref[...] = jnp.where(lane < MLA_V, _softmax_out(0, acc_ref, l_ref),
                           _softmax_out(1, acc_ref, l_ref)).astype(BF16)


def _mla_attention(qm, km, vm, batch, seq):
    t = qm.shape[0]
    tq = min(512, seq)
    tkb = min(1024, seq)
    nq = seq // tq
    pairs = MLA_HEADS // 2
    stat = pltpu.VMEM((2, tq, LANES), F32)
    return pl.pallas_call(
        functools.partial(_mla_kernel, tq=tq, tkb=tkb),
        grid=(batch, pairs, nq),
        in_specs=[
            pl.BlockSpec((tq, 2 * LANES), lambda b, p, i: (b * nq + i, p)),
            pl.BlockSpec((seq, 2 * LANES), lambda b, p, i: (b, p)),
            pl.BlockSpec((seq, LANES), lambda b, p, i: (b, p)),
        ],
        out_specs=pl.BlockSpec((tq, LANES), lambda b, p, i: (b * nq + i, p)),
        out_shape=jax.ShapeDtypeStruct((t, N_VM), BF16),
        scratch_shapes=[stat, stat, stat],
        compiler_params=_cparams(("arbitrary", "arbitrary", "arbitrary")),
        name="mla_attention",
    )(qm, km, vm)


SEARCH_INTERP_STEPS = 10
SEARCH_MAX_STEPS = SEARCH_INTERP_STEPS + 34


def _dsa_kernel(qd_ref, qx_ref, w_ref, kx_ref, kd_ref, vd_ref, o_ref,
                key_ref, qs_ref, wr_ref, top_ref, acc_ref, l_ref, m_ref, *, tq, tk, n_select):
    it = pl.program_id(1)
    t0 = it * tq
    last = t0 // tk
    reps = tk // LANES
    lane = lax.broadcasted_iota(I32, (1, LANES), 1)
    lane2 = lax.broadcasted_iota(I32, (1, 2 * LANES), 1)
    rowid = t0 + lax.broadcasted_iota(I32, (tq, 1), 0)
    row_chunk = rowid >> CHUNK_SHIFT

    qx = qx_ref[...]
    w = w_ref[...]
    for h in range(IDX_HEADS):
        head = (lane2 >= h * IDX_DIM) & (lane2 < (h + 1) * IDX_DIM)
        qs_ref[h * tq:(h + 1) * tq, :] = jnp.where(head, qx, jnp.zeros_like(qx))
        wr_ref[h] = jnp.broadcast_to(w[:, h:h + 1], (tq, LANES))
    top_ref[...] = jnp.full_like(top_ref, INT_MIN)

    def score_block(j, diagonal):
        start = pl.multiple_of(j * tk, tk)
        d = _dot_t(qs_ref[...], kx_ref[pl.ds(start, tk), :])
        score = jnp.zeros((tq, tk), F32)
        for h in range(IDX_HEADS):
            wh = jnp.concatenate([wr_ref[h]] * reps, axis=1)
            score = score + wh * jnp.maximum(d[h * tq:(h + 1) * tq], 0.0)
        score = jnp.where(score == 0.0, 0.0, score)
        bits = pltpu.bitcast(score, I32)
        key = jnp.where(bits < 0, bits ^ jnp.int32(0x7FFFFFFF), bits)
        if diagonal:
            col_chunk = (start + lax.broadcasted_iota(I32, (1, tk), 1)) >> CHUNK_SHIFT
            key = jnp.where(col_chunk <= row_chunk, key, INT_MIN)
        key_ref[j] = key
        t1, t2 = top_ref[0], top_ref[1]
        for c in range(reps):
            x = key[:, c * LANES:(c + 1) * LANES]
            t2 = jnp.maximum(t2, jnp.minimum(t1, x))
            t1 = jnp.maximum(t1, x)
        top_ref[0] = t1
        top_ref[1] = t2

    def score_body(j, c):
        score_block(j, False)
        return c

    lax.fori_loop(0, last, score_body, 0)
    score_block(last, True)
    n_blocks = last + 1

    def count_ge(cand):
        def body(j, acc):
            ge = jnp.where(key_ref[j] >= cand, 1, 0)
            part = ge[:, 0:LANES]
            for c in range(1, reps):
                part = part + ge[:, c * LANES:(c + 1) * LANES]
            return acc + part
        acc = lax.fori_loop(0, n_blocks, body, jnp.zeros((tq, LANES), I32))
        return jnp.sum(acc, axis=-1, keepdims=True)

    few = (row_chunk + 1) * CHUNK <= n_select
    lo = jnp.min(top_ref[1], axis=-1, keepdims=True)
    hi = jnp.max(top_ref[0], axis=-1, keepdims=True) + 1
    c_lo = count_ge(lo)
    c_hi = jnp.zeros((tq, 1), I32)

    def bracket_width(lo, hi):
        w = hi - lo
        return jnp.where(w < 0, hi.astype(F32) - lo.astype(F32), w.astype(F32))

    def active_rows(lo, c_lo, hi):
        return jnp.logical_and(jnp.logical_not(few), jnp.logical_and(c_lo > n_select, bracket_width(lo, hi) > 1.5))

    def search_cond(carry):
        step, lo, c_lo, hi, c_hi = carry
        busy = jnp.max(jnp.where(active_rows(lo, c_lo, hi), 1, 0))
        return jnp.logical_and(step < SEARCH_MAX_STEPS, busy > 0)

    def search_body(carry):
        step, lo, c_lo, hi, c_hi = carry
        active = active_rows(lo, c_lo, hi)
        width = bracket_width(lo, hi)
        frac = (c_lo.astype(F32) - (n_select - 0.5)) / (c_lo - c_hi).astype(F32)
        frac = jnp.where(step < SEARCH_INTERP_STEPS, jnp.clip(frac, 1.0 / 16, 15.0 / 16), 0.5)
        jump = jnp.clip(jnp.floor(frac * width), 1.0, 2.0 ** 31 - 256.0).astype(I32)
        cand = jnp.minimum(jnp.maximum(lo + jump, lo + 1), hi - 1)
        c = count_ge(cand)
        up = jnp.logical_and(active, c >= n_select)
        down = jnp.logical_and(active, c < n_select)
        lo = jnp.where(up, cand, lo)
        c_lo = jnp.where(up, c, c_lo)
        hi = jnp.where(down, cand, hi)
        c_hi = jnp.where(down, c, c_hi)
        return step + 1, lo, c_lo, hi, c_hi

    _, lo, c_lo, hi, c_hi = lax.while_loop(search_cond, search_body, (jnp.int32(0), lo, c_lo, hi, c_hi))
    theta = jnp.where(few, INT_MIN, lo)

    tied = jnp.logical_and(jnp.logical_not(few), c_lo > n_select)

    @pl.when(jnp.max(jnp.where(tied, 1, 0)) > 0)
    def _():
        need = (n_select - c_hi).astype(F32)
        r = lax.broadcasted_iota(I32, (tk, tk), 0)
        c = lax.broadcasted_iota(I32, (tk, tk), 1)
        before = jnp.where(r < c, 1.0, 0.0).astype(BF16)

        def body(j, seen):
            key = key_ref[j]
            eq = jnp.logical_and(key == theta, tied)
            eqf = jnp.where(eq, 1.0, 0.0)
            rank = seen + _dot(eqf.astype(BF16), before)
            key_ref[j] = jnp.where(jnp.logical_and(eq, rank >= need), INT_MIN, key)
            return seen + jnp.sum(eqf, axis=-1, keepdims=True)

        lax.fori_loop(0, n_blocks, body, jnp.zeros((tq, 1), F32))

    theta = jnp.maximum(theta, INT_MIN + 1)

    qd = qd_ref[...]
    q_heads = []
    for h in range(DSA_HEADS):
        blk = qd[:, (h // 2) * LANES:(h // 2 + 1) * LANES]
        head = (lane >= (h % 2) * DSA_DIM) & (lane < (h % 2 + 1) * DSA_DIM)
        q_heads.append(jnp.where(head, blk, jnp.zeros_like(blk)))
    _softmax_init(acc_ref, l_ref, m_ref)

    def attend(j, c):
        start = pl.multiple_of(j * tk, tk)
        sel = key_ref[j] >= theta
        kb = kd_ref[pl.ds(start, tk), :]
        vb = vd_ref[pl.ds(start, tk), :]
        for h in range(DSA_HEADS):
            p0 = (h // 2) * LANES
            s = jnp.where(sel, _dot_t(q_heads[h], kb[:, p0:p0 + LANES]), NEG_BIG)
            _softmax_step(s, vb[:, p0:p0 + LANES], h, acc_ref, l_ref, m_ref)
        return c

    lax.fori_loop(0, n_blocks, attend, 0)
    for p in range(DSA_HEADS // 2):
        o_ref[:, p * LANES:(p + 1) * LANES] = jnp.where(
            lane < DSA_DIM, _softmax_out(2 * p, acc_ref, l_ref), _softmax_out(2 * p + 1, acc_ref, l_ref)).astype(BF16)


def _dsa_attention(a, rp, w8, batch, seq, n_select):
    t = a.shape[0]
    tq = 256
    tk = min(1024, seq)
    nq = seq // tq
    width = DSA_HEADS * DSA_DIM
    keys = lambda col: pl.BlockSpec((seq, width), lambda b, i: (b, col), pipeline_mode=pl.Buffered(1))
    stat = lambda n: pltpu.VMEM((n, tq, LANES), F32)
    return pl.pallas_call(
        functools.partial(_dsa_kernel, tq=tq, tk=tk, n_select=n_select),
        grid=(batch, nq),
        in_specs=[
            pl.BlockSpec((tq, width), lambda b, i: (b * nq + i, 0)),
            pl.BlockSpec((tq, width), lambda b, i: (b * nq + i, 2)),
            pl.BlockSpec((tq, N_W), lambda b, i: (b * nq + i, 0)),
            keys(3),
            keys(1),
            keys(0),
        ],
        out_specs=pl.BlockSpec((tq, width), lambda b, i: (b * nq + i, 0)),
        out_shape=jax.ShapeDtypeStruct((t, width), BF16),
        scratch_shapes=[
            pltpu.VMEM((seq // tk, tq, tk), I32),
            pltpu.VMEM((IDX_HEADS * tq, width), BF16),
            stat(IDX_HEADS),
            pltpu.VMEM((2, tq, LANES), I32),
            stat(DSA_HEADS), stat(DSA_HEADS), stat(DSA_HEADS),
        ],
        compiler_params=_cparams(("arbitrary", "arbitrary")),
        name="dsa_attention",
    )(rp, rp, w8, rp, rp, a)


def _merge_kernel(x_ref, osb_ref, omla_ref, odsa_ref, g_ref, gm_ref, wsb_ref, wmla_ref, wdsa_ref, wout_ref,
                  o_ref):
    d = D_MODEL
    merged = (g_ref[:, 0:d].astype(F32) * _dot(osb_ref[...], wsb_ref[...])
              + g_ref[:, d:2 * d].astype(F32) * _dot(omla_ref[...], wmla_ref[...])
              + g_ref[:, 2 * d:3 * d].astype(F32) * _dot(odsa_ref[...], wdsa_ref[...]))
    o_ref[...] = x_ref[...] + gm_ref[...] * _dot(merged.astype(BF16), wout_ref[...])


def _merge(x, osb, omla, odsa, gates, gm, wsb, wmla, wdsa, wout, layer, seq):
    t, d = x.shape
    tm = 512
    per = seq // tm
    row = lambda i: (i, 0)
    wspec = lambda k: pl.BlockSpec((None, k, d), lambda i: (layer, 0, 0))
    return pl.pallas_call(
        _merge_kernel,
        grid=(t // tm,),
        in_specs=[
            pl.BlockSpec((tm, d), row),
            pl.BlockSpec((tm, osb.shape[1]), row),
            pl.BlockSpec((tm, omla.shape[1]), row),
            pl.BlockSpec((tm, odsa.shape[1]), row),
            pl.BlockSpec((tm, N_G), row),
            pl.BlockSpec((None, None, 1, d), lambda i: (layer, i // per, 0, 0)),
            wspec(osb.shape[1]), wspec(omla.shape[1]), wspec(odsa.shape[1]), wspec(d),
        ],
        out_specs=pl.BlockSpec((tm, d), row),
        out_shape=jax.ShapeDtypeStruct((t, d), F32),
        compiler_params=_cparams(("arbitrary",)),
        name="merge",
    )(x, osb, omla, odsa, gates, gm, wsb, wmla, wdsa, wout)


def _router_kernel(x_ref, sc_ref, sh_ref, rw_ref, rb_ref, h_ref, ti_ref, tw_ref, tr_ref, cnt_ref, run_ref, *, tm):
    @pl.when(pl.program_id(0) == 0)
    def _():
        run_ref[...] = jnp.zeros_like(run_ref)

    h = _rms(x_ref[...]) * sc_ref[...] + sh_ref[...]
    h_ref[...] = h.astype(BF16)
    logits = jnp.dot(h, rw_ref[...], preferred_element_type=F32, precision=lax.Precision.HIGHEST) + rb_ref[...]
    lane = lax.broadcasted_iota(I32, (tm, LANES), 1)
    work = logits
    vals, hots = [], []
    for _ in range(TOP_K):
        m = jnp.max(work, axis=-1, keepdims=True)
        first = jnp.min(jnp.where(work == m, lane, LANES), axis=-1, keepdims=True)
        hot = lane == first
        vals.append(m)
        hots.append(hot)
        work = jnp.where(hot, -jnp.inf, work)
    exps = [jnp.exp(v - vals[0]) for v in vals]
    denom = exps[0] + exps[1] + exps[2] + exps[3]
    chosen = jnp.zeros((tm, LANES), F32)
    for hot in hots:
        chosen = chosen + jnp.where(hot, 1.0, 0.0)
    r = lax.broadcasted_iota(I32, (tm, tm), 0)
    c = lax.broadcasted_iota(I32, (tm, tm), 1)
    earlier = jnp.where(c < r, 1.0, 0.0).astype(BF16)
    rank_all = _dot(earlier, chosen.astype(BF16)) + run_ref[...]
    ti = jnp.zeros((tm, LANES), I32)
    tw = jnp.zeros((tm, LANES), F32)
    tr = jnp.zeros((tm, LANES), I32)
    for k in range(TOP_K):
        e_k = jnp.sum(jnp.where(hots[k], lane, 0), axis=-1, keepdims=True)
        r_k = jnp.sum(jnp.where(hots[k], rank_all, 0.0), axis=-1, keepdims=True).astype(I32)
        ti = jnp.where(lane == k, e_k, ti)
        tw = jnp.where(lane == k, exps[k] / denom, tw)
        tr = jnp.where(lane == k, r_k, tr)
    ti_ref[...] = ti
    tw_ref[...] = tw
    tr_ref[...] = tr
    run_ref[...] = run_ref[...] + jnp.sum(chosen, axis=0, keepdims=True)
    cnt_ref[...] = run_ref[...]


def _router(x, scale, shift, rw, rb, layer, seq):
    t, d = x.shape
    tm = 256
    per = seq // tm
    row = lambda i: (i, 0)
    return pl.pallas_call(
        functools.partial(_router_kernel, tm=tm),
        grid=(t // tm,),
        in_specs=[
            pl.BlockSpec((tm, d), row),
            pl.BlockSpec((None, None, 1, d), lambda i: (layer, i // per, 0, 0)),
            pl.BlockSpec((None, None, 1, d), lambda i: (layer, i // per, 0, 0)),
            pl.BlockSpec((None, d, LANES), lambda i: (layer, 0, 0)),
            pl.BlockSpec((None, 1, LANES), lambda i: (layer, 0, 0)),
        ],
        out_specs=[
            pl.BlockSpec((tm, d), row),
            pl.BlockSpec((tm, LANES), row),
            pl.BlockSpec((tm, LANES), row),
            pl.BlockSpec((tm, LANES), row),
            pl.BlockSpec((1, LANES), lambda i: (0, 0)),
        ],
        out_shape=[
            jax.ShapeDtypeStruct((t, d), BF16),
            jax.ShapeDtypeStruct((t, LANES), I32),
            jax.ShapeDtypeStruct((t, LANES), F32),
            jax.ShapeDtypeStruct((t, LANES), I32),
            jax.ShapeDtypeStruct((1, LANES), F32),
        ],
        scratch_shapes=[pltpu.VMEM((1, LANES), F32)],
        compiler_params=_cparams(("arbitrary",)),
        name="router",
    )(x, scale, shift, rw, rb)


def _expert_kernel(te_ref, tf_ref, nv_ref, x_ref, wgu_ref, bgu_ref, wd_ref, bd_ref, o_ref,
                   wgu_bf, wd_bf):
    i = pl.program_id(0)

    @pl.when(i >= nv_ref[0])
    def _():
        o_ref[...] = jnp.zeros_like(o_ref)

    @pl.when(i < nv_ref[0])
    def _():
        @pl.when(tf_ref[i] == 1)
        def _():
            wgu_bf[...] = wgu_ref[...].astype(BF16)
            wd_bf[...] = wd_ref[...].astype(BF16)

        gu = _dot(x_ref[...], wgu_bf[...]) + bgu_ref[...]
        gate = jnp.minimum(gu[:, :D_EXPERT], SWIGLU_LIMIT)
        up = jnp.clip(gu[:, D_EXPERT:], -SWIGLU_LIMIT, SWIGLU_LIMIT)
        act = (up + 1.0) * (gate * (1.0 / (1.0 + jnp.exp(-SWIGLU_ALPHA * gate))))
        o_ref[...] = _dot(act.astype(BF16), wd_bf[...]) + bd_ref[...]


def _experts(xs, tile_e, tile_first, n_valid, wgu, bgu, wd, bd, layer, tm):
    p, d = xs.shape
    n_tiles = p // tm
    grid_spec = pltpu.PrefetchScalarGridSpec(
        num_scalar_prefetch=3,
        grid=(n_tiles,),
        in_specs=[
            pl.BlockSpec((tm, d), lambda i, te, tf, nv: (i, 0)),
            pl.BlockSpec((None, None, d, 2 * D_EXPERT), lambda i, te, tf, nv: (layer, te[i], 0, 0)),
            pl.BlockSpec((None, None, 1, 2 * D_EXPERT), lambda i, te, tf, nv: (layer, te[i], 0, 0)),
            pl.BlockSpec((None, None, D_EXPERT, d), lambda i, te, tf, nv: (layer, te[i], 0, 0)),
            pl.BlockSpec((None, None, 1, d), lambda i, te, tf, nv: (layer, te[i], 0, 0)),
        ],
        out_specs=pl.BlockSpec((tm, d), lambda i, te, tf, nv: (i, 0)),
        scratch_shapes=[pltpu.VMEM((d, 2 * D_EXPERT), BF16), pltpu.VMEM((D_EXPERT, d), BF16)],
    )
    return pl.pallas_call(
        _expert_kernel,
        grid_spec=grid_spec,
        out_shape=jax.ShapeDtypeStruct((p, d), F32),
        compiler_params=_cparams(("arbitrary",)),
        name="experts",
    )(tile_e, tile_first, n_valid, xs, wgu, bgu, wd, bd)


def _combine_kernel(x_ref, y0_ref, y1_ref, y2_ref, y3_ref, tw_ref, g_ref, fg_ref, o_ref, *, final):
    tw = tw_ref[...]
    y = tw[:, 0:1] * y0_ref[...]
    for k, y_ref in enumerate((y1_ref, y2_ref, y3_ref), start=1):
        y = y + tw[:, k:k + 1] * y_ref[...]
    x = x_ref[...] + g_ref[...] * y
    if final:
        x = _rms(x) * fg_ref[...]
    o_ref[...] = x


def _combine(x, yg, tw, gf, final_g, layer, seq, final):
    t, d = x.shape
    tm = 256
    nt = t // tm
    per = seq // tm
    slot = lambda k: pl.BlockSpec((tm, d), lambda i: (k * nt + i, 0))
    return pl.pallas_call(
        functools.partial(_combine_kernel, final=final),
        grid=(nt,),
        in_specs=[
            pl.BlockSpec((tm, d), lambda i: (i, 0)),
            slot(0), slot(1), slot(2), slot(3),
            pl.BlockSpec((tm, LANES), lambda i: (i, 0)),
            pl.BlockSpec((None, None, 1, d), lambda i: (layer, i // per, 0, 0)),
            pl.BlockSpec((1, d), lambda i: (0, 0)),
        ],
        out_specs=pl.BlockSpec((tm, d), lambda i: (i, 0)),
        out_shape=jax.ShapeDtypeStruct((t, d), F32),
        compiler_params=_cparams(("arbitrary",)),
        name="combine",
    )(x, yg, yg, yg, yg, tw, gf, final_g)


def _rope_tables(positions):
    pos = positions.reshape(-1).astype(F32)

    def cs(dim):
        inv_freq = ROPE_THETA ** (-jnp.arange(0, dim, 2, dtype=F32) / dim)
        ang = pos[:, None] * inv_freq
        return jnp.cos(ang), jnp.sin(ang)

    c_d, s_d = cs(DSA_DIM)
    c_i, s_i = cs(IDX_DIM)
    c_m, s_m = cs(MLA_ROPE)
    t = pos.shape[0]
    ones, zeros = jnp.ones((t, 64), F32), jnp.zeros((t, 64), F32)
    cos_t = jnp.concatenate([jnp.tile(c_d, (1, 4)), jnp.tile(c_i, (1, 8)),
                             ones, c_m, c_m, ones[:, :32]], axis=1)
    sin_t = jnp.concatenate([jnp.tile(s_d, (1, 4)), jnp.tile(s_i, (1, 8)),
                             zeros, s_m, s_m, zeros[:, :32]], axis=1)
    return cos_t, sin_t


def _permute_cols(w, idx, sgn):
    return (jnp.take(w, jnp.asarray(idx), axis=-1) * jnp.asarray(sgn)).astype(BF16)


def _dispatch_tables(ti, tr, counts, tm):
    t = ti.shape[0]
    p = t * TOP_K + N_EXPERTS * tm
    n_tiles = p // tm
    cnt = counts.astype(I32)
    padded = ((cnt + tm - 1) // tm) * tm
    ends = jnp.cumsum(padded)
    starts = ends - padded
    pos = starts[ti] + tr
    tok = jnp.repeat(jnp.arange(t, dtype=I32), TOP_K)
    row_src = jnp.zeros((p,), I32).at[pos.reshape(-1)].set(tok)
    tile_start = jnp.arange(n_tiles, dtype=I32) * tm
    n_valid = ends[-1] // tm
    tile_e = jnp.sum((ends[None, :] <= tile_start[:, None]).astype(I32), axis=1)
    tile_e = jnp.minimum(tile_e, N_EXPERTS - 1)
    last_e = jnp.max(jnp.where(tile_start < ends[-1], tile_e, 0))
    tile_e = jnp.where(tile_start < ends[-1], tile_e, last_e)
    tile_first = jnp.concatenate([jnp.ones((1,), I32), (tile_e[1:] != tile_e[:-1]).astype(I32)])
    return pos, row_src, tile_e, tile_first, n_valid.astype(I32).reshape(1)


def kernel(x, c, positions, ada_w, ada_b, norm_mix_g, w_in, mla_q_norm_g, mla_kv_norm_g, mla_w_uq, mla_w_ukv,
           w_sb_out, w_mla_out, w_dsa_out, w_out, norm_ffn_g, router_w, router_b, expert_w_gu, expert_b_gu,
           expert_w_down, expert_b_down, final_norm_g):
    batch, seq, d = x.shape
    depth = ada_w.shape[0]
    t = batch * seq
    n_select = min(DSA_TOPK_MAX, seq // 4)
    tm_e = 256

    mod = _ada_mod(c, ada_w, ada_b)
    sh_m, sc_m, g_m, sh_f, sc_f, g_f = [m[:, :, None, :] for m in jnp.split(mod, 6, axis=-1)]
    scale_m = norm_mix_g[:, None, None, :] * (1.0 + sc_m)
    scale_f = norm_ffn_g[:, None, None, :] * (1.0 + sc_f)

    cos_t, sin_t = _rope_tables(positions)
    w_in_p = _permute_cols(w_in, _IN_IDX, _IN_SGN)
    w_uq_p = _permute_cols(mla_w_uq, _UQ_IDX, _UQ_SGN)
    w_ukv_p = _permute_cols(mla_w_ukv, _UKV_IDX, _UKV_SGN)
    w_sb_b, w_mla_b, w_dsa_b, w_out_b = [w.astype(BF16) for w in (w_sb_out, w_mla_out, w_dsa_out, w_out)]
    gq = mla_q_norm_g[:, None, :]
    gkv = mla_kv_norm_g[:, None, :]
    rw_p = jnp.zeros((depth, d, LANES), F32).at[:, :, :N_EXPERTS].set(router_w)
    rb_p = jnp.full((depth, 1, LANES), -jnp.inf, F32).at[:, 0, :N_EXPERTS].set(router_b)
    bgu = expert_b_gu[:, :, None, :]
    bd = expert_b_down[:, :, None, :]
    fg = final_norm_g[None, :]

    xf = x.reshape(t, d)
    for l in range(depth):
        a, cl, rp, w8, gates = _inproj(xf, scale_m, sh_m, w_in_p, cos_t, sin_t, l, seq)
        o_sb = _sb_attention(a, batch, seq)
        qm, km, vm = _mla_up(cl, gq, gkv, w_uq_p, w_ukv_p, cos_t, sin_t, rp, l)
        o_mla = _mla_attention(qm, km, vm, batch, seq)
        o_dsa = _dsa_attention(a, rp, w8, batch, seq, n_select)
        xf = _merge(xf, o_sb, o_mla, o_dsa, gates, g_m, w_sb_b, w_mla_b, w_dsa_b, w_out_b, l, seq)

        hf, ti, tw, tr, counts = _router(xf, scale_f, sh_f, rw_p, rb_p, l, seq)
        pos, row_src, tile_e, tile_first, n_valid = _dispatch_tables(
            ti[:, :TOP_K], tr[:, :TOP_K], counts[0, :N_EXPERTS], tm_e)
        xs = jnp.take(hf, row_src, axis=0)
        ys = _experts(xs, tile_e, tile_first, n_valid, expert_w_gu, bgu, expert_w_down, bd, l, tm_e)
        yg = jnp.take(ys, pos.T.reshape(-1), axis=0)
        xf = _combine(xf, yg, tw, g_f, fg, l, seq, l == depth - 1)
    return xf.reshape(batch, seq, d)
```

```python
import functools
import math

import numpy as np
import jax
import jax.numpy as jnp
from jax import lax
from jax.experimental import pallas as pl
from jax.experimental.pallas import tpu as pltpu

F32 = jnp.float32
BF16 = jnp.bfloat16
I32 = jnp.int32

D_MODEL = 1024
CHUNK = 64
CHUNK_SHIFT = 6
ROPE_THETA = 10000.0
NORM_EPS = 1e-6
SB_HEADS, SB_DIM = 6, 64
MLA_HEADS, MLA_NOPE, MLA_ROPE, MLA_V = 6, 64, 32, 64
MLA_Q_RANK, MLA_KV_RANK = 256, 128
DSA_HEADS, DSA_DIM = 4, 64
IDX_HEADS, IDX_DIM = 8, 32
DSA_TOPK_MAX = 256
N_EXPERTS, TOP_K = 32, 4
D_EXPERT = D_MODEL
SWIGLU_LIMIT = 7.0
SWIGLU_ALPHA = 1.702

LANES = 128
VMEM_LIMIT = 56 * 1024 * 1024

INT_MIN = -(2 ** 31)
NEG_BIG = -1e30
SB_UNDERFLOW = 104.0
LOG2E = math.log2(math.e)

_SPLIT = (384, 384, 384, 256, 128, 32, 256, 256, 256, 256, 32, 8, 3072)
_OFF = np.concatenate([[0], np.cumsum(_SPLIT)]).astype(np.int64)
(O_SBQ, O_SBK, O_SBV, O_CQ, O_CKV, O_KR, O_DQ, O_DK, O_DV, O_IQ, O_IK, O_IW, O_GATE, D_IN) = [
    int(v) for v in _OFF]

N_A = 1408
N_C = 384
N_R = 1152
N_W = 128
N_G = 3072
N_IN = N_A + N_C + 2 * N_R + N_W + N_G
_ROPE_KIND = (0, 0, 0, 0, 1, 1, 1, 1, 2)
_ROPE_SCALE = (DSA_DIM ** -0.5 * LOG2E,) * 2 + (1.0,) * 7


def _rot_cols(base, n_heads, d):
    half = d // 2
    idx, sgn = [], []
    for h in range(n_heads):
        for j in range(d):
            if j < half:
                idx.append(base + h * d + j + half)
                sgn.append(-1.0)
            else:
                idx.append(base + h * d + j - half)
                sgn.append(1.0)
    return idx, sgn


def _in_layout():
    idx, sgn = [], []

    def plain(base, n):
        idx.extend(range(base, base + n))
        sgn.extend([1.0] * n)

    def pad(n):
        idx.extend([0] * n)
        sgn.extend([0.0] * n)

    plain(O_DV, 256); plain(O_SBQ, 384); plain(O_SBK, 384); plain(O_SBV, 384)
    plain(O_CQ, 256); plain(O_CKV, 128)
    plain(O_DQ, 256); plain(O_DK, 256); plain(O_IQ, 256)
    for _ in range(IDX_HEADS):
        plain(O_IK, IDX_DIM)
    pad(64); plain(O_KR, 32); pad(32)
    for base, nh, d in ((O_DQ, DSA_HEADS, DSA_DIM), (O_DK, DSA_HEADS, DSA_DIM), (O_IQ, IDX_HEADS, IDX_DIM)):
        i, s = _rot_cols(base, nh, d)
        idx.extend(i); sgn.extend(s)
    i, s = _rot_cols(O_IK, 1, IDX_DIM)
    for _ in range(IDX_HEADS):
        idx.extend(i); sgn.extend(s)
    pad(64)
    i, s = _rot_cols(O_KR, 1, MLA_ROPE)
    idx.extend(i); sgn.extend(s)
    pad(32)
    plain(O_IW, IDX_HEADS); pad(N_W - IDX_HEADS)
    plain(O_GATE, N_G)
    assert len(idx) == N_IN
    return np.asarray(idx, np.int32), np.asarray(sgn, np.float32)


_IN_IDX, _IN_SGN = _in_layout()


def _uq_layout():
    per = MLA_NOPE + MLA_ROPE
    idx, sgn = [], []
    for h in range(MLA_HEADS):
        idx.extend(range(h * per, h * per + per)); sgn.extend([1.0] * per)
        idx.extend([0] * 32); sgn.extend([0.0] * 32)
    for h in range(MLA_HEADS):
        idx.extend([0] * MLA_NOPE); sgn.extend([0.0] * MLA_NOPE)
        i, s = _rot_cols(h * per + MLA_NOPE, 1, MLA_ROPE)
        idx.extend(i); sgn.extend(s)
        idx.extend([0] * 32); sgn.extend([0.0] * 32)
    return np.asarray(idx, np.int32), np.asarray(sgn, np.float32)


def _ukv_layout():
    per = MLA_NOPE + MLA_V
    idx, sgn = [], []
    for h in range(MLA_HEADS):
        idx.extend(range(h * per, h * per + MLA_NOPE)); sgn.extend([1.0] * MLA_NOPE)
        idx.extend([0] * 64); sgn.extend([0.0] * 64)
    for h in range(MLA_HEADS):
        idx.extend(range(h * per + MLA_NOPE, h * per + per)); sgn.extend([1.0] * MLA_V)
    return np.asarray(idx, np.int32), np.asarray(sgn, np.float32)


_UQ_IDX, _UQ_SGN = _uq_layout()
_UKV_IDX, _UKV_SGN = _ukv_layout()
N_QM = MLA_HEADS * LANES
N_VM = MLA_HEADS * MLA_V


def _cparams(sem):
    return pltpu.CompilerParams(dimension_semantics=sem, vmem_limit_bytes=VMEM_LIMIT)


def _dot(a, b):
    return jnp.dot(a, b, preferred_element_type=F32)


def _dot_t(a, b):
    return lax.dot_general(a, b, (((1,), (1,)), ((), ())), preferred_element_type=F32)


def _rms(x):
    return x * lax.rsqrt(jnp.mean(x * x, axis=-1, keepdims=True) + NORM_EPS)


def _ada_kernel(c_ref, w_ref, b_ref, o_ref):
    c = c_ref[...]
    sc = c * (1.0 / (1.0 + jnp.exp(-c)))
    o_ref[...] = jnp.dot(sc, w_ref[...], preferred_element_type=F32,
                         precision=lax.Precision.HIGHEST) + b_ref[...]


def _ada_mod(c, ada_w, ada_b):
    depth, d, n = ada_w.shape
    b = c.shape[0]
    rows = 8
    cp = jnp.zeros((rows, d), F32).at[:b].set(c)
    tn = 2048
    out = pl.pallas_call(
        _ada_kernel,
        grid=(depth, n // tn),
        in_specs=[
            pl.BlockSpec((rows, d), lambda l, j: (0, 0)),
            pl.BlockSpec((None, d, tn), lambda l, j: (l, 0, j)),
            pl.BlockSpec((None, 1, tn), lambda l, j: (l, 0, j)),
        ],
        out_specs=pl.BlockSpec((None, rows, tn), lambda l, j: (l, 0, j)),
        out_shape=jax.ShapeDtypeStruct((depth, rows, n), F32),
        compiler_params=_cparams(("arbitrary", "arbitrary")),
        name="ada_mod",
    )(cp, ada_w, ada_b.reshape(depth, 1, n))
    return out[:, :b]


def _inproj_kernel(x_ref, sc_ref, sh_ref, w_ref, cos_ref, sin_ref,
                   a_ref, c_ref, r_ref, w8_ref, g_ref):
    h = (_rms(x_ref[...]) * sc_ref[...] + sh_ref[...]).astype(BF16)
    o = 0
    for c0 in range(0, N_A, 256):
        c1 = min(c0 + 256, N_A)
        a_ref[:, c0:c1] = _dot(h, w_ref[:, o + c0:o + c1]).astype(BF16)
    o += N_A
    c_ref[...] = _dot(h, w_ref[:, o:o + N_C])
    o += N_C
    for j, kind in enumerate(_ROPE_KIND):
        lo = j * LANES
        y = _dot(h, w_ref[:, o + lo:o + lo + LANES])
        yr = _dot(h, w_ref[:, o + N_R + lo:o + N_R + lo + LANES])
        cs = cos_ref[:, kind * LANES:(kind + 1) * LANES]
        sn = sin_ref[:, kind * LANES:(kind + 1) * LANES]
        r = y * cs + yr * sn
        if _ROPE_SCALE[j] != 1.0:
            r = r * _ROPE_SCALE[j]
        r_ref[:, lo:lo + LANES] = r.astype(BF16)
    o += 2 * N_R
    w8_ref[...] = _dot(h, w_ref[:, o:o + N_W]) * (IDX_DIM ** -0.5 * IDX_HEADS ** -0.5)
    o += N_W
    for c0 in range(0, N_G, 512):
        z = _dot(h, w_ref[:, o + c0:o + c0 + 512])
        g_ref[:, c0:c0 + 512] = (1.0 / (1.0 + jnp.exp(-z))).astype(BF16)


def _inproj(x, scale, shift, w, cos_t, sin_t, layer, seq):
    t, d = x.shape
    tm = 512
    per = seq // tm
    return pl.pallas_call(
        _inproj_kernel,
        grid=(t // tm,),
        in_specs=[
            pl.BlockSpec((tm, d), lambda i: (i, 0)),
            pl.BlockSpec((None, None, 1, d), lambda i: (layer, i // per, 0, 0)),
            pl.BlockSpec((None, None, 1, d), lambda i: (layer, i // per, 0, 0)),
            pl.BlockSpec((None, d, N_IN), lambda i: (layer, 0, 0), pipeline_mode=pl.Buffered(1)),
            pl.BlockSpec((tm, 3 * LANES), lambda i: (i, 0)),
            pl.BlockSpec((tm, 3 * LANES), lambda i: (i, 0)),
        ],
        out_specs=[
            pl.BlockSpec((tm, N_A), lambda i: (i, 0)),
            pl.BlockSpec((tm, N_C), lambda i: (i, 0)),
            pl.BlockSpec((tm, N_R), lambda i: (i, 0)),
            pl.BlockSpec((tm, N_W), lambda i: (i, 0)),
            pl.BlockSpec((tm, N_G), lambda i: (i, 0)),
        ],
        out_shape=[
            jax.ShapeDtypeStruct((t, N_A), BF16),
            jax.ShapeDtypeStruct((t, N_C), F32),
            jax.ShapeDtypeStruct((t, N_R), BF16),
            jax.ShapeDtypeStruct((t, N_W), F32),
            jax.ShapeDtypeStruct((t, N_G), BF16),
        ],
        compiler_params=_cparams(("arbitrary",)),
        name="inproj",
    )(x, scale, shift, w, cos_t, sin_t)


def _mla_up_kernel(c_ref, gq_ref, gkv_ref, wq_ref, wkv_ref, cos_ref, sin_ref, kr_ref,
                   q_ref, k_ref, v_ref):
    c = c_ref[...]
    nq = (_rms(c[:, :MLA_Q_RANK]) * gq_ref[...]).astype(BF16)
    nkv = (_rms(c[:, MLA_Q_RANK:]) * gkv_ref[...]).astype(BF16)
    scale = (MLA_NOPE + MLA_ROPE) ** -0.5 * LOG2E
    cs = cos_ref[...] * scale
    sn = sin_ref[...] * scale
    kr = kr_ref[...].astype(F32)
    for h in range(MLA_HEADS):
        lo = h * LANES
        y = _dot(nq, wq_ref[:, lo:lo + LANES])
        yr = _dot(nq, wq_ref[:, N_QM + lo:N_QM + lo + LANES])
        q_ref[:, lo:lo + LANES] = (y * cs + yr * sn).astype(BF16)
        k_ref[:, lo:lo + LANES] = (_dot(nkv, wkv_ref[:, lo:lo + LANES]) + kr).astype(BF16)
    v_ref[...] = _dot(nkv, wkv_ref[:, N_QM:N_QM + N_VM]).astype(BF16)


def _mla_up(cl, gq, gkv, wq, wkv, cos_t, sin_t, rp, layer):
    t = cl.shape[0]
    tm = 512
    return pl.pallas_call(
        _mla_up_kernel,
        grid=(t // tm,),
        in_specs=[
            pl.BlockSpec((tm, N_C), lambda i: (i, 0)),
            pl.BlockSpec((None, 1, MLA_Q_RANK), lambda i: (layer, 0, 0)),
            pl.BlockSpec((None, 1, MLA_KV_RANK), lambda i: (layer, 0, 0)),
            pl.BlockSpec((None, MLA_Q_RANK, 2 * N_QM), lambda i: (layer, 0, 0)),
            pl.BlockSpec((None, MLA_KV_RANK, N_QM + N_VM), lambda i: (layer, 0, 0)),
            pl.BlockSpec((tm, LANES), lambda i: (i, 2)),
            pl.BlockSpec((tm, LANES), lambda i: (i, 2)),
            pl.BlockSpec((tm, LANES), lambda i: (i, 8)),
        ],
        out_specs=[
            pl.BlockSpec((tm, N_QM), lambda i: (i, 0)),
            pl.BlockSpec((tm, N_QM), lambda i: (i, 0)),
            pl.BlockSpec((tm, N_VM), lambda i: (i, 0)),
        ],
        out_shape=[
            jax.ShapeDtypeStruct((t, N_QM), BF16),
            jax.ShapeDtypeStruct((t, N_QM), BF16),
            jax.ShapeDtypeStruct((t, N_VM), BF16),
        ],
        compiler_params=_cparams(("arbitrary",)),
        name="mla_up",
    )(cl, gq, gkv, wq, wkv, cos_t, sin_t, rp)


def _sb_kernel(q_ref, k_ref, v_ref, o_ref, *, tq):
    qi = pl.program_id(2)
    lane = lax.broadcasted_iota(I32, (1, LANES), 1)
    row = lax.broadcasted_iota(I32, (tq, tq), 0)
    col = lax.broadcasted_iota(I32, (tq, tq), 1)
    causal = col < row
    tri = jnp.where(row > col, 1.0, 0.0).astype(BF16)
    q = q_ref[...]
    outs = []
    for h in range(2):
        head = (lane >= h * SB_DIM) & (lane < (h + 1) * SB_DIM)
        qh = jnp.where(head, q, jnp.zeros_like(q)) * jnp.asarray(SB_DIM ** -0.5, BF16)

        def block(j, remain, acc, diagonal, qh=qh):
            start = pl.multiple_of(j * tq, tq)
            kb = k_ref[pl.ds(start, tq), :]
            vb = v_ref[pl.ds(start, tq), :]
            z = _dot_t(qh, kb)
            soft = jnp.log1p(jnp.exp(-jnp.abs(z)))
            log_stay = -(jnp.maximum(z, 0.0) + soft)
            if diagonal:
                log_stay = jnp.where(causal, log_stay, 0.0)
            hi = log_stay.astype(BF16)
            lo = (log_stay - hi.astype(F32)).astype(BF16)
            later = _dot(hi, tri) + _dot(lo, tri)
            log_a = (z + log_stay) + later + remain
            a = jnp.exp(log_a)
            if diagonal:
                a = jnp.where(causal, a, 0.0)
            acc = acc + _dot(a.astype(BF16), vb)
            remain = remain + jnp.sum(log_stay, axis=-1, keepdims=True)
            return remain, acc

        remain, acc = block(qi, jnp.zeros((tq, 1), F32), jnp.zeros((tq, LANES), F32), True)

        def cond(carry):
            j, remain, _ = carry
            return jnp.logical_and(j >= 0, jnp.max(remain) > -SB_UNDERFLOW)

        def body(carry, block=block):
            j, remain, acc = carry
            remain, acc = block(j, remain, acc, False)
            return j - 1, remain, acc

        _, _, acc = lax.while_loop(cond, body, (qi - 1, remain, acc))
        outs.append(acc)
    o_ref[...] = jnp.where(lane < SB_DIM, outs[0], outs[1]).astype(BF16)


def _sb_attention(a, batch, seq):
    t = a.shape[0]
    tq = 256
    nq = seq // tq
    pairs = SB_HEADS // 2
    return pl.pallas_call(
        functools.partial(_sb_kernel, tq=tq),
        grid=(batch, pairs, nq),
        in_specs=[
            pl.BlockSpec((tq, LANES), lambda b, p, i: (b * nq + i, 2 + p)),
            pl.BlockSpec((seq, LANES), lambda b, p, i: (b, 2 + pairs + p)),
            pl.BlockSpec((seq, LANES), lambda b, p, i: (b, 2 + 2 * pairs + p)),
        ],
        out_specs=pl.BlockSpec((tq, LANES), lambda b, p, i: (b * nq + i, p)),
        out_shape=jax.ShapeDtypeStruct((t, SB_HEADS * SB_DIM), BF16),
        compiler_params=_cparams(("arbitrary", "arbitrary", "arbitrary")),
        name="sb_attention",
    )(a, a, a)


def _softmax_step(s, vb, h, acc_ref, l_ref, m_ref):
    reps = s.shape[1] // LANES
    m_old = m_ref[h]
    m_new = jnp.maximum(m_old, jnp.max(s, axis=-1, keepdims=True))
    alpha = jnp.exp2(m_old - m_new)
    p = jnp.exp2(s - jnp.concatenate([m_new] * reps, axis=1))
    part = p[:, 0:LANES]
    for c in range(1, reps):
        part = part + p[:, c * LANES:(c + 1) * LANES]
    l_ref[h] = alpha * l_ref[h] + part
    acc_ref[h] = alpha * acc_ref[h] + _dot(p.astype(BF16), vb)
    m_ref[h] = m_new


def _softmax_init(acc_ref, l_ref, m_ref):
    acc_ref[...] = jnp.zeros_like(acc_ref)
    l_ref[...] = jnp.zeros_like(l_ref)
    m_ref[...] = jnp.full_like(m_ref, NEG_BIG)


def _softmax_out(h, acc_ref, l_ref):
    return acc_ref[h] / jnp.sum(l_ref[h], axis=-1, keepdims=True)


def _mla_kernel(q_ref, k_ref, v_ref, o_ref, acc_ref, l_ref, m_ref, *, tq, tkb):
    qi = pl.program_id(2)
    lane = lax.broadcasted_iota(I32, (1, LANES), 1)
    row = lax.broadcasted_iota(I32, (tq, tq), 0)
    col = lax.broadcasted_iota(I32, (tq, tq), 1)
    visible = (col >> CHUNK_SHIFT) <= (row >> CHUNK_SHIFT)
    q = q_ref[...]
    _softmax_init(acc_ref, l_ref, m_ref)

    def step(start, tk, diagonal):
        kb = k_ref[pl.ds(start, tk), :]
        vb = v_ref[pl.ds(start, tk), :]
        for h in range(2):
            s = _dot_t(q[:, h * LANES:(h + 1) * LANES], kb[:, h * LANES:(h + 1) * LANES])
            if diagonal:
                s = jnp.where(visible, s, NEG_BIG)
            _softmax_step(s, vb, h, acc_ref, l_ref, m_ref)

    n_wide = (qi * tq) // tkb

    def wide(j, c):
        step(pl.multiple_of(j * tkb, tkb), tkb, False)
        return c

    def narrow(j, c):
        step(pl.multiple_of(j * tq, tq), tq, False)
        return c

    lax.fori_loop(0, n_wide, wide, 0)
    lax.fori_loop(n_wide * (tkb // tq), qi, narrow, 0)
    step(pl.multiple_of(qi * tq, tq), tq, True)
    o_ref[...] = jnp.where(lane < MLA_V, _softmax_out(0, acc_ref, l_ref),
                           _softmax_out(1, acc_ref, l_ref)).astype(BF16)


def _mla_attention(qm, km, vm, batch, seq):
    t = qm.shape[0]
    tq = min(512, seq)
    tkb = min(1024, seq)
    nq = seq // tq
    pairs = MLA_HEADS // 2
    stat = pltpu.VMEM((2, tq, LANES), F32)
    return pl.pallas_call(
        functools.partial(_mla_kernel, tq=tq, tkb=tkb),
        grid=(batch, pairs, nq),
        in_specs=[
            pl.BlockSpec((tq, 2 * LANES), lambda b, p, i: (b * nq + i, p)),
            pl.BlockSpec((seq, 2 * LANES), lambda b, p, i: (b, p)),
            pl.BlockSpec((seq, LANES), lambda b, p, i: (b, p)),
        ],
        out_specs=pl.BlockSpec((tq, LANES), lambda b, p, i: (b * nq + i, p)),
        out_shape=jax.ShapeDtypeStruct((t, N_VM), BF16),
        scratch_shapes=[stat, stat, stat],
        compiler_params=_cparams(("arbitrary", "arbitrary", "arbitrary")),
        name="mla_attention",
    )(qm, km, vm)


SEARCH_BISECT_EVERY = 3
SEARCH_MAX_STEPS = 3 * 32 + 4


def _dsa_kernel(qd_ref, qx_ref, w_ref, kx_ref, kd_ref, vd_ref, o_ref,
                key_ref, qs_ref, wr_ref, top_ref, cand_ref, cnt_ref, acc_ref, l_ref, m_ref, *, tq, tk, n_select):
    it = pl.program_id(1)
    t0 = it * tq
    last = t0 // tk
    reps = tk // LANES
    lane = lax.broadcasted_iota(I32, (1, LANES), 1)
    lane2 = lax.broadcasted_iota(I32, (1, 2 * LANES), 1)
    rowid = t0 + lax.broadcasted_iota(I32, (tq, 1), 0)
    row_chunk = rowid >> CHUNK_SHIFT

    qx = qx_ref[...]
    w = w_ref[...]
    for h in range(IDX_HEADS):
        head = (lane2 >= h * IDX_DIM) & (lane2 < (h + 1) * IDX_DIM)
        qs_ref[h * tq:(h + 1) * tq, :] = jnp.where(head, qx, jnp.zeros_like(qx))
        wr_ref[h] = jnp.broadcast_to(w[:, h:h + 1], (tq, LANES))
    top_ref[...] = jnp.full_like(top_ref, INT_MIN)

    def score_block(j, diagonal):
        start = pl.multiple_of(j * tk, tk)
        d = _dot_t(qs_ref[...], kx_ref[pl.ds(start, tk), :])
        score = jnp.zeros((tq, tk), F32)
        for h in range(IDX_HEADS):
            wh = jnp.concatenate([wr_ref[h]] * reps, axis=1)
            score = score + wh * jnp.maximum(d[h * tq:(h + 1) * tq], 0.0)
        score = jnp.where(score == 0.0, 0.0, score)
        bits = pltpu.bitcast(score, I32)
        key = jnp.where(bits < 0, bits ^ jnp.int32(0x7FFFFFFF), bits)
        if diagonal:
            col_chunk = (start + lax.broadcasted_iota(I32, (1, tk), 1)) >> CHUNK_SHIFT
            key = jnp.where(col_chunk <= row_chunk, key, INT_MIN)
        key_ref[j] = key
        t1, t2 = top_ref[0], top_ref[1]
        for c in range(reps):
            x = key[:, c * LANES:(c + 1) * LANES]
            t2 = jnp.maximum(t2, jnp.minimum(t1, x))
            t1 = jnp.maximum(t1, x)
        top_ref[0] = t1
        top_ref[1] = t2

    def score_body(j, c):
        score_block(j, False)
        return c

    lax.fori_loop(0, last, score_body, 0)
    score_block(last, True)
    n_blocks = last + 1

    n_chunks = tq // LANES

    def to_lanes(rep):
        return jnp.concatenate([jnp.transpose(rep[c * LANES:(c + 1) * LANES, :])[0:1, :]
                                for c in range(n_chunks)], axis=1)

    def to_rows(row):
        return jnp.concatenate([jnp.transpose(jnp.broadcast_to(row[:, c * LANES:(c + 1) * LANES], (LANES, LANES)))
                                for c in range(n_chunks)], axis=0)

    def count_ge(cand):
        cand_ref[...] = to_rows(cand)
        cnt_ref[...] = jnp.zeros_like(cnt_ref)

        def body(j, c):
            for half in range(n_chunks):
                rows = slice(half * LANES, (half + 1) * LANES)
                cb = jnp.concatenate([cand_ref[rows, :]] * reps, axis=1)
                ge = jnp.where(key_ref[j, rows, :] >= cb, 1, 0)
                part = ge[:, 0:LANES]
                for cc in range(1, reps):
                    part = part + ge[:, cc * LANES:(cc + 1) * LANES]
                cnt_ref[rows, :] += part
            return c

        lax.fori_loop(0, n_blocks, body, 0)
        parts = cnt_ref[...]
        return jnp.concatenate([jnp.sum(jnp.transpose(parts[c * LANES:(c + 1) * LANES, :]), axis=0, keepdims=True)
                                for c in range(n_chunks)], axis=1)

    def key_value(k):
        return pltpu.bitcast(jnp.where(k < 0, k ^ jnp.int32(0x7FFFFFFF), k), F32)

    def value_key(v):
        bits = pltpu.bitcast(v, I32)
        return jnp.where(bits < 0, bits ^ jnp.int32(0x7FFFFFFF), bits)

    row_l = t0 + lax.broadcasted_iota(I32, (1, tq), 1)
    few = ((row_l >> CHUNK_SHIFT) + 1) * CHUNK <= n_select
    t2_min = jnp.broadcast_to(jnp.min(top_ref[1], axis=-1, keepdims=True), (tq, LANES))
    t1_max = jnp.broadcast_to(jnp.max(top_ref[0], axis=-1, keepdims=True), (tq, LANES))
    lo = to_lanes(t2_min)
    hi = to_lanes(t1_max) + 1
    c_lo = count_ge(lo)
    c_hi = jnp.zeros((1, tq), I32)
    c_pos = count_ge(jnp.full((1, tq), 1, I32))
    c_nn = count_ge(jnp.zeros((1, tq), I32))
    positive = c_pos >= n_select
    negative = c_nn < n_select
    at_zero = jnp.logical_not(jnp.logical_or(positive, negative))
    raise_lo = jnp.logical_and(positive, lo < 1)
    lower_hi = jnp.logical_and(negative, hi > 0)
    lo, c_lo = jnp.where(raise_lo, 1, lo), jnp.where(raise_lo, c_pos, c_lo)
    hi, c_hi = jnp.where(lower_hi, 0, hi), jnp.where(lower_hi, c_nn, c_hi)
    lo, c_lo = jnp.where(at_zero, 0, lo), jnp.where(at_zero, c_nn, c_lo)
    hi, c_hi = jnp.where(at_zero, 1, hi), jnp.where(at_zero, c_pos, c_hi)

    def active_rows(lo, c_lo, hi):
        return jnp.logical_and(jnp.logical_not(few), jnp.logical_and(c_lo > n_select, hi - lo > 1))

    def search_cond(carry):
        step, lo, c_lo, hi, c_hi = carry
        busy = jnp.max(jnp.where(active_rows(lo, c_lo, hi), 1, 0))
        return jnp.logical_and(step < SEARCH_MAX_STEPS, busy > 0)

    def search_body(carry):
        step, lo, c_lo, hi, c_hi = carry
        active = active_rows(lo, c_lo, hi)
        a = jnp.log(c_lo.astype(F32))
        b = jnp.log(c_hi.astype(F32) + 0.5)
        frac = jnp.clip((a - math.log(n_select - 0.25)) / (a - b), 1.0 / 64, 63.0 / 64)
        v_lo, v_hi = key_value(lo), key_value(hi)
        guess = value_key(v_lo + frac * (v_hi - v_lo))
        middle = lo + lax.shift_right_logical(hi - lo, 1)
        cand = jnp.where(step % SEARCH_BISECT_EVERY == SEARCH_BISECT_EVERY - 1, middle, guess)
        cand = jnp.minimum(jnp.maximum(cand, lo + 1), hi - 1)
        c = count_ge(cand)
        up = jnp.logical_and(active, c >= n_select)
        down = jnp.logical_and(active, c < n_select)
        lo = jnp.where(up, cand, lo)
        c_lo = jnp.where(up, c, c_lo)
        hi = jnp.where(down, cand, hi)
        c_hi = jnp.where(down, c, c_hi)
        return step + 1, lo, c_lo, hi, c_hi

    _, lo, c_lo, hi, c_hi = lax.while_loop(search_cond, search_body, (jnp.int32(0), lo, c_lo, hi, c_hi))
    theta = jnp.where(few, INT_MIN, lo)

    tied = jnp.logical_and(jnp.logical_not(few), c_lo > n_select)

    @pl.when(jnp.max(jnp.where(tied, 1, 0)) > 0)
    def _():
        theta_r = to_rows(theta)[:, 0:1]
        tied_r = to_rows(jnp.where(tied, 1, 0))[:, 0:1] > 0
        need = to_rows((n_select - c_hi).astype(F32))[:, 0:1]
        r = lax.broadcasted_iota(I32, (tk, tk), 0)
        c = lax.broadcasted_iota(I32, (tk, tk), 1)
        before = jnp.where(r < c, 1.0, 0.0).astype(BF16)

        def body(j, seen):
            key = key_ref[j]
            eq = jnp.logical_and(key == theta_r, tied_r)
            eqf = jnp.where(eq, 1.0, 0.0)
            rank = seen + _dot(eqf.astype(BF16), before)
            key_ref[j] = jnp.where(jnp.logical_and(eq, rank >= need), INT_MIN, key)
            return seen + jnp.sum(eqf, axis=-1, keepdims=True)

        lax.fori_loop(0, n_blocks, body, jnp.zeros((tq, 1), F32))

    cand_ref[...] = to_rows(jnp.maximum(theta, INT_MIN + 1))

    qd = qd_ref[...]
    q_heads = []
    for h in range(DSA_HEADS):
        blk = qd[:, (h // 2) * LANES:(h // 2 + 1) * LANES]
        head = (lane >= (h % 2) * DSA_DIM) & (lane < (h % 2 + 1) * DSA_DIM)
        q_heads.append(jnp.where(head, blk, jnp.zeros_like(blk)))
    _softmax_init(acc_ref, l_ref, m_ref)

    def attend(j, c):
        start = pl.multiple_of(j * tk, tk)
        sel = key_ref[j] >= jnp.concatenate([cand_ref[...]] * reps, axis=1)
        kb = kd_ref[pl.ds(start, tk), :]
        vb = vd_ref[pl.ds(start, tk), :]
        for h in range(DSA_HEADS):
            p0 = (h // 2) * LANES
            s = jnp.where(sel, _dot_t(q_heads[h], kb[:, p0:p0 + LANES]), NEG_BIG)
            _softmax_step(s, vb[:, p0:p0 + LANES], h, acc_ref, l_ref, m_ref)
        return c

    lax.fori_loop(0, n_blocks, attend, 0)
    for p in range(DSA_HEADS // 2):
        o_ref[:, p * LANES:(p + 1) * LANES] = jnp.where(
            lane < DSA_DIM, _softmax_out(2 * p, acc_ref, l_ref), _softmax_out(2 * p + 1, acc_ref, l_ref)).astype(BF16)


def _dsa_attention(a, rp, w8, batch, seq, n_select):
    t = a.shape[0]
    tq = 256
    tk = min(1024, seq)
    nq = seq // tq
    width = DSA_HEADS * DSA_DIM
    keys = lambda col: pl.BlockSpec((seq, width), lambda b, i: (b, col), pipeline_mode=pl.Buffered(1))
    stat = lambda n: pltpu.VMEM((n, tq, LANES), F32)
    return pl.pallas_call(
        functools.partial(_dsa_kernel, tq=tq, tk=tk, n_select=n_select),
        grid=(batch, nq),
        in_specs=[
            pl.BlockSpec((tq, width), lambda b, i: (b * nq + i, 0)),
            pl.BlockSpec((tq, width), lambda b, i: (b * nq + i, 2)),
            pl.BlockSpec((tq, N_W), lambda b, i: (b * nq + i, 0)),
            keys(3),
            keys(1),
            keys(0),
        ],
        out_specs=pl.BlockSpec((tq, width), lambda b, i: (b * nq + i, 0)),
        out_shape=jax.ShapeDtypeStruct((t, width), BF16),
        scratch_shapes=[
            pltpu.VMEM((seq // tk, tq, tk), I32),
            pltpu.VMEM((IDX_HEADS * tq, width), BF16),
            stat(IDX_HEADS),
            pltpu.VMEM((2, tq, LANES), I32),
            pltpu.VMEM((tq, LANES), I32),
            pltpu.VMEM((tq, LANES), I32),
            stat(DSA_HEADS), stat(DSA_HEADS), stat(DSA_HEADS),
        ],
        compiler_params=_cparams(("arbitrary", "arbitrary")),
        name="dsa_attention",
    )(rp, rp, w8, rp, rp, a)


def _merge_kernel(x_ref, osb_ref, omla_ref, odsa_ref, g_ref, gm_ref, wsb_ref, wmla_ref, wdsa_ref, wout_ref,
                  o_ref):
    d = D_MODEL
    merged = (g_ref[:, 0:d].astype(F32) * _dot(osb_ref[...], wsb_ref[...])
              + g_ref[:, d:2 * d].astype(F32) * _dot(omla_ref[...], wmla_ref[...])
              + g_ref[:, 2 * d:3 * d].astype(F32) * _dot(odsa_ref[...], wdsa_ref[...]))
    o_ref[...] = x_ref[...] + gm_ref[...] * _dot(merged.astype(BF16), wout_ref[...])


def _merge(x, osb, omla, odsa, gates, gm, wsb, wmla, wdsa, wout, layer, seq):
    t, d = x.shape
    tm = 512
    per = seq // tm
    row = lambda i: (i, 0)
    wspec = lambda k: pl.BlockSpec((None, k, d), lambda i: (layer, 0, 0))
    return pl.pallas_call(
        _merge_kernel,
        grid=(t // tm,),
        in_specs=[
            pl.BlockSpec((tm, d), row),
            pl.BlockSpec((tm, osb.shape[1]), row),
            pl.BlockSpec((tm, omla.shape[1]), row),
            pl.BlockSpec((tm, odsa.shape[1]), row),
            pl.BlockSpec((tm, N_G), row),
            pl.BlockSpec((None, None, 1, d), lambda i: (layer, i // per, 0, 0)),
            wspec(osb.shape[1]), wspec(omla.shape[1]), wspec(odsa.shape[1]), wspec(d),
        ],
        out_specs=pl.BlockSpec((tm, d), row),
        out_shape=jax.ShapeDtypeStruct((t, d), F32),
        compiler_params=_cparams(("arbitrary",)),
        name="merge",
    )(x, osb, omla, odsa, gates, gm, wsb, wmla, wdsa, wout)


def _router_kernel(x_ref, sc_ref, sh_ref, rw_ref, rb_ref, h_ref, ti_ref, tw_ref, tr_ref, cnt_ref, run_ref, *, tm):
    @pl.when(pl.program_id(0) == 0)
    def _():
        run_ref[...] = jnp.zeros_like(run_ref)

    h = _rms(x_ref[...]) * sc_ref[...] + sh_ref[...]
    h_ref[...] = h
    logits = jnp.dot(h, rw_ref[...], preferred_element_type=F32, precision=lax.Precision.HIGHEST) + rb_ref[...]
    lane = lax.broadcasted_iota(I32, (tm, LANES), 1)
    work = logits
    vals, hots = [], []
    for _ in range(TOP_K):
        m = jnp.max(work, axis=-1, keepdims=True)
        first = jnp.min(jnp.where(work == m, lane, LANES), axis=-1, keepdims=True)
        hot = lane == first
        vals.append(m)
        hots.append(hot)
        work = jnp.where(hot, -jnp.inf, work)
    exps = [jnp.exp(v - vals[0]) for v in vals]
    denom = exps[0] + exps[1] + exps[2] + exps[3]
    chosen = jnp.zeros((tm, LANES), F32)
    for hot in hots:
        chosen = chosen + jnp.where(hot, 1.0, 0.0)
    r = lax.broadcasted_iota(I32, (tm, tm), 0)
    c = lax.broadcasted_iota(I32, (tm, tm), 1)
    earlier = jnp.where(c < r, 1.0, 0.0).astype(BF16)
    rank_all = _dot(earlier, chosen.astype(BF16)) + run_ref[...]
    ti = jnp.zeros((tm, LANES), I32)
    tw = jnp.zeros((tm, LANES), F32)
    tr = jnp.zeros((tm, LANES), I32)
    for k in range(TOP_K):
        e_k = jnp.sum(jnp.where(hots[k], lane, 0), axis=-1, keepdims=True)
        r_k = jnp.sum(jnp.where(hots[k], rank_all, 0.0), axis=-1, keepdims=True).astype(I32)
        ti = jnp.where(lane == k, e_k, ti)
        tw = jnp.where(lane == k, exps[k] / denom, tw)
        tr = jnp.where(lane == k, r_k, tr)
    ti_ref[...] = ti
    tw_ref[...] = tw
    tr_ref[...] = tr
    run_ref[...] = run_ref[...] + jnp.sum(chosen, axis=0, keepdims=True)
    cnt_ref[...] = run_ref[...]


def _router(x, scale, shift, rw, rb, layer, seq):
    t, d = x.shape
    tm = 256
    per = seq // tm
    row = lambda i: (i, 0)
    return pl.pallas_call(
        functools.partial(_router_kernel, tm=tm),
        grid=(t // tm,),
        in_specs=[
            pl.BlockSpec((tm, d), row),
            pl.BlockSpec((None, None, 1, d), lambda i: (layer, i // per, 0, 0)),
            pl.BlockSpec((None, None, 1, d), lambda i: (layer, i // per, 0, 0)),
            pl.BlockSpec((None, d, LANES), lambda i: (layer, 0, 0)),
            pl.BlockSpec((None, 1, LANES), lambda i: (layer, 0, 0)),
        ],
        out_specs=[
            pl.BlockSpec((tm, d), row),
            pl.BlockSpec((tm, LANES), row),
            pl.BlockSpec((tm, LANES), row),
            pl.BlockSpec((tm, LANES), row),
            pl.BlockSpec((1, LANES), lambda i: (0, 0)),
        ],
        out_shape=[
            jax.ShapeDtypeStruct((t, d), F32),
            jax.ShapeDtypeStruct((t, LANES), I32),
            jax.ShapeDtypeStruct((t, LANES), F32),
            jax.ShapeDtypeStruct((t, LANES), I32),
            jax.ShapeDtypeStruct((1, LANES), F32),
        ],
        scratch_shapes=[pltpu.VMEM((1, LANES), F32)],
        compiler_params=_cparams(("arbitrary",)),
        name="router",
    )(x, scale, shift, rw, rb)


def _expert_kernel(te_ref, tf_ref, nv_ref, x_ref, wgu_ref, bgu_ref, wd_ref, bd_ref, o_ref,
                   wgu_bf, wd_bf):
    i = pl.program_id(0)

    @pl.when(i >= nv_ref[0])
    def _():
        o_ref[...] = jnp.zeros_like(o_ref)

    @pl.when(i < nv_ref[0])
    def _():
        @pl.when(tf_ref[i] == 1)
        def _():
            wgu_bf[...] = wgu_ref[...].astype(BF16)
            wd_bf[...] = wd_ref[...].astype(BF16)

        gu = _dot(x_ref[...].astype(BF16), wgu_bf[...]) + bgu_ref[...]
        gate = jnp.minimum(gu[:, :D_EXPERT], SWIGLU_LIMIT)
        up = jnp.clip(gu[:, D_EXPERT:], -SWIGLU_LIMIT, SWIGLU_LIMIT)
        act = (up + 1.0) * (gate * (1.0 / (1.0 + jnp.exp(-SWIGLU_ALPHA * gate))))
        o_ref[...] = _dot(act.astype(BF16), wd_bf[...]) + bd_ref[...]


def _experts(xs, tile_e, tile_first, n_valid, wgu, bgu, wd, bd, layer, tm):
    p, d = xs.shape
    n_tiles = p // tm
    grid_spec = pltpu.PrefetchScalarGridSpec(
        num_scalar_prefetch=3,
        grid=(n_tiles,),
        in_specs=[
            pl.BlockSpec((tm, d), lambda i, te, tf, nv: (i, 0)),
            pl.BlockSpec((None, None, d, 2 * D_EXPERT), lambda i, te, tf, nv: (layer, te[i], 0, 0)),
            pl.BlockSpec((None, None, 1, 2 * D_EXPERT), lambda i, te, tf, nv: (layer, te[i], 0, 0)),
            pl.BlockSpec((None, None, D_EXPERT, d), lambda i, te, tf, nv: (layer, te[i], 0, 0)),
            pl.BlockSpec((None, None, 1, d), lambda i, te, tf, nv: (layer, te[i], 0, 0)),
        ],
        out_specs=pl.BlockSpec((tm, d), lambda i, te, tf, nv: (i, 0)),
        scratch_shapes=[pltpu.VMEM((d, 2 * D_EXPERT), BF16), pltpu.VMEM((D_EXPERT, d), BF16)],
    )
    return pl.pallas_call(
        _expert_kernel,
        grid_spec=grid_spec,
        out_shape=jax.ShapeDtypeStruct((p, d), F32),
        compiler_params=_cparams(("arbitrary",)),
        name="experts",
    )(tile_e, tile_first, n_valid, xs, wgu, bgu, wd, bd)


def _combine_kernel(x_ref, y0_ref, y1_ref, y2_ref, y3_ref, tw_ref, g_ref, fg_ref, o_ref, *, final):
    tw = tw_ref[...]
    y = tw[:, 0:1] * y0_ref[...]
    for k, y_ref in enumerate((y1_ref, y2_ref, y3_ref), start=1):
        y = y + tw[:, k:k + 1] * y_ref[...]
    x = x_ref[...] + g_ref[...] * y
    if final:
        x = _rms(x) * fg_ref[...]
    o_ref[...] = x


def _combine(x, yg, tw, gf, final_g, layer, seq, final):
    t, d = x.shape
    tm = 256
    nt = t // tm
    per = seq // tm
    slot = lambda k: pl.BlockSpec((tm, d), lambda i: (k * nt + i, 0))
    return pl.pallas_call(
        functools.partial(_combine_kernel, final=final),
        grid=(nt,),
        in_specs=[
            pl.BlockSpec((tm, d), lambda i: (i, 0)),
            slot(0), slot(1), slot(2), slot(3),
            pl.BlockSpec((tm, LANES), lambda i: (i, 0)),
            pl.BlockSpec((None, None, 1, d), lambda i: (layer, i // per, 0, 0)),
            pl.BlockSpec((1, d), lambda i: (0, 0)),
        ],
        out_specs=pl.BlockSpec((tm, d), lambda i: (i, 0)),
        out_shape=jax.ShapeDtypeStruct((t, d), F32),
        compiler_params=_cparams(("arbitrary",)),
        name="combine",
    )(x, yg, yg, yg, yg, tw, gf, final_g)


def _rope_tables(positions):
    pos = positions.reshape(-1).astype(F32)

    def cs(dim):
        inv_freq = ROPE_THETA ** (-jnp.arange(0, dim, 2, dtype=F32) / dim)
        ang = pos[:, None] * inv_freq
        return jnp.cos(ang), jnp.sin(ang)

    c_d, s_d = cs(DSA_DIM)
    c_i, s_i = cs(IDX_DIM)
    c_m, s_m = cs(MLA_ROPE)
    t = pos.shape[0]
    ones, zeros = jnp.ones((t, 64), F32), jnp.zeros((t, 64), F32)
    cos_t = jnp.concatenate([jnp.tile(c_d, (1, 4)), jnp.tile(c_i, (1, 8)),
                             ones, c_m, c_m, ones[:, :32]], axis=1)
    sin_t = jnp.concatenate([jnp.tile(s_d, (1, 4)), jnp.tile(s_i, (1, 8)),
                             zeros, s_m, s_m, zeros[:, :32]], axis=1)
    return cos_t, sin_t


def _permute_cols(w, idx, sgn):
    return (jnp.take(w, jnp.asarray(idx), axis=-1) * jnp.asarray(sgn)).astype(BF16)


def _dispatch_tables(ti, tr, counts, tm):
    t = ti.shape[0]
    p = t * TOP_K + N_EXPERTS * tm
    n_tiles = p // tm
    cnt = counts.astype(I32)
    padded = ((cnt + tm - 1) // tm) * tm
    ends = jnp.cumsum(padded)
    starts = ends - padded
    pos = starts[ti] + tr
    tok = jnp.repeat(jnp.arange(t, dtype=I32), TOP_K)
    row_src = jnp.zeros((p,), I32).at[pos.reshape(-1)].set(tok)
    tile_start = jnp.arange(n_tiles, dtype=I32) * tm
    n_valid = ends[-1] // tm
    tile_e = jnp.sum((ends[None, :] <= tile_start[:, None]).astype(I32), axis=1)
    tile_e = jnp.minimum(tile_e, N_EXPERTS - 1)
    last_e = jnp.max(jnp.where(tile_start < ends[-1], tile_e, 0))
    tile_e = jnp.where(tile_start < ends[-1], tile_e, last_e)
    tile_first = jnp.concatenate([jnp.ones((1,), I32), (tile_e[1:] != tile_e[:-1]).astype(I32)])
    return pos, row_src, tile_e, tile_first, n_valid.astype(I32).reshape(1)


def kernel(x, c, positions, ada_w, ada_b, norm_mix_g, w_in, mla_q_norm_g, mla_kv_norm_g, mla_w_uq, mla_w_ukv,
           w_sb_out, w_mla_out, w_dsa_out, w_out, norm_ffn_g, router_w, router_b, expert_w_gu, expert_b_gu,
           expert_w_down, expert_b_down, final_norm_g):
    batch, seq, d = x.shape
    depth = ada_w.shape[0]
    t = batch * seq
    n_select = min(DSA_TOPK_MAX, seq // 4)
    tm_e = 256

    mod = _ada_mod(c, ada_w, ada_b)
    sh_m, sc_m, g_m, sh_f, sc_f, g_f = [m[:, :, None, :] for m in jnp.split(mod, 6, axis=-1)]
    scale_m = norm_mix_g[:, None, None, :] * (1.0 + sc_m)
    scale_f = norm_ffn_g[:, None, None, :] * (1.0 + sc_f)

    cos_t, sin_t = _rope_tables(positions)
    w_in_p = _permute_cols(w_in, _IN_IDX, _IN_SGN)
    w_uq_p = _permute_cols(mla_w_uq, _UQ_IDX, _UQ_SGN)
    w_ukv_p = _permute_cols(mla_w_ukv, _UKV_IDX, _UKV_SGN)
    w_sb_b, w_mla_b, w_dsa_b, w_out_b = [w.astype(BF16) for w in (w_sb_out, w_mla_out, w_dsa_out, w_out)]
    gq = mla_q_norm_g[:, None, :]
    gkv = mla_kv_norm_g[:, None, :]
    rw_p = jnp.zeros((depth, d, LANES), F32).at[:, :, :N_EXPERTS].set(router_w)
    rb_p = jnp.full((depth, 1, LANES), -jnp.inf, F32).at[:, 0, :N_EXPERTS].set(router_b)
    bgu = expert_b_gu[:, :, None, :]
    bd = expert_b_down[:, :, None, :]
    fg = final_norm_g[None, :]

    xf = x.reshape(t, d)
    for l in range(depth):
        a, cl, rp, w8, gates = _inproj(xf, scale_m, sh_m, w_in_p, cos_t, sin_t, l, seq)
        o_sb = _sb_attention(a, batch, seq)
        qm, km, vm = _mla_up(cl, gq, gkv, w_uq_p, w_ukv_p, cos_t, sin_t, rp, l)
        o_mla = _mla_attention(qm, km, vm, batch, seq)
        o_dsa = _dsa_attention(a, rp, w8, batch, seq, n_select)
        xf = _merge(xf, o_sb, o_mla, o_dsa, gates, g_m, w_sb_b, w_mla_b, w_dsa_b, w_out_b, l, seq)

        hf, ti, tw, tr, counts = _router(xf, scale_f, sh_f, rw_p, rb_p, l, seq)
        pos, row_src, tile_e, tile_first, n_valid = _dispatch_tables(
            ti[:, :TOP_K], tr[:, :TOP_K], counts[0, :N_EXPERTS], tm_e)
        xs = jnp.take(hf, row_src, axis=0)
        ys = _experts(xs, tile_e, tile_first, n_valid, expert_w_gu, bgu, expert_w_down, bd, l, tm_e)
        yg = jnp.take(ys, pos.T.reshape(-1), axis=0)
        xf = _combine(xf, yg, tw, g_f, fg, l, seq, l == depth - 1)
    return xf.reshape(batch, seq, d)
```

```python
import functools
import math

import numpy as np
import jax
import jax.numpy as jnp
from jax import lax
from jax.experimental import pallas as pl
from jax.experimental.pallas import tpu as pltpu

F32 = jnp.float32
BF16 = jnp.bfloat16
I32 = jnp.int32

D_MODEL = 1024
CHUNK = 64
CHUNK_SHIFT = 6
ROPE_THETA = 10000.0
NORM_EPS = 1e-6
SB_HEADS, SB_DIM = 6, 64
MLA_HEADS, MLA_NOPE, MLA_ROPE, MLA_V = 6, 64, 32, 64
MLA_Q_RANK, MLA_KV_RANK = 256, 128
DSA_HEADS, DSA_DIM = 4, 64
IDX_HEADS, IDX_DIM = 8, 32
DSA_TOPK_MAX = 256
N_EXPERTS, TOP_K = 32, 4
D_EXPERT = D_MODEL
SWIGLU_LIMIT = 7.0
SWIGLU_ALPHA = 1.702

LANES = 128
VMEM_LIMIT = 56 * 1024 * 1024

INT_MIN = -(2 ** 31)
NEG_BIG = -1e30
SB_UNDERFLOW = 104.0
LOG2E = math.log2(math.e)

_SPLIT = (384, 384, 384, 256, 128, 32, 256, 256, 256, 256, 32, 8, 3072)
_OFF = np.concatenate([[0], np.cumsum(_SPLIT)]).astype(np.int64)
(O_SBQ, O_SBK, O_SBV, O_CQ, O_CKV, O_KR, O_DQ, O_DK, O_DV, O_IQ, O_IK, O_IW, O_GATE, D_IN) = [
    int(v) for v in _OFF]

N_A = 1408
N_C = 384
N_R = 1152
N_W = 128
N_G = 3072
N_IN = N_A + N_C + 2 * N_R + N_W + N_G
_ROPE_SPANS = ((0, 2, 0, DSA_DIM ** -0.5 * LOG2E), (256, 2, 0, 1.0), (512, 2, 1, 1.0), (768, 2, 1, 1.0),
               (1024, 1, 2, 1.0))


def _rot_cols(base, n_heads, d):
    half = d // 2
    idx, sgn = [], []
    for h in range(n_heads):
        for j in range(d):
            if j < half:
                idx.append(base + h * d + j + half)
                sgn.append(-1.0)
            else:
                idx.append(base + h * d + j - half)
                sgn.append(1.0)
    return idx, sgn


def _in_layout():
    idx, sgn = [], []

    def plain(base, n):
        idx.extend(range(base, base + n))
        sgn.extend([1.0] * n)

    def pad(n):
        idx.extend([0] * n)
        sgn.extend([0.0] * n)

    plain(O_DV, 256); plain(O_SBQ, 384); plain(O_SBK, 384); plain(O_SBV, 384)
    plain(O_CQ, 256); plain(O_CKV, 128)
    plain(O_DQ, 256); plain(O_DK, 256); plain(O_IQ, 256)
    for _ in range(IDX_HEADS):
        plain(O_IK, IDX_DIM)
    pad(64); plain(O_KR, 32); pad(32)
    for base, nh, d in ((O_DQ, DSA_HEADS, DSA_DIM), (O_DK, DSA_HEADS, DSA_DIM), (O_IQ, IDX_HEADS, IDX_DIM)):
        i, s = _rot_cols(base, nh, d)
        idx.extend(i); sgn.extend(s)
    i, s = _rot_cols(O_IK, 1, IDX_DIM)
    for _ in range(IDX_HEADS):
        idx.extend(i); sgn.extend(s)
    pad(64)
    i, s = _rot_cols(O_KR, 1, MLA_ROPE)
    idx.extend(i); sgn.extend(s)
    pad(32)
    plain(O_IW, IDX_HEADS); pad(N_W - IDX_HEADS)
    plain(O_GATE, N_G)
    assert len(idx) == N_IN
    return np.asarray(idx, np.int32), np.asarray(sgn, np.float32)


_IN_IDX, _IN_SGN = _in_layout()


def _uq_layout():
    per = MLA_NOPE + MLA_ROPE
    idx, sgn = [], []
    for h in range(MLA_HEADS):
        idx.extend(range(h * per, h * per + per)); sgn.extend([1.0] * per)
        idx.extend([0] * 32); sgn.extend([0.0] * 32)
    for h in range(MLA_HEADS):
        idx.extend([0] * MLA_NOPE); sgn.extend([0.0] * MLA_NOPE)
        i, s = _rot_cols(h * per + MLA_NOPE, 1, MLA_ROPE)
        idx.extend(i); sgn.extend(s)
        idx.extend([0] * 32); sgn.extend([0.0] * 32)
    return np.asarray(idx, np.int32), np.asarray(sgn, np.float32)


def _ukv_layout():
    per = MLA_NOPE + MLA_V
    idx, sgn = [], []
    for h in range(MLA_HEADS):
        idx.extend(range(h * per, h * per + MLA_NOPE)); sgn.extend([1.0] * MLA_NOPE)
        idx.extend([0] * 64); sgn.extend([0.0] * 64)
    for h in range(MLA_HEADS):
        idx.extend(range(h * per + MLA_NOPE, h * per + per)); sgn.extend([1.0] * MLA_V)
    return np.asarray(idx, np.int32), np.asarray(sgn, np.float32)


_UQ_IDX, _UQ_SGN = _uq_layout()
_UKV_IDX, _UKV_SGN = _ukv_layout()
N_QM = MLA_HEADS * LANES
N_VM = MLA_HEADS * MLA_V


def _cparams(sem):
    return pltpu.CompilerParams(dimension_semantics=sem, vmem_limit_bytes=VMEM_LIMIT)


def _dot(a, b):
    return jnp.dot(a, b, preferred_element_type=F32)


def _dot_t(a, b):
    return lax.dot_general(a, b, (((1,), (1,)), ((), ())), preferred_element_type=F32)


def _rms(x):
    return x * lax.rsqrt(jnp.mean(x * x, axis=-1, keepdims=True) + NORM_EPS)


def _ada_kernel(c_ref, w_ref, b_ref, o_ref):
    c = c_ref[...]
    sc = c * (1.0 / (1.0 + jnp.exp(-c)))
    o_ref[...] = jnp.dot(sc, w_ref[...], preferred_element_type=F32,
                         precision=lax.Precision.HIGHEST) + b_ref[...]


def _ada_mod(c, ada_w, ada_b):
    depth, d, n = ada_w.shape
    b = c.shape[0]
    rows = 8
    cp = jnp.zeros((rows, d), F32).at[:b].set(c)
    tn = 2048
    out = pl.pallas_call(
        _ada_kernel,
        grid=(depth, n // tn),
        in_specs=[
            pl.BlockSpec((rows, d), lambda l, j: (0, 0)),
            pl.BlockSpec((None, d, tn), lambda l, j: (l, 0, j)),
            pl.BlockSpec((None, 1, tn), lambda l, j: (l, 0, j)),
        ],
        out_specs=pl.BlockSpec((None, rows, tn), lambda l, j: (l, 0, j)),
        out_shape=jax.ShapeDtypeStruct((depth, rows, n), F32),
        compiler_params=_cparams(("arbitrary", "arbitrary")),
        name="ada_mod",
    )(cp, ada_w, ada_b.reshape(depth, 1, n))
    return out[:, :b]


def _inproj_kernel(x_ref, sc_ref, sh_ref, w_ref, cos_ref, sin_ref,
                   a_ref, c_ref, r_ref, w8_ref, g_ref):
    h = (_rms(x_ref[...]) * sc_ref[...] + sh_ref[...]).astype(BF16)
    o = 0
    for c0 in range(0, N_A, 256):
        c1 = min(c0 + 256, N_A)
        a_ref[:, c0:c1] = _dot(h, w_ref[:, o + c0:o + c1]).astype(BF16)
    o += N_A
    c_ref[...] = _dot(h, w_ref[:, o:o + N_C])
    o += N_C
    for lo, n, kind, scale in _ROPE_SPANS:
        y = _dot(h, w_ref[:, o + lo:o + lo + n * LANES])
        yr = _dot(h, w_ref[:, o + N_R + lo:o + N_R + lo + n * LANES])
        cs = jnp.concatenate([cos_ref[:, kind * LANES:(kind + 1) * LANES]] * n, axis=1)
        sn = jnp.concatenate([sin_ref[:, kind * LANES:(kind + 1) * LANES]] * n, axis=1)
        r = y * cs + yr * sn
        if scale != 1.0:
            r = r * scale
        r_ref[:, lo:lo + n * LANES] = r.astype(BF16)
    o += 2 * N_R
    w8_ref[...] = _dot(h, w_ref[:, o:o + N_W]) * (IDX_DIM ** -0.5 * IDX_HEADS ** -0.5)
    o += N_W
    for c0 in range(0, N_G, 512):
        z = _dot(h, w_ref[:, o + c0:o + c0 + 512])
        g_ref[:, c0:c0 + 512] = (1.0 / (1.0 + jnp.exp(-z))).astype(BF16)


def _inproj(x, scale, shift, w, cos_t, sin_t, layer, seq):
    t, d = x.shape
    tm = 512
    per = seq // tm
    return pl.pallas_call(
        _inproj_kernel,
        grid=(t // tm,),
        in_specs=[
            pl.BlockSpec((tm, d), lambda i: (i, 0)),
            pl.BlockSpec((None, None, 1, d), lambda i: (layer, i // per, 0, 0)),
            pl.BlockSpec((None, None, 1, d), lambda i: (layer, i // per, 0, 0)),
            pl.BlockSpec((None, d, N_IN), lambda i: (layer, 0, 0), pipeline_mode=pl.Buffered(1)),
            pl.BlockSpec((tm, 3 * LANES), lambda i: (i, 0)),
            pl.BlockSpec((tm, 3 * LANES), lambda i: (i, 0)),
        ],
        out_specs=[
            pl.BlockSpec((tm, N_A), lambda i: (i, 0)),
            pl.BlockSpec((tm, N_C), lambda i: (i, 0)),
            pl.BlockSpec((tm, N_R), lambda i: (i, 0)),
            pl.BlockSpec((tm, N_W), lambda i: (i, 0)),
            pl.BlockSpec((tm, N_G), lambda i: (i, 0)),
        ],
        out_shape=[
            jax.ShapeDtypeStruct((t, N_A), BF16),
            jax.ShapeDtypeStruct((t, N_C), F32),
            jax.ShapeDtypeStruct((t, N_R), BF16),
            jax.ShapeDtypeStruct((t, N_W), F32),
            jax.ShapeDtypeStruct((t, N_G), BF16),
        ],
        compiler_params=_cparams(("arbitrary",)),
        name="inproj",
    )(x, scale, shift, w, cos_t, sin_t)


def _mla_up_kernel(c_ref, gq_ref, gkv_ref, wq_ref, wkv_ref, cos_ref, sin_ref, kr_ref,
                   q_ref, k_ref, v_ref):
    c = c_ref[...]
    nq = (_rms(c[:, :MLA_Q_RANK]) * gq_ref[...]).astype(BF16)
    nkv = (_rms(c[:, MLA_Q_RANK:]) * gkv_ref[...]).astype(BF16)
    scale = (MLA_NOPE + MLA_ROPE) ** -0.5 * LOG2E
    cs = cos_ref[...] * scale
    sn = sin_ref[...] * scale
    kr = kr_ref[...].astype(F32)
    for h in range(MLA_HEADS):
        lo = h * LANES
        y = _dot(nq, wq_ref[:, lo:lo + LANES])
        yr = _dot(nq, wq_ref[:, N_QM + lo:N_QM + lo + LANES])
        q_ref[:, lo:lo + LANES] = (y * cs + yr * sn).astype(BF16)
        k_ref[:, lo:lo + LANES] = (_dot(nkv, wkv_ref[:, lo:lo + LANES]) + kr).astype(BF16)
    v_ref[...] = _dot(nkv, wkv_ref[:, N_QM:N_QM + N_VM]).astype(BF16)


def _mla_up(cl, gq, gkv, wq, wkv, cos_t, sin_t, rp, layer):
    t = cl.shape[0]
    tm = 512
    return pl.pallas_call(
        _mla_up_kernel,
        grid=(t // tm,),
        in_specs=[
            pl.BlockSpec((tm, N_C), lambda i: (i, 0)),
            pl.BlockSpec((None, 1, MLA_Q_RANK), lambda i: (layer, 0, 0)),
            pl.BlockSpec((None, 1, MLA_KV_RANK), lambda i: (layer, 0, 0)),
            pl.BlockSpec((None, MLA_Q_RANK, 2 * N_QM), lambda i: (layer, 0, 0)),
            pl.BlockSpec((None, MLA_KV_RANK, N_QM + N_VM), lambda i: (layer, 0, 0)),
            pl.BlockSpec((tm, LANES), lambda i: (i, 2)),
            pl.BlockSpec((tm, LANES), lambda i: (i, 2)),
            pl.BlockSpec((tm, LANES), lambda i: (i, 8)),
        ],
        out_specs=[
            pl.BlockSpec((tm, N_QM), lambda i: (i, 0)),
            pl.BlockSpec((tm, N_QM), lambda i: (i, 0)),
            pl.BlockSpec((tm, N_VM), lambda i: (i, 0)),
        ],
        out_shape=[
            jax.ShapeDtypeStruct((t, N_QM), BF16),
            jax.ShapeDtypeStruct((t, N_QM), BF16),
            jax.ShapeDtypeStruct((t, N_VM), BF16),
        ],
        compiler_params=_cparams(("arbitrary",)),
        name="mla_up",
    )(cl, gq, gkv, wq, wkv, cos_t, sin_t, rp)


def _sb_kernel(q_ref, k_ref, v_ref, o_ref, *, tq):
    qi = pl.program_id(2)
    lane = lax.broadcasted_iota(I32, (1, LANES), 1)
    row = lax.broadcasted_iota(I32, (tq, tq), 0)
    col = lax.broadcasted_iota(I32, (tq, tq), 1)
    causal = col < row
    tri = jnp.where(row > col, 1.0, 0.0).astype(BF16)
    q = q_ref[...]
    outs = []
    for h in range(2):
        head = (lane >= h * SB_DIM) & (lane < (h + 1) * SB_DIM)
        qh = jnp.where(head, q, jnp.zeros_like(q)) * jnp.asarray(SB_DIM ** -0.5, BF16)

        def block(j, remain, acc, diagonal, qh=qh):
            start = pl.multiple_of(j * tq, tq)
            kb = k_ref[pl.ds(start, tq), :]
            vb = v_ref[pl.ds(start, tq), :]
            z = _dot_t(qh, kb)
            soft = jnp.log1p(jnp.exp(-jnp.abs(z)))
            log_stay = -(jnp.maximum(z, 0.0) + soft)
            if diagonal:
                log_stay = jnp.where(causal, log_stay, 0.0)
            hi = log_stay.astype(BF16)
            lo = (log_stay - hi.astype(F32)).astype(BF16)
            later = _dot(hi, tri) + _dot(lo, tri)
            log_a = (z + log_stay) + later + remain
            a = jnp.exp(log_a)
            if diagonal:
                a = jnp.where(causal, a, 0.0)
            acc = acc + _dot(a.astype(BF16), vb)
            remain = remain + jnp.sum(log_stay, axis=-1, keepdims=True)
            return remain, acc

        remain, acc = block(qi, jnp.zeros((tq, 1), F32), jnp.zeros((tq, LANES), F32), True)

        def cond(carry):
            j, remain, _ = carry
            return jnp.logical_and(j >= 0, jnp.max(remain) > -SB_UNDERFLOW)

        def body(carry, block=block):
            j, remain, acc = carry
            remain, acc = block(j, remain, acc, False)
            return j - 1, remain, acc

        _, _, acc = lax.while_loop(cond, body, (qi - 1, remain, acc))
        outs.append(acc)
    o_ref[...] = jnp.where(lane < SB_DIM, outs[0], outs[1]).astype(BF16)


def _sb_attention(a, batch, seq):
    t = a.shape[0]
    tq = 256
    nq = seq // tq
    pairs = SB_HEADS // 2
    return pl.pallas_call(
        functools.partial(_sb_kernel, tq=tq),
        grid=(batch, pairs, nq),
        in_specs=[
            pl.BlockSpec((tq, LANES), lambda b, p, i: (b * nq + i, 2 + p)),
            pl.BlockSpec((seq, LANES), lambda b, p, i: (b, 2 + pairs + p)),
            pl.BlockSpec((seq, LANES), lambda b, p, i: (b, 2 + 2 * pairs + p)),
        ],
        out_specs=pl.BlockSpec((tq, LANES), lambda b, p, i: (b * nq + i, p)),
        out_shape=jax.ShapeDtypeStruct((t, SB_HEADS * SB_DIM), BF16),
        compiler_params=_cparams(("arbitrary", "arbitrary", "arbitrary")),
        name="sb_attention",
    )(a, a, a)


def _softmax_step(s, vb, h, acc_ref, l_ref, m_ref):
    reps = s.shape[1] // LANES
    m_old = m_ref[h]
    m_new = jnp.maximum(m_old, jnp.max(s, axis=-1, keepdims=True))
    alpha = jnp.exp2(m_old - m_new)
    p = jnp.exp2(s - jnp.concatenate([m_new] * reps, axis=1))
    part = p[:, 0:LANES]
    for c in range(1, reps):
        part = part + p[:, c * LANES:(c + 1) * LANES]
    l_ref[h] = alpha * l_ref[h] + part
    acc_ref[h] = alpha * acc_ref[h] + _dot(p.astype(BF16), vb)
    m_ref[h] = m_new


def _softmax_init(acc_ref, l_ref, m_ref):
    acc_ref[...] = jnp.zeros_like(acc_ref)
    l_ref[...] = jnp.zeros_like(l_ref)
    m_ref[...] = jnp.full_like(m_ref, NEG_BIG)


def _softmax_out(h, acc_ref, l_ref):
    return acc_ref[h] / jnp.sum(l_ref[h], axis=-1, keepdims=True)


def _mla_kernel(q_ref, k_ref, v_ref, o_ref, acc_ref, l_ref, m_ref, *, tq, tkb):
    qi = pl.program_id(2)
    lane = lax.broadcasted_iota(I32, (1, LANES), 1)
    row = lax.broadcasted_iota(I32, (tq, tq), 0)
    col = lax.broadcasted_iota(I32, (tq, tq), 1)
    visible = (col >> CHUNK_SHIFT) <= (row >> CHUNK_SHIFT)
    q = q_ref[...]
    _softmax_init(acc_ref, l_ref, m_ref)

    def step(start, tk, diagonal):
        kb = k_ref[pl.ds(start, tk), :]
        vb = v_ref[pl.ds(start, tk), :]
        for h in range(2):
            s = _dot_t(q[:, h * LANES:(h + 1) * LANES], kb[:, h * LANES:(h + 1) * LANES])
            if diagonal:
                s = jnp.where(visible, s, NEG_BIG)
            _softmax_step(s, vb, h, acc_ref, l_ref, m_ref)

    n_wide = (qi * tq) // tkb

    def wide(j, c):
        step(pl.multiple_of(j * tkb, tkb), tkb, False)
        return c

    def narrow(j, c):
        step(pl.multiple_of(j * tq, tq), tq, False)
        return c

    lax.fori_loop(0, n_wide, wide, 0)
    lax.fori_loop(n_wide * (tkb // tq), qi, narrow, 0)
    step(pl.multiple_of(qi * tq, tq), tq, True)
    o_ref[...] = jnp.where(lane < MLA_V, _softmax_out(0, acc_ref, l_ref),
                           _softmax_out(1, acc_ref, l_ref)).astype(BF16)


def _mla_attention(qm, km, vm, batch, seq):
    t = qm.shape[0]
    tq = min(512, seq)
    tkb = min(1024, seq)
    nq = seq // tq
    pairs = MLA_HEADS // 2
    stat = pltpu.VMEM((2, tq, LANES), F32)
    return pl.pallas_call(
        functools.partial(_mla_kernel, tq=tq, tkb=tkb),
        grid=(batch, pairs, nq),
        in_specs=[
            pl.BlockSpec((tq, 2 * LANES), lambda b, p, i: (b * nq + i, p)),
            pl.BlockSpec((seq, 2 * LANES), lambda b, p, i: (b, p)),
            pl.BlockSpec((seq, LANES), lambda b, p, i: (b, p)),
        ],
        out_specs=pl.BlockSpec((tq, LANES), lambda b, p, i: (b * nq + i, p)),
        out_shape=jax.ShapeDtypeStruct((t, N_VM), BF16),
        scratch_shapes=[stat, stat, stat],
        compiler_params=_cparams(("arbitrary", "arbitrary", "arbitrary")),
        name="mla_attention",
    )(qm, km, vm)


SEARCH_BISECT_EVERY = 3
SEARCH_MAX_STEPS = 3 * 32 + 4


def _dsa_kernel(qd_ref, qx_ref, w_ref, kx_ref, kd_ref, vd_ref, o_ref,
                key_ref, qs_ref, wr_ref, top_ref, cand_ref, cnt_ref, acc_ref, l_ref, m_ref, *, tq, tk, n_select):
    it = pl.program_id(1)
    t0 = it * tq
    last = t0 // tk
    reps = tk // LANES
    lane = lax.broadcasted_iota(I32, (1, LANES), 1)
    lane2 = lax.broadcasted_iota(I32, (1, 2 * LANES), 1)
    rowid = t0 + lax.broadcasted_iota(I32, (tq, 1), 0)
    row_chunk = rowid >> CHUNK_SHIFT

    qx = qx_ref[...]
    w = w_ref[...]
    for h in range(IDX_HEADS):
        head = (lane2 >= h * IDX_DIM) & (lane2 < (h + 1) * IDX_DIM)
        qs_ref[h * tq:(h + 1) * tq, :] = jnp.where(head, qx, jnp.zeros_like(qx))
        wr_ref[h] = jnp.broadcast_to(w[:, h:h + 1], (tq, LANES))
    top_ref[...] = jnp.full_like(top_ref, INT_MIN)

    def score_block(j, diagonal):
        start = pl.multiple_of(j * tk, tk)
        d = _dot_t(qs_ref[...], kx_ref[pl.ds(start, tk), :])
        score = jnp.zeros((tq, tk), F32)
        for h in range(IDX_HEADS):
            wh = jnp.concatenate([wr_ref[h]] * reps, axis=1)
            score = score + wh * jnp.maximum(d[h * tq:(h + 1) * tq], 0.0)
        score = jnp.where(score == 0.0, 0.0, score)
        bits = pltpu.bitcast(score, I32)
        key = jnp.where(bits < 0, bits ^ jnp.int32(0x7FFFFFFF), bits)
        if diagonal:
            col_chunk = (start + lax.broadcasted_iota(I32, (1, tk), 1)) >> CHUNK_SHIFT
            key = jnp.where(col_chunk <= row_chunk, key, INT_MIN)
        key_ref[j] = key
        t1, t2 = top_ref[0], top_ref[1]
        for c in range(reps):
            x = key[:, c * LANES:(c + 1) * LANES]
            t2 = jnp.maximum(t2, jnp.minimum(t1, x))
            t1 = jnp.maximum(t1, x)
        top_ref[0] = t1
        top_ref[1] = t2

    def score_body(j, c):
        score_block(j, False)
        return c

    lax.fori_loop(0, last, score_body, 0)
    score_block(last, True)
    n_blocks = last + 1

    n_chunks = tq // LANES

    def to_lanes(rep):
        return jnp.concatenate([jnp.transpose(rep[c * LANES:(c + 1) * LANES, :])[0:1, :]
                                for c in range(n_chunks)], axis=1)

    def to_rows(row):
        return jnp.concatenate([jnp.transpose(jnp.broadcast_to(row[:, c * LANES:(c + 1) * LANES], (LANES, LANES)))
                                for c in range(n_chunks)], axis=0)

    def count_ge(cand):
        cand_ref[...] = to_rows(cand)
        cnt_ref[...] = jnp.zeros_like(cnt_ref)

        def body(j, c):
            for half in range(n_chunks):
                rows = slice(half * LANES, (half + 1) * LANES)
                cb = jnp.concatenate([cand_ref[rows, :]] * reps, axis=1)
                ge = jnp.where(key_ref[j, rows, :] >= cb, 1, 0)
                part = ge[:, 0:LANES]
                for cc in range(1, reps):
                    part = part + ge[:, cc * LANES:(cc + 1) * LANES]
                cnt_ref[rows, :] += part
            return c

        lax.fori_loop(0, n_blocks, body, 0)
        parts = cnt_ref[...]
        return jnp.concatenate([jnp.sum(jnp.transpose(parts[c * LANES:(c + 1) * LANES, :]), axis=0, keepdims=True)
                                for c in range(n_chunks)], axis=1)

    def key_value(k):
        return pltpu.bitcast(jnp.where(k < 0, k ^ jnp.int32(0x7FFFFFFF), k), F32)

    def value_key(v):
        bits = pltpu.bitcast(v, I32)
        return jnp.where(bits < 0, bits ^ jnp.int32(0x7FFFFFFF), bits)

    row_l = t0 + lax.broadcasted_iota(I32, (1, tq), 1)
    few = ((row_l >> CHUNK_SHIFT) + 1) * CHUNK <= n_select
    t2_min = jnp.broadcast_to(jnp.min(top_ref[1], axis=-1, keepdims=True), (tq, LANES))
    t1_max = jnp.broadcast_to(jnp.max(top_ref[0], axis=-1, keepdims=True), (tq, LANES))
    lo = to_lanes(t2_min)
    hi = to_lanes(t1_max) + 1
    c_lo = count_ge(lo)
    c_hi = jnp.zeros((1, tq), I32)
    c_pos = count_ge(jnp.full((1, tq), 1, I32))
    c_nn = count_ge(jnp.zeros((1, tq), I32))
    positive = c_pos >= n_select
    negative = c_nn < n_select
    at_zero = jnp.logical_not(jnp.logical_or(positive, negative))
    raise_lo = jnp.logical_and(positive, lo < 1)
    lower_hi = jnp.logical_and(negative, hi > 0)
    lo, c_lo = jnp.where(raise_lo, 1, lo), jnp.where(raise_lo, c_pos, c_lo)
    hi, c_hi = jnp.where(lower_hi, 0, hi), jnp.where(lower_hi, c_nn, c_hi)
    lo, c_lo = jnp.where(at_zero, 0, lo), jnp.where(at_zero, c_nn, c_lo)
    hi, c_hi = jnp.where(at_zero, 1, hi), jnp.where(at_zero, c_pos, c_hi)

    def active_rows(lo, c_lo, hi):
        return jnp.logical_and(jnp.logical_not(few), jnp.logical_and(c_lo > n_select, hi - lo > 1))

    def search_cond(carry):
        step, lo, c_lo, hi, c_hi = carry
        busy = jnp.max(jnp.where(active_rows(lo, c_lo, hi), 1, 0))
        return jnp.logical_and(step < SEARCH_MAX_STEPS, busy > 0)

    def search_body(carry):
        step, lo, c_lo, hi, c_hi = carry
        active = active_rows(lo, c_lo, hi)
        a = jnp.log(c_lo.astype(F32))
        b = jnp.log(c_hi.astype(F32) + 0.5)
        frac = jnp.clip((a - math.log(n_select - 0.25)) / (a - b), 1.0 / 64, 63.0 / 64)
        v_lo, v_hi = key_value(lo), key_value(hi)
        guess = value_key(v_lo + frac * (v_hi - v_lo))
        middle = lo + lax.shift_right_logical(hi - lo, 1)
        cand = jnp.where(step % SEARCH_BISECT_EVERY == SEARCH_BISECT_EVERY - 1, middle, guess)
        cand = jnp.minimum(jnp.maximum(cand, lo + 1), hi - 1)
        c = count_ge(cand)
        up = jnp.logical_and(active, c >= n_select)
        down = jnp.logical_and(active, c < n_select)
        lo = jnp.where(up, cand, lo)
        c_lo = jnp.where(up, c, c_lo)
        hi = jnp.where(down, cand, hi)
        c_hi = jnp.where(down, c, c_hi)
        return step + 1, lo, c_lo, hi, c_hi

    _, lo, c_lo, hi, c_hi = lax.while_loop(search_cond, search_body, (jnp.int32(0), lo, c_lo, hi, c_hi))
    theta = jnp.where(few, INT_MIN, lo)

    tied = jnp.logical_and(jnp.logical_not(few), c_lo > n_select)

    @pl.when(jnp.max(jnp.where(tied, 1, 0)) > 0)
    def _():
        theta_r = to_rows(theta)[:, 0:1]
        tied_r = to_rows(jnp.where(tied, 1, 0))[:, 0:1] > 0
        need = to_rows((n_select - c_hi).astype(F32))[:, 0:1]
        r = lax.broadcasted_iota(I32, (tk, tk), 0)
        c = lax.broadcasted_iota(I32, (tk, tk), 1)
        before = jnp.where(r < c, 1.0, 0.0).astype(BF16)

        def body(j, seen):
            key = key_ref[j]
            eq = jnp.logical_and(key == theta_r, tied_r)
            eqf = jnp.where(eq, 1.0, 0.0)
            rank = seen + _dot(eqf.astype(BF16), before)
            key_ref[j] = jnp.where(jnp.logical_and(eq, rank >= need), INT_MIN, key)
            return seen + jnp.sum(eqf, axis=-1, keepdims=True)

        lax.fori_loop(0, n_blocks, body, jnp.zeros((tq, 1), F32))

    cand_ref[...] = to_rows(jnp.maximum(theta, INT_MIN + 1))

    qd = qd_ref[...]
    q_heads = []
    for h in range(DSA_HEADS):
        blk = qd[:, (h // 2) * LANES:(h // 2 + 1) * LANES]
        head = (lane >= (h % 2) * DSA_DIM) & (lane < (h % 2 + 1) * DSA_DIM)
        q_heads.append(jnp.where(head, blk, jnp.zeros_like(blk)))
    _softmax_init(acc_ref, l_ref, m_ref)

    def attend(j, c):
        start = pl.multiple_of(j * tk, tk)
        sel = key_ref[j] >= jnp.concatenate([cand_ref[...]] * reps, axis=1)
        kb = kd_ref[pl.ds(start, tk), :]
        vb = vd_ref[pl.ds(start, tk), :]
        for h in range(DSA_HEADS):
            p0 = (h // 2) * LANES
            s = jnp.where(sel, _dot_t(q_heads[h], kb[:, p0:p0 + LANES]), NEG_BIG)
            _softmax_step(s, vb[:, p0:p0 + LANES], h, acc_ref, l_ref, m_ref)
        return c

    lax.fori_loop(0, n_blocks, attend, 0)
    for p in range(DSA_HEADS // 2):
        o_ref[:, p * LANES:(p + 1) * LANES] = jnp.where(
            lane < DSA_DIM, _softmax_out(2 * p, acc_ref, l_ref), _softmax_out(2 * p + 1, acc_ref, l_ref)).astype(BF16)


def _dsa_attention(a, rp, w8, batch, seq, n_select):
    t = a.shape[0]
    tq = 256
    tk = min(1024, seq)
    nq = seq // tq
    width = DSA_HEADS * DSA_DIM
    keys = lambda col: pl.BlockSpec((seq, width), lambda b, i: (b, col), pipeline_mode=pl.Buffered(1))
    stat = lambda n: pltpu.VMEM((n, tq, LANES), F32)
    return pl.pallas_call(
        functools.partial(_dsa_kernel, tq=tq, tk=tk, n_select=n_select),
        grid=(batch, nq),
        in_specs=[
            pl.BlockSpec((tq, width), lambda b, i: (b * nq + i, 0)),
            pl.BlockSpec((tq, width), lambda b, i: (b * nq + i, 2)),
            pl.BlockSpec((tq, N_W), lambda b, i: (b * nq + i, 0)),
            keys(3),
            keys(1),
            keys(0),
        ],
        out_specs=pl.BlockSpec((tq, width), lambda b, i: (b * nq + i, 0)),
        out_shape=jax.ShapeDtypeStruct((t, width), BF16),
        scratch_shapes=[
            pltpu.VMEM((seq // tk, tq, tk), I32),
            pltpu.VMEM((IDX_HEADS * tq, width), BF16),
            stat(IDX_HEADS),
            pltpu.VMEM((2, tq, LANES), I32),
            pltpu.VMEM((tq, LANES), I32),
            pltpu.VMEM((tq, LANES), I32),
            stat(DSA_HEADS), stat(DSA_HEADS), stat(DSA_HEADS),
        ],
        compiler_params=_cparams(("arbitrary", "arbitrary")),
        name="dsa_attention",
    )(rp, rp, w8, rp, rp, a)


def _merge_kernel(x_ref, osb_ref, omla_ref, odsa_ref, g_ref, gm_ref, wsb_ref, wmla_ref, wdsa_ref, wout_ref,
                  o_ref):
    d = D_MODEL
    merged = (g_ref[:, 0:d].astype(F32) * _dot(osb_ref[...], wsb_ref[...])
              + g_ref[:, d:2 * d].astype(F32) * _dot(omla_ref[...], wmla_ref[...])
              + g_ref[:, 2 * d:3 * d].astype(F32) * _dot(odsa_ref[...], wdsa_ref[...]))
    o_ref[...] = x_ref[...] + gm_ref[...] * _dot(merged.astype(BF16), wout_ref[...])


def _merge(x, osb, omla, odsa, gates, gm, wsb, wmla, wdsa, wout, layer, seq):
    t, d = x.shape
    tm = 512
    per = seq // tm
    row = lambda i: (i, 0)
    wspec = lambda k: pl.BlockSpec((None, k, d), lambda i: (layer, 0, 0))
    return pl.pallas_call(
        _merge_kernel,
        grid=(t // tm,),
        in_specs=[
            pl.BlockSpec((tm, d), row),
            pl.BlockSpec((tm, osb.shape[1]), row),
            pl.BlockSpec((tm, omla.shape[1]), row),
            pl.BlockSpec((tm, odsa.shape[1]), row),
            pl.BlockSpec((tm, N_G), row),
            pl.BlockSpec((None, None, 1, d), lambda i: (layer, i // per, 0, 0)),
            wspec(osb.shape[1]), wspec(omla.shape[1]), wspec(odsa.shape[1]), wspec(d),
        ],
        out_specs=pl.BlockSpec((tm, d), row),
        out_shape=jax.ShapeDtypeStruct((t, d), F32),
        compiler_params=_cparams(("arbitrary",)),
        name="merge",
    )(x, osb, omla, odsa, gates, gm, wsb, wmla, wdsa, wout)


def _router_kernel(x_ref, sc_ref, sh_ref, rw_ref, rb_ref, h_ref, ti_ref, tw_ref, tr_ref, cnt_ref, run_ref, *, tm):
    @pl.when(pl.program_id(0) == 0)
    def _():
        run_ref[...] = jnp.zeros_like(run_ref)

    h = _rms(x_ref[...]) * sc_ref[...] + sh_ref[...]
    h_ref[...] = h
    logits = jnp.dot(h, rw_ref[...], preferred_element_type=F32, precision=lax.Precision.HIGHEST) + rb_ref[...]
    lane = lax.broadcasted_iota(I32, (tm, LANES), 1)
    work = logits
    vals, hots = [], []
    for _ in range(TOP_K):
        m = jnp.max(work, axis=-1, keepdims=True)
        first = jnp.min(jnp.where(work == m, lane, LANES), axis=-1, keepdims=True)
        hot = lane == first
        vals.append(m)
        hots.append(hot)
        work = jnp.where(hot, -jnp.inf, work)
    exps = [jnp.exp(v - vals[0]) for v in vals]
    denom = exps[0] + exps[1] + exps[2] + exps[3]
    chosen = jnp.zeros((tm, LANES), F32)
    for hot in hots:
        chosen = chosen + jnp.where(hot, 1.0, 0.0)
    r = lax.broadcasted_iota(I32, (tm, tm), 0)
    c = lax.broadcasted_iota(I32, (tm, tm), 1)
    earlier = jnp.where(c < r, 1.0, 0.0).astype(BF16)
    rank_all = _dot(earlier, chosen.astype(BF16)) + run_ref[...]
    ti = jnp.zeros((tm, LANES), I32)
    tw = jnp.zeros((tm, LANES), F32)
    tr = jnp.zeros((tm, LANES), I32)
    for k in range(TOP_K):
        e_k = jnp.sum(jnp.where(hots[k], lane, 0), axis=-1, keepdims=True)
        r_k = jnp.sum(jnp.where(hots[k], rank_all, 0.0), axis=-1, keepdims=True).astype(I32)
        ti = jnp.where(lane == k, e_k, ti)
        tw = jnp.where(lane == k, exps[k] / denom, tw)
        tr = jnp.where(lane == k, r_k, tr)
    ti_ref[...] = ti
    tw_ref[...] = tw
    tr_ref[...] = tr
    run_ref[...] = run_ref[...] + jnp.sum(chosen, axis=0, keepdims=True)
    cnt_ref[...] = run_ref[...]


def _router(x, scale, shift, rw, rb, layer, seq):
    t, d = x.shape
    tm = 256
    per = seq // tm
    row = lambda i: (i, 0)
    return pl.pallas_call(
        functools.partial(_router_kernel, tm=tm),
        grid=(t // tm,),
        in_specs=[
            pl.BlockSpec((tm, d), row),
            pl.BlockSpec((None, None, 1, d), lambda i: (layer, i // per, 0, 0)),
            pl.BlockSpec((None, None, 1, d), lambda i: (layer, i // per, 0, 0)),
            pl.BlockSpec((None, d, LANES), lambda i: (layer, 0, 0)),
            pl.BlockSpec((None, 1, LANES), lambda i: (layer, 0, 0)),
        ],
        out_specs=[
            pl.BlockSpec((tm, d), row),
            pl.BlockSpec((tm, LANES), row),
            pl.BlockSpec((tm, LANES), row),
            pl.BlockSpec((tm, LANES), row),
            pl.BlockSpec((1, LANES), lambda i: (0, 0)),
        ],
        out_shape=[
            jax.ShapeDtypeStruct((t, d), F32),
            jax.ShapeDtypeStruct((t, LANES), I32),
            jax.ShapeDtypeStruct((t, LANES), F32),
            jax.ShapeDtypeStruct((t, LANES), I32),
            jax.ShapeDtypeStruct((1, LANES), F32),
        ],
        scratch_shapes=[pltpu.VMEM((1, LANES), F32)],
        compiler_params=_cparams(("arbitrary",)),
        name="router",
    )(x, scale, shift, rw, rb)


def _expert_kernel(te_ref, tf_ref, nv_ref, x_ref, wgu_ref, bgu_ref, wd_ref, bd_ref, o_ref,
                   wgu_bf, wd_bf):
    i = pl.program_id(0)

    @pl.when(i >= nv_ref[0])
    def _():
        o_ref[...] = jnp.zeros_like(o_ref)

    @pl.when(i < nv_ref[0])
    def _():
        @pl.when(tf_ref[i] == 1)
        def _():
            wgu_bf[...] = wgu_ref[...].astype(BF16)
            wd_bf[...] = wd_ref[...].astype(BF16)

        gu = _dot(x_ref[...].astype(BF16), wgu_bf[...]) + bgu_ref[...]
        gate = jnp.minimum(gu[:, :D_EXPERT], SWIGLU_LIMIT)
        up = jnp.clip(gu[:, D_EXPERT:], -SWIGLU_LIMIT, SWIGLU_LIMIT)
        act = (up + 1.0) * (gate * (1.0 / (1.0 + jnp.exp(-SWIGLU_ALPHA * gate))))
        o_ref[...] = _dot(act.astype(BF16), wd_bf[...]) + bd_ref[...]


def _experts(xs, tile_e, tile_first, n_valid, wgu, bgu, wd, bd, layer, tm):
    p, d = xs.shape
    n_tiles = p // tm
    grid_spec = pltpu.PrefetchScalarGridSpec(
        num_scalar_prefetch=3,
        grid=(n_tiles,),
        in_specs=[
            pl.BlockSpec((tm, d), lambda i, te, tf, nv: (i, 0)),
            pl.BlockSpec((None, None, d, 2 * D_EXPERT), lambda i, te, tf, nv: (layer, te[i], 0, 0)),
            pl.BlockSpec((None, None, 1, 2 * D_EXPERT), lambda i, te, tf, nv: (layer, te[i], 0, 0)),
            pl.BlockSpec((None, None, D_EXPERT, d), lambda i, te, tf, nv: (layer, te[i], 0, 0)),
            pl.BlockSpec((None, None, 1, d), lambda i, te, tf, nv: (layer, te[i], 0, 0)),
        ],
        out_specs=pl.BlockSpec((tm, d), lambda i, te, tf, nv: (i, 0)),
        scratch_shapes=[pltpu.VMEM((d, 2 * D_EXPERT), BF16), pltpu.VMEM((D_EXPERT, d), BF16)],
    )
    return pl.pallas_call(
        _expert_kernel,
        grid_spec=grid_spec,
        out_shape=jax.ShapeDtypeStruct((p, d), F32),
        compiler_params=_cparams(("arbitrary",)),
        name="experts",
    )(tile_e, tile_first, n_valid, xs, wgu, bgu, wd, bd)


def _combine_kernel(x_ref, y0_ref, y1_ref, y2_ref, y3_ref, tw_ref, g_ref, fg_ref, o_ref, *, final):
    tw = tw_ref[...]
    y = tw[:, 0:1] * y0_ref[...]
    for k, y_ref in enumerate((y1_ref, y2_ref, y3_ref), start=1):
        y = y + tw[:, k:k + 1] * y_ref[...]
    x = x_ref[...] + g_ref[...] * y
    if final:
        x = _rms(x) * fg_ref[...]
    o_ref[...] = x


def _combine(x, yg, tw, gf, final_g, layer, seq, final):
    t, d = x.shape
    tm = 256
    nt = t // tm
    per = seq // tm
    slot = lambda k: pl.BlockSpec((tm, d), lambda i: (k * nt + i, 0))
    return pl.pallas_call(
        functools.partial(_combine_kernel, final=final),
        grid=(nt,),
        in_specs=[
            pl.BlockSpec((tm, d), lambda i: (i, 0)),
            slot(0), slot(1), slot(2), slot(3),
            pl.BlockSpec((tm, LANES), lambda i: (i, 0)),
            pl.BlockSpec((None, None, 1, d), lambda i: (layer, i // per, 0, 0)),
            pl.BlockSpec((1, d), lambda i: (0, 0)),
        ],
        out_specs=pl.BlockSpec((tm, d), lambda i: (i, 0)),
        out_shape=jax.ShapeDtypeStruct((t, d), F32),
        compiler_params=_cparams(("arbitrary",)),
        name="combine",
    )(x, yg, yg, yg, yg, tw, gf, final_g)


def _rope_tables(positions):
    pos = positions.reshape(-1).astype(F32)

    def cs(dim):
        inv_freq = ROPE_THETA ** (-jnp.arange(0, dim, 2, dtype=F32) / dim)
        ang = pos[:, None] * inv_freq
        return jnp.cos(ang), jnp.sin(ang)

    c_d, s_d = cs(DSA_DIM)
    c_i, s_i = cs(IDX_DIM)
    c_m, s_m = cs(MLA_ROPE)
    t = pos.shape[0]
    ones, zeros = jnp.ones((t, 64), F32), jnp.zeros((t, 64), F32)
    cos_t = jnp.concatenate([jnp.tile(c_d, (1, 4)), jnp.tile(c_i, (1, 8)),
                             ones, c_m, c_m, ones[:, :32]], axis=1)
    sin_t = jnp.concatenate([jnp.tile(s_d, (1, 4)), jnp.tile(s_i, (1, 8)),
                             zeros, s_m, s_m, zeros[:, :32]], axis=1)
    return cos_t, sin_t


def _permute_cols(w, idx, sgn):
    return (w.at[..., jnp.asarray(idx)].get(mode="promise_in_bounds") * jnp.asarray(sgn)).astype(BF16)


def _dispatch_tables(ti, tr, counts, tm):
    t = ti.shape[0]
    p = t * TOP_K + N_EXPERTS * tm
    n_tiles = p // tm
    cnt = counts.astype(I32)
    padded = ((cnt + tm - 1) // tm) * tm
    ends = jnp.cumsum(padded)
    starts = ends - padded
    pos = starts[ti] + tr
    tok = jnp.repeat(jnp.arange(t, dtype=I32), TOP_K)
    row_src = jnp.zeros((p,), I32).at[pos.reshape(-1)].set(tok, unique_indices=True, mode="promise_in_bounds")
    tile_start = jnp.arange(n_tiles, dtype=I32) * tm
    n_valid = ends[-1] // tm
    tile_e = jnp.sum((ends[None, :] <= tile_start[:, None]).astype(I32), axis=1)
    tile_e = jnp.minimum(tile_e, N_EXPERTS - 1)
    last_e = jnp.max(jnp.where(tile_start < ends[-1], tile_e, 0))
    tile_e = jnp.where(tile_start < ends[-1], tile_e, last_e)
    tile_first = jnp.concatenate([jnp.ones((1,), I32), (tile_e[1:] != tile_e[:-1]).astype(I32)])
    return pos, row_src, tile_e, tile_first, n_valid.astype(I32).reshape(1)


def kernel(x, c, positions, ada_w, ada_b, norm_mix_g, w_in, mla_q_norm_g, mla_kv_norm_g, mla_w_uq, mla_w_ukv,
           w_sb_out, w_mla_out, w_dsa_out, w_out, norm_ffn_g, router_w, router_b, expert_w_gu, expert_b_gu,
           expert_w_down, expert_b_down, final_norm_g):
    batch, seq, d = x.shape
    depth = ada_w.shape[0]
    t = batch * seq
    n_select = min(DSA_TOPK_MAX, seq // 4)
    tm_e = 256

    mod = _ada_mod(c, ada_w, ada_b)
    sh_m, sc_m, g_m, sh_f, sc_f, g_f = [m[:, :, None, :] for m in jnp.split(mod, 6, axis=-1)]
    scale_m = norm_mix_g[:, None, None, :] * (1.0 + sc_m)
    scale_f = norm_ffn_g[:, None, None, :] * (1.0 + sc_f)

    cos_t, sin_t = _rope_tables(positions)
    w_in_p = _permute_cols(w_in, _IN_IDX, _IN_SGN)
    w_uq_p = _permute_cols(mla_w_uq, _UQ_IDX, _UQ_SGN)
    w_ukv_p = _permute_cols(mla_w_ukv, _UKV_IDX, _UKV_SGN)
    w_sb_b, w_mla_b, w_dsa_b, w_out_b = [w.astype(BF16) for w in (w_sb_out, w_mla_out, w_dsa_out, w_out)]
    gq = mla_q_norm_g[:, None, :]
    gkv = mla_kv_norm_g[:, None, :]
    rw_p = jnp.zeros((depth, d, LANES), F32).at[:, :, :N_EXPERTS].set(router_w)
    rb_p = jnp.full((depth, 1, LANES), -jnp.inf, F32).at[:, 0, :N_EXPERTS].set(router_b)
    bgu = expert_b_gu[:, :, None, :]
    bd = expert_b_down[:, :, None, :]
    fg = final_norm_g[None, :]

    xf = x.reshape(t, d)
    for l in range(depth):
        a, cl, rp, w8, gates = _inproj(xf, scale_m, sh_m, w_in_p, cos_t, sin_t, l, seq)
        o_sb = _sb_attention(a, batch, seq)
        qm, km, vm = _mla_up(cl, gq, gkv, w_uq_p, w_ukv_p, cos_t, sin_t, rp, l)
        o_mla = _mla_attention(qm, km, vm, batch, seq)
        o_dsa = _dsa_attention(a, rp, w8, batch, seq, n_select)
        xf = _merge(xf, o_sb, o_mla, o_dsa, gates, g_m, w_sb_b, w_mla_b, w_dsa_b, w_out_b, l, seq)

        hf, ti, tw, tr, counts = _router(xf, scale_f, sh_f, rw_p, rb_p, l, seq)
        pos, row_src, tile_e, tile_first, n_valid = _dispatch_tables(
            ti[:, :TOP_K], tr[:, :TOP_K], counts[0, :N_EXPERTS], tm_e)
        xs = hf.at[row_src].get(mode="promise_in_bounds")
        ys = _experts(xs, tile_e, tile_first, n_valid, expert_w_gu, bgu, expert_w_down, bd, l, tm_e)
        yg = ys.at[pos.T.reshape(-1)].get(mode="promise_in_bounds")
        xf = _combine(xf, yg, tw, g_f, fg, l, seq, l == depth - 1)
    return xf.reshape(batch, seq, d)
```

```python
import functools
import math

import numpy as np
import jax
import jax.numpy as jnp
from jax import lax
from jax.experimental import pallas as pl
from jax.experimental.pallas import tpu as pltpu

F32 = jnp.float32
BF16 = jnp.bfloat16
I32 = jnp.int32
I16 = jnp.int16

D_MODEL = 1024
CHUNK = 64
CHUNK_SHIFT = 6
ROPE_THETA = 10000.0
NORM_EPS = 1e-6
SB_HEADS, SB_DIM = 6, 64
MLA_HEADS, MLA_NOPE, MLA_ROPE, MLA_V = 6, 64, 32, 64
MLA_Q_RANK, MLA_KV_RANK = 256, 128
DSA_HEADS, DSA_DIM = 4, 64
IDX_HEADS, IDX_DIM = 8, 32
DSA_TOPK_MAX = 256
N_EXPERTS, TOP_K = 32, 4
D_EXPERT = D_MODEL
SWIGLU_LIMIT = 7.0
SWIGLU_ALPHA = 1.702

LANES = 128
VMEM_LIMIT = 56 * 1024 * 1024

INT_MIN = -(2 ** 31)
NEG_BIG = -1e30
SB_UNDERFLOW = 104.0
LOG2E = math.log2(math.e)

_SPLIT = (384, 384, 384, 256, 128, 32, 256, 256, 256, 256, 32, 8, 3072)
_OFF = np.concatenate([[0], np.cumsum(_SPLIT)]).astype(np.int64)
(O_SBQ, O_SBK, O_SBV, O_CQ, O_CKV, O_KR, O_DQ, O_DK, O_DV, O_IQ, O_IK, O_IW, O_GATE, D_IN) = [
    int(v) for v in _OFF]

N_A = 1408
N_C = 384
N_R = 1152
N_W = 128
N_G = 3072
N_IN = N_A + N_C + 2 * N_R + N_W + N_G
_ROPE_SPANS = ((0, 2, 0, DSA_DIM ** -0.5 * LOG2E), (256, 2, 0, 1.0), (512, 2, 1, 1.0), (768, 2, 1, 1.0),
               (1024, 1, 2, 1.0))


def _rot_cols(base, n_heads, d):
    half = d // 2
    idx, sgn = [], []
    for h in range(n_heads):
        for j in range(d):
            if j < half:
                idx.append(base + h * d + j + half)
                sgn.append(-1.0)
            else:
                idx.append(base + h * d + j - half)
                sgn.append(1.0)
    return idx, sgn


def _in_layout():
    idx, sgn = [], []

    def plain(base, n):
        idx.extend(range(base, base + n))
        sgn.extend([1.0] * n)

    def pad(n):
        idx.extend([0] * n)
        sgn.extend([0.0] * n)

    plain(O_DV, 256); plain(O_SBQ, 384); plain(O_SBK, 384); plain(O_SBV, 384)
    plain(O_CQ, 256); plain(O_CKV, 128)
    plain(O_DQ, 256); plain(O_DK, 256); plain(O_IQ, 256)
    for _ in range(IDX_HEADS):
        plain(O_IK, IDX_DIM)
    pad(64); plain(O_KR, 32); pad(32)
    for base, nh, d in ((O_DQ, DSA_HEADS, DSA_DIM), (O_DK, DSA_HEADS, DSA_DIM), (O_IQ, IDX_HEADS, IDX_DIM)):
        i, s = _rot_cols(base, nh, d)
        idx.extend(i); sgn.extend(s)
    i, s = _rot_cols(O_IK, 1, IDX_DIM)
    for _ in range(IDX_HEADS):
        idx.extend(i); sgn.extend(s)
    pad(64)
    i, s = _rot_cols(O_KR, 1, MLA_ROPE)
    idx.extend(i); sgn.extend(s)
    pad(32)
    plain(O_IW, IDX_HEADS); pad(N_W - IDX_HEADS)
    plain(O_GATE, N_G)
    assert len(idx) == N_IN
    return np.asarray(idx, np.int32), np.asarray(sgn, np.float32)


_IN_IDX, _IN_SGN = _in_layout()


def _uq_layout():
    per = MLA_NOPE + MLA_ROPE
    idx, sgn = [], []
    for h in range(MLA_HEADS):
        idx.extend(range(h * per, h * per + per)); sgn.extend([1.0] * per)
        idx.extend([0] * 32); sgn.extend([0.0] * 32)
    for h in range(MLA_HEADS):
        idx.extend([0] * MLA_NOPE); sgn.extend([0.0] * MLA_NOPE)
        i, s = _rot_cols(h * per + MLA_NOPE, 1, MLA_ROPE)
        idx.extend(i); sgn.extend(s)
        idx.extend([0] * 32); sgn.extend([0.0] * 32)
    return np.asarray(idx, np.int32), np.asarray(sgn, np.float32)


def _ukv_layout():
    per = MLA_NOPE + MLA_V
    idx, sgn = [], []
    for h in range(MLA_HEADS):
        idx.extend(range(h * per, h * per + MLA_NOPE)); sgn.extend([1.0] * MLA_NOPE)
        idx.extend([0] * 64); sgn.extend([0.0] * 64)
    for h in range(MLA_HEADS):
        idx.extend(range(h * per + MLA_NOPE, h * per + per)); sgn.extend([1.0] * MLA_V)
    return np.asarray(idx, np.int32), np.asarray(sgn, np.float32)


_UQ_IDX, _UQ_SGN = _uq_layout()
_UKV_IDX, _UKV_SGN = _ukv_layout()
N_QM = MLA_HEADS * LANES
N_VM = MLA_HEADS * MLA_V


def _cparams(sem):
    return pltpu.CompilerParams(dimension_semantics=sem, vmem_limit_bytes=VMEM_LIMIT)


def _dot(a, b):
    return jnp.dot(a, b, preferred_element_type=F32)


def _dot_t(a, b):
    return lax.dot_general(a, b, (((1,), (1,)), ((), ())), preferred_element_type=F32)


def _rms(x):
    return x * lax.rsqrt(jnp.mean(x * x, axis=-1, keepdims=True) + NORM_EPS)


def _ada_kernel(c_ref, w_ref, b_ref, o_ref):
    c = c_ref[...]
    sc = c * (1.0 / (1.0 + jnp.exp(-c)))
    o_ref[...] = jnp.dot(sc, w_ref[...], preferred_element_type=F32,
                         precision=lax.Precision.HIGHEST) + b_ref[...]


def _ada_mod(c, ada_w, ada_b):
    depth, d, n = ada_w.shape
    b = c.shape[0]
    rows = 8
    cp = jnp.zeros((rows, d), F32).at[:b].set(c)
    tn = 2048
    out = pl.pallas_call(
        _ada_kernel,
        grid=(depth, n // tn),
        in_specs=[
            pl.BlockSpec((rows, d), lambda l, j: (0, 0)),
            pl.BlockSpec((None, d, tn), lambda l, j: (l, 0, j)),
            pl.BlockSpec((None, 1, tn), lambda l, j: (l, 0, j)),
        ],
        out_specs=pl.BlockSpec((None, rows, tn), lambda l, j: (l, 0, j)),
        out_shape=jax.ShapeDtypeStruct((depth, rows, n), F32),
        compiler_params=_cparams(("arbitrary", "arbitrary")),
        name="ada_mod",
    )(cp, ada_w, ada_b.reshape(depth, 1, n))
    return out[:, :b]


def _inproj_kernel(x_ref, sc_ref, sh_ref, w_ref, cos_ref, sin_ref,
                   a_ref, c_ref, r_ref, w8_ref, g_ref):
    h = (_rms(x_ref[...]) * sc_ref[...] + sh_ref[...]).astype(BF16)
    o = 0
    for c0 in range(0, N_A, 256):
        c1 = min(c0 + 256, N_A)
        a_ref[:, c0:c1] = _dot(h, w_ref[:, o + c0:o + c1]).astype(BF16)
    o += N_A
    c_ref[...] = _dot(h, w_ref[:, o:o + N_C])
    o += N_C
    for lo, n, kind, scale in _ROPE_SPANS:
        y = _dot(h, w_ref[:, o + lo:o + lo + n * LANES])
        yr = _dot(h, w_ref[:, o + N_R + lo:o + N_R + lo + n * LANES])
        cs = jnp.concatenate([cos_ref[:, kind * LANES:(kind + 1) * LANES]] * n, axis=1)
        sn = jnp.concatenate([sin_ref[:, kind * LANES:(kind + 1) * LANES]] * n, axis=1)
        r = y * cs + yr * sn
        if scale != 1.0:
            r = r * scale
        r_ref[:, lo:lo + n * LANES] = r.astype(BF16)
    o += 2 * N_R
    w8_ref[...] = _dot(h, w_ref[:, o:o + N_W]) * (IDX_DIM ** -0.5 * IDX_HEADS ** -0.5)
    o += N_W
    for c0 in range(0, N_G, 512):
        z = _dot(h, w_ref[:, o + c0:o + c0 + 512])
        g_ref[:, c0:c0 + 512] = (1.0 / (1.0 + jnp.exp(-z))).astype(BF16)


def _inproj(x, scale, shift, w, cos_t, sin_t, layer, seq):
    t, d = x.shape
    tm = 512
    per = seq // tm
    return pl.pallas_call(
        _inproj_kernel,
        grid=(t // tm,),
        in_specs=[
            pl.BlockSpec((tm, d), lambda i: (i, 0)),
            pl.BlockSpec((None, None, 1, d), lambda i: (layer, i // per, 0, 0)),
            pl.BlockSpec((None, None, 1, d), lambda i: (layer, i // per, 0, 0)),
            pl.BlockSpec((None, d, N_IN), lambda i: (layer, 0, 0), pipeline_mode=pl.Buffered(1)),
            pl.BlockSpec((tm, 3 * LANES), lambda i: (i, 0)),
            pl.BlockSpec((tm, 3 * LANES), lambda i: (i, 0)),
        ],
        out_specs=[
            pl.BlockSpec((tm, N_A), lambda i: (i, 0)),
            pl.BlockSpec((tm, N_C), lambda i: (i, 0)),
            pl.BlockSpec((tm, N_R), lambda i: (i, 0)),
            pl.BlockSpec((tm, N_W), lambda i: (i, 0)),
            pl.BlockSpec((tm, N_G), lambda i: (i, 0)),
        ],
        out_shape=[
            jax.ShapeDtypeStruct((t, N_A), BF16),
            jax.ShapeDtypeStruct((t, N_C), F32),
            jax.ShapeDtypeStruct((t, N_R), BF16),
            jax.ShapeDtypeStruct((t, N_W), F32),
            jax.ShapeDtypeStruct((t, N_G), BF16),
        ],
        compiler_params=_cparams(("arbitrary",)),
        name="inproj",
    )(x, scale, shift, w, cos_t, sin_t)


def _mla_up_kernel(c_ref, gq_ref, gkv_ref, wq_ref, wkv_ref, cos_ref, sin_ref, kr_ref,
                   q_ref, k_ref, v_ref):
    c = c_ref[...]
    nq = (_rms(c[:, :MLA_Q_RANK]) * gq_ref[...]).astype(BF16)
    nkv = (_rms(c[:, MLA_Q_RANK:]) * gkv_ref[...]).astype(BF16)
    scale = (MLA_NOPE + MLA_ROPE) ** -0.5 * LOG2E
    cs = cos_ref[...] * scale
    sn = sin_ref[...] * scale
    kr = kr_ref[...].astype(F32)
    for h in range(MLA_HEADS):
        lo = h * LANES
        y = _dot(nq, wq_ref[:, lo:lo + LANES])
        yr = _dot(nq, wq_ref[:, N_QM + lo:N_QM + lo + LANES])
        q_ref[:, lo:lo + LANES] = (y * cs + yr * sn).astype(BF16)
        k_ref[:, lo:lo + LANES] = (_dot(nkv, wkv_ref[:, lo:lo + LANES]) + kr).astype(BF16)
    v_ref[...] = _dot(nkv, wkv_ref[:, N_QM:N_QM + N_VM]).astype(BF16)


def _mla_up(cl, gq, gkv, wq, wkv, cos_t, sin_t, rp, layer):
    t = cl.shape[0]
    tm = 512
    return pl.pallas_call(
        _mla_up_kernel,
        grid=(t // tm,),
        in_specs=[
            pl.BlockSpec((tm, N_C), lambda i: (i, 0)),
            pl.BlockSpec((None, 1, MLA_Q_RANK), lambda i: (layer, 0, 0)),
            pl.BlockSpec((None, 1, MLA_KV_RANK), lambda i: (layer, 0, 0)),
            pl.BlockSpec((None, MLA_Q_RANK, 2 * N_QM), lambda i: (layer, 0, 0)),
            pl.BlockSpec((None, MLA_KV_RANK, N_QM + N_VM), lambda i: (layer, 0, 0)),
            pl.BlockSpec((tm, LANES), lambda i: (i, 2)),
            pl.BlockSpec((tm, LANES), lambda i: (i, 2)),
            pl.BlockSpec((tm, LANES), lambda i: (i, 8)),
        ],
        out_specs=[
            pl.BlockSpec((tm, N_QM), lambda i: (i, 0)),
            pl.BlockSpec((tm, N_QM), lambda i: (i, 0)),
            pl.BlockSpec((tm, N_VM), lambda i: (i, 0)),
        ],
        out_shape=[
            jax.ShapeDtypeStruct((t, N_QM), BF16),
            jax.ShapeDtypeStruct((t, N_QM), BF16),
            jax.ShapeDtypeStruct((t, N_VM), BF16),
        ],
        compiler_params=_cparams(("arbitrary",)),
        name="mla_up",
    )(cl, gq, gkv, wq, wkv, cos_t, sin_t, rp)


def _sb_kernel(q_ref, k_ref, v_ref, o_ref, *, tq):
    qi = pl.program_id(2)
    lane = lax.broadcasted_iota(I32, (1, LANES), 1)
    row = lax.broadcasted_iota(I32, (tq, tq), 0)
    col = lax.broadcasted_iota(I32, (tq, tq), 1)
    causal = col < row
    tri = jnp.where(row > col, 1.0, 0.0).astype(BF16)
    q = q_ref[...]
    outs = []
    for h in range(2):
        head = (lane >= h * SB_DIM) & (lane < (h + 1) * SB_DIM)
        qh = jnp.where(head, q, jnp.zeros_like(q)) * jnp.asarray(SB_DIM ** -0.5, BF16)

        def block(j, remain, acc, diagonal, qh=qh):
            start = pl.multiple_of(j * tq, tq)
            kb = k_ref[pl.ds(start, tq), :]
            vb = v_ref[pl.ds(start, tq), :]
            z = _dot_t(qh, kb)
            soft = jnp.log1p(jnp.exp(-jnp.abs(z)))
            log_stay = -(jnp.maximum(z, 0.0) + soft)
            if diagonal:
                log_stay = jnp.where(causal, log_stay, 0.0)
            hi = log_stay.astype(BF16)
            lo = (log_stay - hi.astype(F32)).astype(BF16)
            later = _dot(hi, tri) + _dot(lo, tri)
            log_a = (z + log_stay) + later + remain
            a = jnp.exp(log_a)
            if diagonal:
                a = jnp.where(causal, a, 0.0)
            acc = acc + _dot(a.astype(BF16), vb)
            remain = remain + jnp.sum(log_stay, axis=-1, keepdims=True)
            return remain, acc

        remain, acc = block(qi, jnp.zeros((tq, 1), F32), jnp.zeros((tq, LANES), F32), True)

        def cond(carry):
            j, remain, _ = carry
            return jnp.logical_and(j >= 0, jnp.max(remain) > -SB_UNDERFLOW)

        def body(carry, block=block):
            j, remain, acc = carry
            remain, acc = block(j, remain, acc, False)
            return j - 1, remain, acc

        _, _, acc = lax.while_loop(cond, body, (qi - 1, remain, acc))
        outs.append(acc)
    o_ref[...] = jnp.where(lane < SB_DIM, outs[0], outs[1]).astype(BF16)


def _sb_attention(a, batch, seq):
    t = a.shape[0]
    tq = 256
    nq = seq // tq
    pairs = SB_HEADS // 2
    return pl.pallas_call(
        functools.partial(_sb_kernel, tq=tq),
        grid=(batch, pairs, nq),
        in_specs=[
            pl.BlockSpec((tq, LANES), lambda b, p, i: (b * nq + i, 2 + p)),
            pl.BlockSpec((seq, LANES), lambda b, p, i: (b, 2 + pairs + p)),
            pl.BlockSpec((seq, LANES), lambda b, p, i: (b, 2 + 2 * pairs + p)),
        ],
        out_specs=pl.BlockSpec((tq, LANES), lambda b, p, i: (b * nq + i, p)),
        out_shape=jax.ShapeDtypeStruct((t, SB_HEADS * SB_DIM), BF16),
        compiler_params=_cparams(("arbitrary", "arbitrary", "arbitrary")),
        name="sb_attention",
    )(a, a, a)


def _softmax_step(s, vb, h, acc_ref, l_ref, m_ref):
    reps = s.shape[1] // LANES
    m_old = m_ref[h]
    m_new = jnp.maximum(m_old, jnp.max(s, axis=-1, keepdims=True))
    alpha = jnp.exp2(m_old - m_new)
    p = jnp.exp2(s - jnp.concatenate([m_new] * reps, axis=1))
    part = p[:, 0:LANES]
    for c in range(1, reps):
        part = part + p[:, c * LANES:(c + 1) * LANES]
    l_ref[h] = alpha * l_ref[h] + part
    acc_ref[h] = alpha * acc_ref[h] + _dot(p.astype(BF16), vb)
    m_ref[h] = m_new


def _softmax_init(acc_ref, l_ref, m_ref):
    acc_ref[...] = jnp.zeros_like(acc_ref)
    l_ref[...] = jnp.zeros_like(l_ref)
    m_ref[...] = jnp.full_like(m_ref, NEG_BIG)


def _softmax_out(h, acc_ref, l_ref):
    return acc_ref[h] / jnp.sum(l_ref[h], axis=-1, keepdims=True)


def _mla_kernel(q_ref, k_ref, v_ref, o_ref, acc_ref, l_ref, m_ref, *, tq, tkb):
    qi = pl.program_id(2)
    lane = lax.broadcasted_iota(I32, (1, LANES), 1)
    row = lax.broadcasted_iota(I32, (tq, tq), 0)
    col = lax.broadcasted_iota(I32, (tq, tq), 1)
    visible = (col >> CHUNK_SHIFT) <= (row >> CHUNK_SHIFT)
    q = q_ref[...]
    _softmax_init(acc_ref, l_ref, m_ref)

    def step(start, tk, diagonal):
        kb = k_ref[pl.ds(start, tk), :]
        vb = v_ref[pl.ds(start, tk), :]
        for h in range(2):
            s = _dot_t(q[:, h * LANES:(h + 1) * LANES], kb[:, h * LANES:(h + 1) * LANES])
            if diagonal:
                s = jnp.where(visible, s, NEG_BIG)
            _softmax_step(s, vb, h, acc_ref, l_ref, m_ref)

    n_wide = (qi * tq) // tkb

    def wide(j, c):
        step(pl.multiple_of(j * tkb, tkb), tkb, False)
        return c

    def narrow(j, c):
        step(pl.multiple_of(j * tq, tq), tq, False)
        return c

    lax.fori_loop(0, n_wide, wide, 0)
    lax.fori_loop(n_wide * (tkb // tq), qi, narrow, 0)
    step(pl.multiple_of(qi * tq, tq), tq, True)
    o_ref[...] = jnp.where(lane < MLA_V, _softmax_out(0, acc_ref, l_ref),
                           _softmax_out(1, acc_ref, l_ref)).astype(BF16)


def _mla_attention(qm, km, vm, batch, seq):
    t = qm.shape[0]
    tq = min(512, seq)
    tkb = min(1024, seq)
    nq = seq // tq
    pairs = MLA_HEADS // 2
    stat = pltpu.VMEM((2, tq, LANES), F32)
    return pl.pallas_call(
        functools.partial(_mla_kernel, tq=tq, tkb=tkb),
        grid=(batch, pairs, nq),
        in_specs=[
            pl.BlockSpec((tq, 2 * LANES), lambda b, p, i: (b * nq + i, p)),
            pl.BlockSpec((seq, 2 * LANES), lambda b, p, i: (b, p)),
            pl.BlockSpec((seq, LANES), lambda b, p, i: (b, p)),
        ],
        out_specs=pl.BlockSpec((tq, LANES), lambda b, p, i: (b * nq + i, p)),
        out_shape=jax.ShapeDtypeStruct((t, N_VM), BF16),
        scratch_shapes=[stat, stat, stat],
        compiler_params=_cparams(("arbitrary", "arbitrary", "arbitrary")),
        name="mla_attention",
    )(qm, km, vm)


SEARCH_BISECT_EVERY = 3
SEARCH_MAX_STEPS = 3 * 16 + 4


def _dsa_kernel(qd_ref, qx_ref, w_ref, kx_ref, kd_ref, vd_ref, o_ref,
                key_ref, k16_ref, qs_ref, wr_ref, top_ref, cand_ref, cnt_ref, acc_ref, l_ref, m_ref, *, tq, tk, n_select):
    it = pl.program_id(1)
    t0 = it * tq
    last = t0 // tk
    reps = tk // LANES
    lane = lax.broadcasted_iota(I32, (1, LANES), 1)
    lane2 = lax.broadcasted_iota(I32, (1, 2 * LANES), 1)
    rowid = t0 + lax.broadcasted_iota(I32, (tq, 1), 0)
    row_chunk = rowid >> CHUNK_SHIFT

    qx = qx_ref[...]
    w = w_ref[...]
    for h in range(IDX_HEADS):
        head = (lane2 >= h * IDX_DIM) & (lane2 < (h + 1) * IDX_DIM)
        qs_ref[h * tq:(h + 1) * tq, :] = jnp.where(head, qx, jnp.zeros_like(qx))
        wr_ref[h] = jnp.broadcast_to(w[:, h:h + 1], (tq, LANES))
    top_ref[...] = jnp.full_like(top_ref, INT_MIN)

    def score_block(j, diagonal):
        start = pl.multiple_of(j * tk, tk)
        d = _dot_t(qs_ref[...], kx_ref[pl.ds(start, tk), :])
        score = jnp.zeros((tq, tk), F32)
        for h in range(IDX_HEADS):
            wh = jnp.concatenate([wr_ref[h]] * reps, axis=1)
            score = score + wh * jnp.maximum(d[h * tq:(h + 1) * tq], 0.0)
        score = jnp.where(score == 0.0, 0.0, score)
        bits = pltpu.bitcast(score, I32)
        key = jnp.where(bits < 0, bits ^ jnp.int32(0x7FFFFFFF), bits)
        if diagonal:
            col_chunk = (start + lax.broadcasted_iota(I32, (1, tk), 1)) >> CHUNK_SHIFT
            key = jnp.where(col_chunk <= row_chunk, key, INT_MIN)
        key_ref[j] = key
        k16_ref[j] = (key >> 16).astype(I16)
        t1, t2 = top_ref[0], top_ref[1]
        for c in range(reps):
            x = key[:, c * LANES:(c + 1) * LANES]
            t2 = jnp.maximum(t2, jnp.minimum(t1, x))
            t1 = jnp.maximum(t1, x)
        top_ref[0] = t1
        top_ref[1] = t2

    def score_body(j, c):
        score_block(j, False)
        return c

    lax.fori_loop(0, last, score_body, 0)
    score_block(last, True)
    n_blocks = last + 1

    n_chunks = tq // LANES

    def to_lanes(rep):
        return jnp.concatenate([jnp.transpose(rep[c * LANES:(c + 1) * LANES, :])[0:1, :]
                                for c in range(n_chunks)], axis=1)

    def to_rows(row):
        return jnp.concatenate([jnp.transpose(jnp.broadcast_to(row[:, c * LANES:(c + 1) * LANES], (LANES, LANES)))
                                for c in range(n_chunks)], axis=0)

    def count_prepared():
        cnt_ref[...] = jnp.zeros_like(cnt_ref)

        def body(j, c):
            for half in range(n_chunks):
                rows = slice(half * LANES, (half + 1) * LANES)
                cb = jnp.concatenate([cand_ref[rows, :].astype(I16)] * reps, axis=1)
                ge = jnp.where(k16_ref[j, rows, :] >= cb, jnp.int16(1), jnp.int16(0))
                part = ge[:, 0:LANES]
                for cc in range(1, reps):
                    part = part + ge[:, cc * LANES:(cc + 1) * LANES]
                cnt_ref[rows, :] += part
            return c

        lax.fori_loop(0, n_blocks, body, 0)
        parts = cnt_ref[...].astype(I32)
        return jnp.concatenate([jnp.sum(jnp.transpose(parts[c * LANES:(c + 1) * LANES, :]), axis=0, keepdims=True)
                                for c in range(n_chunks)], axis=1)

    def count_ge(cand):
        cand_ref[...] = to_rows(cand)
        return count_prepared()

    def narrow(lo, c_lo, hi, c_hi, target, frozen, guess):
        def propose(step, lo, c_lo, hi, c_hi):
            active = jnp.logical_and(jnp.logical_not(frozen), jnp.logical_and(c_lo > target, hi - lo > 1))
            middle = lo + lax.shift_right_logical(hi - lo, 1)
            cand = jnp.where(step % SEARCH_BISECT_EVERY == SEARCH_BISECT_EVERY - 1, middle, guess(lo, c_lo, hi, c_hi))
            cand = jnp.minimum(jnp.maximum(cand, lo + 1), hi - 1)
            cand_ref[...] = to_rows(cand)
            return cand, jnp.where(active, 1, 0), jnp.max(jnp.where(active, 1, 0))

        def cond(carry):
            return jnp.logical_and(carry[0] < SEARCH_MAX_STEPS, carry[1] > 0)

        def body(carry):
            step, _, cand, active, lo, c_lo, hi, c_hi = carry
            c = count_prepared()
            up = jnp.logical_and(active > 0, c >= target)
            down = jnp.logical_and(active > 0, c < target)
            lo, c_lo = jnp.where(up, cand, lo), jnp.where(up, c, c_lo)
            hi, c_hi = jnp.where(down, cand, hi), jnp.where(down, c, c_hi)
            cand, active, busy = propose(step + 1, lo, c_lo, hi, c_hi)
            return step + 1, busy, cand, active, lo, c_lo, hi, c_hi

        cand, active, busy = propose(jnp.int32(0), lo, c_lo, hi, c_hi)
        return lax.while_loop(cond, body, (jnp.int32(0), busy, cand, active, lo, c_lo, hi, c_hi))[4:]

    def key_value(k):
        return pltpu.bitcast(jnp.where(k < 0, k ^ jnp.int32(0x7FFFFFFF), k), F32)

    def value_key(v):
        bits = pltpu.bitcast(v, I32)
        return jnp.where(bits < 0, bits ^ jnp.int32(0x7FFFFFFF), bits)

    row_l = t0 + lax.broadcasted_iota(I32, (1, tq), 1)
    few = ((row_l >> CHUNK_SHIFT) + 1) * CHUNK <= n_select

    def guess_upper(lo, c_lo, hi, c_hi):
        a = jnp.log(c_lo.astype(F32))
        b = jnp.log(c_hi.astype(F32) + 0.5)
        frac = jnp.clip((a - math.log(n_select - 0.25)) / (a - b), 1.0 / 64, 63.0 / 64)
        v_lo, v_hi = key_value(lo << 16), key_value((hi << 16) - 1)
        return value_key(v_lo + frac * (v_hi - v_lo)) >> 16

    t2_min = jnp.broadcast_to(jnp.min(top_ref[1], axis=-1, keepdims=True), (tq, LANES))
    t1_max = jnp.broadcast_to(jnp.max(top_ref[0], axis=-1, keepdims=True), (tq, LANES))
    lo = to_lanes(t2_min) >> 16
    hi = (to_lanes(t1_max) >> 16) + 1
    c_lo = count_ge(lo)
    c_hi = jnp.zeros((1, tq), I32)
    c_nn = count_ge(jnp.zeros((1, tq), I32))
    c_pos = count_ge(jnp.ones((1, tq), I32))
    positive = c_pos >= n_select
    negative = c_nn < n_select
    at_zero = jnp.logical_not(jnp.logical_or(positive, negative))
    raise_lo = jnp.logical_and(positive, lo < 1)
    lower_hi = jnp.logical_and(negative, hi > 0)
    lo, c_lo = jnp.where(raise_lo, 1, lo), jnp.where(raise_lo, c_pos, c_lo)
    hi, c_hi = jnp.where(lower_hi, 0, hi), jnp.where(lower_hi, c_nn, c_hi)
    lo, c_lo = jnp.where(at_zero, 0, lo), jnp.where(at_zero, c_nn, c_lo)
    hi, c_hi = jnp.where(at_zero, 1, hi), jnp.where(at_zero, c_pos, c_hi)
    th, c_th, _, c_above = narrow(lo, c_lo, hi, c_hi, n_select, few, guess_upper)

    split = jnp.logical_and(jnp.logical_not(few), c_th > n_select)
    want = n_select - c_above
    cand_ref[...] = to_rows(th)

    def lower_halves(j, c):
        for half in range(n_chunks):
            rows = slice(half * LANES, (half + 1) * LANES)
            key = key_ref[j, rows, :]
            inside = (key >> 16) == jnp.concatenate([cand_ref[rows, :]] * reps, axis=1)
            k16_ref[j, rows, :] = jnp.where(inside, (key & 0xFFFF) - 32768, -32768).astype(I16)
        return c

    lax.fori_loop(0, n_blocks, lower_halves, 0)

    def guess_lower(lo, c_lo, hi, c_hi):
        frac = (c_lo - want).astype(F32) + 0.5
        frac = jnp.clip(frac / (c_lo - c_hi).astype(F32), 1.0 / 64, 63.0 / 64)
        return lo + jnp.floor(frac * (hi - lo).astype(F32)).astype(I32)

    floor = jnp.full((1, tq), -32768, I32)
    c_next = count_ge(floor + 1)
    at_floor = c_next < want
    lo2, c_lo2 = jnp.where(at_floor, floor, floor + 1), jnp.where(at_floor, c_th - c_above, c_next)
    hi2, c_hi2 = jnp.where(at_floor, floor + 1, 32768), jnp.where(at_floor, c_next, 0)
    lo2, c_lo2, _, c_hi2 = narrow(lo2, c_lo2, hi2, c_hi2, want, jnp.logical_not(split), guess_lower)

    theta = jnp.where(split, (th << 16) + (lo2 + 32768), th << 16)
    theta = jnp.where(few, INT_MIN, theta)
    tied = jnp.logical_and(split, c_lo2 > want)
    need_rows = want - c_hi2

    @pl.when(jnp.max(jnp.where(tied, 1, 0)) > 0)
    def _():
        theta_r = to_rows(theta)[:, 0:1]
        tied_r = to_rows(jnp.where(tied, 1, 0))[:, 0:1] > 0
        need = to_rows(need_rows.astype(F32))[:, 0:1]
        r = lax.broadcasted_iota(I32, (tk, tk), 0)
        c = lax.broadcasted_iota(I32, (tk, tk), 1)
        before = jnp.where(r < c, 1.0, 0.0).astype(BF16)

        def body(j, seen):
            key = key_ref[j]
            eq = jnp.logical_and(key == theta_r, tied_r)
            eqf = jnp.where(eq, 1.0, 0.0)
            rank = seen + _dot(eqf.astype(BF16), before)
            key_ref[j] = jnp.where(jnp.logical_and(eq, rank >= need), INT_MIN, key)
            return seen + jnp.sum(eqf, axis=-1, keepdims=True)

        lax.fori_loop(0, n_blocks, body, jnp.zeros((tq, 1), F32))

    cand_ref[...] = to_rows(jnp.maximum(theta, INT_MIN + 1))

    qd = qd_ref[...]
    q_heads = []
    for h in range(DSA_HEADS):
        blk = qd[:, (h // 2) * LANES:(h // 2 + 1) * LANES]
        head = (lane >= (h % 2) * DSA_DIM) & (lane < (h % 2 + 1) * DSA_DIM)
        q_heads.append(jnp.where(head, blk, jnp.zeros_like(blk)))
    _softmax_init(acc_ref, l_ref, m_ref)

    def attend(j, c):
        start = pl.multiple_of(j * tk, tk)
        sel = key_ref[j] >= jnp.concatenate([cand_ref[...]] * reps, axis=1)
        kb = kd_ref[pl.ds(start, tk), :]
        vb = vd_ref[pl.ds(start, tk), :]
        for h in range(DSA_HEADS):
            p0 = (h // 2) * LANES
            s = jnp.where(sel, _dot_t(q_heads[h], kb[:, p0:p0 + LANES]), NEG_BIG)
            _softmax_step(s, vb[:, p0:p0 + LANES], h, acc_ref, l_ref, m_ref)
        return c

    lax.fori_loop(0, n_blocks, attend, 0)
    for p in range(DSA_HEADS // 2):
        o_ref[:, p * LANES:(p + 1) * LANES] = jnp.where(
            lane < DSA_DIM, _softmax_out(2 * p, acc_ref, l_ref), _softmax_out(2 * p + 1, acc_ref, l_ref)).astype(BF16)


def _dsa_attention(a, rp, w8, batch, seq, n_select):
    t = a.shape[0]
    tq = 256
    tk = min(1024, seq)
    nq = seq // tq
    width = DSA_HEADS * DSA_DIM
    keys = lambda col: pl.BlockSpec((seq, width), lambda b, i: (b, col), pipeline_mode=pl.Buffered(1))
    stat = lambda n: pltpu.VMEM((n, tq, LANES), F32)
    return pl.pallas_call(
        functools.partial(_dsa_kernel, tq=tq, tk=tk, n_select=n_select),
        grid=(batch, nq),
        in_specs=[
            pl.BlockSpec((tq, width), lambda b, i: (b * nq + i, 0)),
            pl.BlockSpec((tq, width), lambda b, i: (b * nq + i, 2)),
            pl.BlockSpec((tq, N_W), lambda b, i: (b * nq + i, 0)),
            keys(3),
            keys(1),
            keys(0),
        ],
        out_specs=pl.BlockSpec((tq, width), lambda b, i: (b * nq + i, 0)),
        out_shape=jax.ShapeDtypeStruct((t, width), BF16),
        scratch_shapes=[
            pltpu.VMEM((seq // tk, tq, tk), I32),
            pltpu.VMEM((seq // tk, tq, tk), I16),
            pltpu.VMEM((IDX_HEADS * tq, width), BF16),
            stat(IDX_HEADS),
            pltpu.VMEM((2, tq, LANES), I32),
            pltpu.VMEM((tq, LANES), I32),
            pltpu.VMEM((tq, LANES), I16),
            stat(DSA_HEADS), stat(DSA_HEADS), stat(DSA_HEADS),
        ],
        compiler_params=_cparams(("arbitrary", "arbitrary")),
        name="dsa_attention",
    )(rp, rp, w8, rp, rp, a)


def _merge_kernel(x_ref, osb_ref, omla_ref, odsa_ref, g_ref, gm_ref, wsb_ref, wmla_ref, wdsa_ref, wout_ref,
                  o_ref):
    d = D_MODEL
    merged = (g_ref[:, 0:d].astype(F32) * _dot(osb_ref[...], wsb_ref[...])
              + g_ref[:, d:2 * d].astype(F32) * _dot(omla_ref[...], wmla_ref[...])
              + g_ref[:, 2 * d:3 * d].astype(F32) * _dot(odsa_ref[...], wdsa_ref[...]))
    o_ref[...] = x_ref[...] + gm_ref[...] * _dot(merged.astype(BF16), wout_ref[...])


def _merge(x, osb, omla, odsa, gates, gm, wsb, wmla, wdsa, wout, layer, seq):
    t, d = x.shape
    tm = 512
    per = seq // tm
    row = lambda i: (i, 0)
    wspec = lambda k: pl.BlockSpec((None, k, d), lambda i: (layer, 0, 0))
    return pl.pallas_call(
        _merge_kernel,
        grid=(t // tm,),
        in_specs=[
            pl.BlockSpec((tm, d), row),
            pl.BlockSpec((tm, osb.shape[1]), row),
            pl.BlockSpec((tm, omla.shape[1]), row),
            pl.BlockSpec((tm, odsa.shape[1]), row),
            pl.BlockSpec((tm, N_G), row),
            pl.BlockSpec((None, None, 1, d), lambda i: (layer, i // per, 0, 0)),
            wspec(osb.shape[1]), wspec(omla.shape[1]), wspec(odsa.shape[1]), wspec(d),
        ],
        out_specs=pl.BlockSpec((tm, d), row),
        out_shape=jax.ShapeDtypeStruct((t, d), F32),
        compiler_params=_cparams(("arbitrary",)),
        name="merge",
    )(x, osb, omla, odsa, gates, gm, wsb, wmla, wdsa, wout)


def _router_kernel(x_ref, sc_ref, sh_ref, rw_ref, rb_ref, h_ref, ti_ref, tw_ref, tr_ref, cnt_ref, run_ref, *, tm):
    @pl.when(pl.program_id(0) == 0)
    def _():
        run_ref[...] = jnp.zeros_like(run_ref)

    h = _rms(x_ref[...]) * sc_ref[...] + sh_ref[...]
    h_ref[...] = h
    logits = jnp.dot(h, rw_ref[...], preferred_element_type=F32, precision=lax.Precision.HIGHEST) + rb_ref[...]
    lane = lax.broadcasted_iota(I32, (tm, LANES), 1)
    work = logits
    vals, hots = [], []
    for _ in range(TOP_K):
        m = jnp.max(work, axis=-1, keepdims=True)
        first = jnp.min(jnp.where(work == m, lane, LANES), axis=-1, keepdims=True)
        hot = lane == first
        vals.append(m)
        hots.append(hot)
        work = jnp.where(hot, -jnp.inf, work)
    exps = [jnp.exp(v - vals[0]) for v in vals]
    denom = exps[0] + exps[1] + exps[2] + exps[3]
    chosen = jnp.zeros((tm, LANES), F32)
    for hot in hots:
        chosen = chosen + jnp.where(hot, 1.0, 0.0)
    r = lax.broadcasted_iota(I32, (tm, tm), 0)
    c = lax.broadcasted_iota(I32, (tm, tm), 1)
    earlier = jnp.where(c < r, 1.0, 0.0).astype(BF16)
    rank_all = _dot(earlier, chosen.astype(BF16)) + run_ref[...]
    ti = jnp.zeros((tm, LANES), I32)
    tw = jnp.zeros((tm, LANES), F32)
    tr = jnp.zeros((tm, LANES), I32)
    for k in range(TOP_K):
        e_k = jnp.sum(jnp.where(hots[k], lane, 0), axis=-1, keepdims=True)
        r_k = jnp.sum(jnp.where(hots[k], rank_all, 0.0), axis=-1, keepdims=True).astype(I32)
        ti = jnp.where(lane == k, e_k, ti)
        tw = jnp.where(lane == k, exps[k] / denom, tw)
        tr = jnp.where(lane == k, r_k, tr)
    ti_ref[...] = ti
    tw_ref[...] = tw
    tr_ref[...] = tr
    run_ref[...] = run_ref[...] + jnp.sum(chosen, axis=0, keepdims=True)
    cnt_ref[...] = run_ref[...]


def _router(x, scale, shift, rw, rb, layer, seq):
    t, d = x.shape
    tm = 256
    per = seq // tm
    row = lambda i: (i, 0)
    return pl.pallas_call(
        functools.partial(_router_kernel, tm=tm),
        grid=(t // tm,),
        in_specs=[
            pl.BlockSpec((tm, d), row),
            pl.BlockSpec((None, None, 1, d), lambda i: (layer, i // per, 0, 0)),
            pl.BlockSpec((None, None, 1, d), lambda i: (layer, i // per, 0, 0)),
            pl.BlockSpec((None, d, LANES), lambda i: (layer, 0, 0)),
            pl.BlockSpec((None, 1, LANES), lambda i: (layer, 0, 0)),
        ],
        out_specs=[
            pl.BlockSpec((tm, d), row),
            pl.BlockSpec((tm, LANES), row),
            pl.BlockSpec((tm, LANES), row),
            pl.BlockSpec((tm, LANES), row),
            pl.BlockSpec((1, LANES), lambda i: (0, 0)),
        ],
        out_shape=[
            jax.ShapeDtypeStruct((t, d), F32),
            jax.ShapeDtypeStruct((t, LANES), I32),
            jax.ShapeDtypeStruct((t, LANES), F32),
            jax.ShapeDtypeStruct((t, LANES), I32),
            jax.ShapeDtypeStruct((1, LANES), F32),
        ],
        scratch_shapes=[pltpu.VMEM((1, LANES), F32)],
        compiler_params=_cparams(("arbitrary",)),
        name="router",
    )(x, scale, shift, rw, rb)


def _expert_kernel(te_ref, tf_ref, nv_ref, x_ref, wgu_ref, bgu_ref, wd_ref, bd_ref, o_ref,
                   wgu_bf, wd_bf):
    i = pl.program_id(0)

    @pl.when(i >= nv_ref[0])
    def _():
        o_ref[...] = jnp.zeros_like(o_ref)

    @pl.when(i < nv_ref[0])
    def _():
        @pl.when(tf_ref[i] == 1)
        def _():
            wgu_bf[...] = wgu_ref[...].astype(BF16)
            wd_bf[...] = wd_ref[...].astype(BF16)

        gu = _dot(x_ref[...].astype(BF16), wgu_bf[...]) + bgu_ref[...]
        gate = jnp.minimum(gu[:, :D_EXPERT], SWIGLU_LIMIT)
        up = jnp.clip(gu[:, D_EXPERT:], -SWIGLU_LIMIT, SWIGLU_LIMIT)
        act = (up + 1.0) * (gate * (1.0 / (1.0 + jnp.exp(-SWIGLU_ALPHA * gate))))
        o_ref[...] = _dot(act.astype(BF16), wd_bf[...]) + bd_ref[...]


def _experts(xs, tile_e, tile_first, n_valid, wgu, bgu, wd, bd, layer, tm):
    p, d = xs.shape
    n_tiles = p // tm
    grid_spec = pltpu.PrefetchScalarGridSpec(
        num_scalar_prefetch=3,
        grid=(n_tiles,),
        in_specs=[
            pl.BlockSpec((tm, d), lambda i, te, tf, nv: (i, 0)),
            pl.BlockSpec((None, None, d, 2 * D_EXPERT), lambda i, te, tf, nv: (layer, te[i], 0, 0)),
            pl.BlockSpec((None, None, 1, 2 * D_EXPERT), lambda i, te, tf, nv: (layer, te[i], 0, 0)),
            pl.BlockSpec((None, None, D_EXPERT, d), lambda i, te, tf, nv: (layer, te[i], 0, 0)),
            pl.BlockSpec((None, None, 1, d), lambda i, te, tf, nv: (layer, te[i], 0, 0)),
        ],
        out_specs=pl.BlockSpec((tm, d), lambda i, te, tf, nv: (i, 0)),
        scratch_shapes=[pltpu.VMEM((d, 2 * D_EXPERT), BF16), pltpu.VMEM((D_EXPERT, d), BF16)],
    )
    return pl.pallas_call(
        _expert_kernel,
        grid_spec=grid_spec,
        out_shape=jax.ShapeDtypeStruct((p, d), F32),
        compiler_params=_cparams(("arbitrary",)),
        name="experts",
    )(tile_e, tile_first, n_valid, xs, wgu, bgu, wd, bd)


def _combine_kernel(x_ref, y0_ref, y1_ref, y2_ref, y3_ref, tw_ref, g_ref, fg_ref, o_ref, *, final):
    tw = tw_ref[...]
    y = tw[:, 0:1] * y0_ref[...]
    for k, y_ref in enumerate((y1_ref, y2_ref, y3_ref), start=1):
        y = y + tw[:, k:k + 1] * y_ref[...]
    x = x_ref[...] + g_ref[...] * y
    if final:
        x = _rms(x) * fg_ref[...]
    o_ref[...] = x


def _combine(x, yg, tw, gf, final_g, layer, seq, final):
    t, d = x.shape
    tm = 256
    nt = t // tm
    per = seq // tm
    slot = lambda k: pl.BlockSpec((tm, d), lambda i: (k * nt + i, 0))
    return pl.pallas_call(
        functools.partial(_combine_kernel, final=final),
        grid=(nt,),
        in_specs=[
            pl.BlockSpec((tm, d), lambda i: (i, 0)),
            slot(0), slot(1), slot(2), slot(3),
            pl.BlockSpec((tm, LANES), lambda i: (i, 0)),
            pl.BlockSpec((None, None, 1, d), lambda i: (layer, i // per, 0, 0)),
            pl.BlockSpec((1, d), lambda i: (0, 0)),
        ],
        out_specs=pl.BlockSpec((tm, d), lambda i: (i, 0)),
        out_shape=jax.ShapeDtypeStruct((t, d), F32),
        compiler_params=_cparams(("arbitrary",)),
        name="combine",
    )(x, yg, yg, yg, yg, tw, gf, final_g)


def _rope_tables(positions):
    pos = positions.reshape(-1).astype(F32)

    def cs(dim):
        inv_freq = ROPE_THETA ** (-jnp.arange(0, dim, 2, dtype=F32) / dim)
        ang = pos[:, None] * inv_freq
        return jnp.cos(ang), jnp.sin(ang)

    c_d, s_d = cs(DSA_DIM)
    c_i, s_i = cs(IDX_DIM)
    c_m, s_m = cs(MLA_ROPE)
    t = pos.shape[0]
    ones, zeros = jnp.ones((t, 64), F32), jnp.zeros((t, 64), F32)
    cos_t = jnp.concatenate([jnp.tile(c_d, (1, 4)), jnp.tile(c_i, (1, 8)),
                             ones, c_m, c_m, ones[:, :32]], axis=1)
    sin_t = jnp.concatenate([jnp.tile(s_d, (1, 4)), jnp.tile(s_i, (1, 8)),
                             zeros, s_m, s_m, zeros[:, :32]], axis=1)
    return cos_t, sin_t


def _permute_cols(w, idx, sgn):
    return (w.at[..., jnp.asarray(idx)].get(mode="promise_in_bounds") * jnp.asarray(sgn)).astype(BF16)


def _dispatch_tables(ti, tr, counts, tm):
    t = ti.shape[0]
    p = t * TOP_K + N_EXPERTS * tm
    n_tiles = p // tm
    cnt = counts.astype(I32)
    padded = ((cnt + tm - 1) // tm) * tm
    ends = jnp.cumsum(padded)
    starts = ends - padded
    pos = starts[ti] + tr
    tok = jnp.repeat(jnp.arange(t, dtype=I32), TOP_K)
    row_src = jnp.zeros((p,), I32).at[pos.reshape(-1)].set(tok, unique_indices=True, mode="promise_in_bounds")
    tile_start = jnp.arange(n_tiles, dtype=I32) * tm
    n_valid = ends[-1] // tm
    tile_e = jnp.sum((ends[None, :] <= tile_start[:, None]).astype(I32), axis=1)
    tile_e = jnp.minimum(tile_e, N_EXPERTS - 1)
    last_e = jnp.max(jnp.where(tile_start < ends[-1], tile_e, 0))
    tile_e = jnp.where(tile_start < ends[-1], tile_e, last_e)
    tile_first = jnp.concatenate([jnp.ones((1,), I32), (tile_e[1:] != tile_e[:-1]).astype(I32)])
    return pos, row_src, tile_e, tile_first, n_valid.astype(I32).reshape(1)


def kernel(x, c, positions, ada_w, ada_b, norm_mix_g, w_in, mla_q_norm_g, mla_kv_norm_g, mla_w_uq, mla_w_ukv,
           w_sb_out, w_mla_out, w_dsa_out, w_out, norm_ffn_g, router_w, router_b, expert_w_gu, expert_b_gu,
           expert_w_down, expert_b_down, final_norm_g):
    batch, seq, d = x.shape
    depth = ada_w.shape[0]
    t = batch * seq
    n_select = min(DSA_TOPK_MAX, seq // 4)
    tm_e = 256

    mod = _ada_mod(c, ada_w, ada_b)
    sh_m, sc_m, g_m, sh_f, sc_f, g_f = [m[:, :, None, :] for m in jnp.split(mod, 6, axis=-1)]
    scale_m = norm_mix_g[:, None, None, :] * (1.0 + sc_m)
    scale_f = norm_ffn_g[:, None, None, :] * (1.0 + sc_f)

    cos_t, sin_t = _rope_tables(positions)
    w_in_p = _permute_cols(w_in, _IN_IDX, _IN_SGN)
    w_uq_p = _permute_cols(mla_w_uq, _UQ_IDX, _UQ_SGN)
    w_ukv_p = _permute_cols(mla_w_ukv, _UKV_IDX, _UKV_SGN)
    w_sb_b, w_mla_b, w_dsa_b, w_out_b = [w.astype(BF16) for w in (w_sb_out, w_mla_out, w_dsa_out, w_out)]
    gq = mla_q_norm_g[:, None, :]
    gkv = mla_kv_norm_g[:, None, :]
    rw_p = jnp.zeros((depth, d, LANES), F32).at[:, :, :N_EXPERTS].set(router_w)
    rb_p = jnp.full((depth, 1, LANES), -jnp.inf, F32).at[:, 0, :N_EXPERTS].set(router_b)
    bgu = expert_b_gu[:, :, None, :]
    bd = expert_b_down[:, :, None, :]
    fg = final_norm_g[None, :]

    xf = x.reshape(t, d)
    for l in range(depth):
        a, cl, rp, w8, gates = _inproj(xf, scale_m, sh_m, w_in_p, cos_t, sin_t, l, seq)
        o_sb = _sb_attention(a, batch, seq)
        qm, km, vm = _mla_up(cl, gq, gkv, w_uq_p, w_ukv_p, cos_t, sin_t, rp, l)
        o_mla = _mla_attention(qm, km, vm, batch, seq)
        o_dsa = _dsa_attention(a, rp, w8, batch, seq, n_select)
        xf = _merge(xf, o_sb, o_mla, o_dsa, gates, g_m, w_sb_b, w_mla_b, w_dsa_b, w_out_b, l, seq)

        hf, ti, tw, tr, counts = _router(xf, scale_f, sh_f, rw_p, rb_p, l, seq)
        pos, row_src, tile_e, tile_first, n_valid = _dispatch_tables(
            ti[:, :TOP_K], tr[:, :TOP_K], counts[0, :N_EXPERTS], tm_e)
        xs = hf.at[row_src].get(mode="promise_in_bounds")
        ys = _experts(xs, tile_e, tile_first, n_valid, expert_w_gu, bgu, expert_w_down, bd, l, tm_e)
        yg = ys.at[pos.T.reshape(-1)].get(mode="promise_in_bounds")
        xf = _combine(xf, yg, tw, g_f, fg, l, seq, l == depth - 1)
    return xf.reshape(batch, seq, d)
```

```python
import functools
import math

import numpy as np
import jax
import jax.numpy as jnp
from jax import lax
from jax.experimental import pallas as pl
from jax.experimental.pallas import tpu as pltpu

F32 = jnp.float32
BF16 = jnp.bfloat16
I32 = jnp.int32

D_MODEL = 1024
CHUNK = 64
CHUNK_SHIFT = 6
ROPE_THETA = 10000.0
NORM_EPS = 1e-6
SB_HEADS, SB_DIM = 6, 64
MLA_HEADS, MLA_NOPE, MLA_ROPE, MLA_V = 6, 64, 32, 64
MLA_Q_RANK, MLA_KV_RANK = 256, 128
DSA_HEADS, DSA_DIM = 4, 64
IDX_HEADS, IDX_DIM = 8, 32
DSA_TOPK_MAX = 256
N_EXPERTS, TOP_K = 32, 4
D_EXPERT = D_MODEL
SWIGLU_LIMIT = 7.0
SWIGLU_ALPHA = 1.702

LANES = 128
VMEM_LIMIT = 56 * 1024 * 1024

INT_MIN = -(2 ** 31)
NEG_BIG = -1e30
SB_UNDERFLOW = 104.0
LOG2E = math.log2(math.e)

_SPLIT = (384, 384, 384, 256, 128, 32, 256, 256, 256, 256, 32, 8, 3072)
_OFF = np.concatenate([[0], np.cumsum(_SPLIT)]).astype(np.int64)
(O_SBQ, O_SBK, O_SBV, O_CQ, O_CKV, O_KR, O_DQ, O_DK, O_DV, O_IQ, O_IK, O_IW, O_GATE, D_IN) = [
    int(v) for v in _OFF]

N_A = 1408
N_C = 384
N_R = 1152
N_W = 128
N_G = 3072
N_IN = N_A + N_C + 2 * N_R + N_W + N_G
_ROPE_SPANS = ((0, 2, 0, DSA_DIM ** -0.5 * LOG2E), (256, 2, 0, 1.0), (512, 2, 1, 1.0), (768, 2, 1, 1.0),
               (1024, 1, 2, 1.0))


def _rot_cols(base, n_heads, d):
    half = d // 2
    idx, sgn = [], []
    for h in range(n_heads):
        for j in range(d):
            if j < half:
                idx.append(base + h * d + j + half)
                sgn.append(-1.0)
            else:
                idx.append(base + h * d + j - half)
                sgn.append(1.0)
    return idx, sgn


def _in_layout():
    idx, sgn = [], []

    def plain(base, n):
        idx.extend(range(base, base + n))
        sgn.extend([1.0] * n)

    def pad(n):
        idx.extend([0] * n)
        sgn.extend([0.0] * n)

    plain(O_DV, 256); plain(O_SBQ, 384); plain(O_SBK, 384); plain(O_SBV, 384)
    plain(O_CQ, 256); plain(O_CKV, 128)
    plain(O_DQ, 256); plain(O_DK, 256); plain(O_IQ, 256)
    for _ in range(IDX_HEADS):
        plain(O_IK, IDX_DIM)
    pad(64); plain(O_KR, 32); pad(32)
    for base, nh, d in ((O_DQ, DSA_HEADS, DSA_DIM), (O_DK, DSA_HEADS, DSA_DIM), (O_IQ, IDX_HEADS, IDX_DIM)):
        i, s = _rot_cols(base, nh, d)
        idx.extend(i); sgn.extend(s)
    i, s = _rot_cols(O_IK, 1, IDX_DIM)
    for _ in range(IDX_HEADS):
        idx.extend(i); sgn.extend(s)
    pad(64)
    i, s = _rot_cols(O_KR, 1, MLA_ROPE)
    idx.extend(i); sgn.extend(s)
    pad(32)
    plain(O_IW, IDX_HEADS); pad(N_W - IDX_HEADS)
    plain(O_GATE, N_G)
    assert len(idx) == N_IN
    return np.asarray(idx, np.int32), np.asarray(sgn, np.float32)


_IN_IDX, _IN_SGN = _in_layout()


def _uq_layout():
    per = MLA_NOPE + MLA_ROPE
    idx, sgn = [], []
    for h in range(MLA_HEADS):
        idx.extend(range(h * per, h * per + per)); sgn.extend([1.0] * per)
        idx.extend([0] * 32); sgn.extend([0.0] * 32)
    for h in range(MLA_HEADS):
        idx.extend([0] * MLA_NOPE); sgn.extend([0.0] * MLA_NOPE)
        i, s = _rot_cols(h * per + MLA_NOPE, 1, MLA_ROPE)
        idx.extend(i); sgn.extend(s)
        idx.extend([0] * 32); sgn.extend([0.0] * 32)
    return np.asarray(idx, np.int32), np.asarray(sgn, np.float32)


def _ukv_layout():
    per = MLA_NOPE + MLA_V
    idx, sgn = [], []
    for h in range(MLA_HEADS):
        idx.extend(range(h * per, h * per + MLA_NOPE)); sgn.extend([1.0] * MLA_NOPE)
        idx.extend([0] * 64); sgn.extend([0.0] * 64)
    for h in range(MLA_HEADS):
        idx.extend(range(h * per + MLA_NOPE, h * per + per)); sgn.extend([1.0] * MLA_V)
    return np.asarray(idx, np.int32), np.asarray(sgn, np.float32)


_UQ_IDX, _UQ_SGN = _uq_layout()
_UKV_IDX, _UKV_SGN = _ukv_layout()
N_QM = MLA_HEADS * LANES
N_VM = MLA_HEADS * MLA_V


def _cparams(sem):
    return pltpu.CompilerParams(dimension_semantics=sem, vmem_limit_bytes=VMEM_LIMIT)


def _dot(a, b):
    return jnp.dot(a, b, preferred_element_type=F32)


def _dot_t(a, b):
    return lax.dot_general(a, b, (((1,), (1,)), ((), ())), preferred_element_type=F32)


def _rms(x):
    return x * lax.rsqrt(jnp.mean(x * x, axis=-1, keepdims=True) + NORM_EPS)


def _ada_kernel(c_ref, w_ref, b_ref, o_ref):
    c = c_ref[...]
    sc = c * (1.0 / (1.0 + jnp.exp(-c)))
    o_ref[...] = jnp.dot(sc, w_ref[...], preferred_element_type=F32,
                         precision=lax.Precision.HIGHEST) + b_ref[...]


def _ada_mod(c, ada_w, ada_b):
    depth, d, n = ada_w.shape
    b = c.shape[0]
    rows = 8
    cp = jnp.zeros((rows, d), F32).at[:b].set(c)
    tn = 2048
    out = pl.pallas_call(
        _ada_kernel,
        grid=(depth, n // tn),
        in_specs=[
            pl.BlockSpec((rows, d), lambda l, j: (0, 0)),
            pl.BlockSpec((None, d, tn), lambda l, j: (l, 0, j)),
            pl.BlockSpec((None, 1, tn), lambda l, j: (l, 0, j)),
        ],
        out_specs=pl.BlockSpec((None, rows, tn), lambda l, j: (l, 0, j)),
        out_shape=jax.ShapeDtypeStruct((depth, rows, n), F32),
        compiler_params=_cparams(("arbitrary", "arbitrary")),
        name="ada_mod",
    )(cp, ada_w, ada_b.reshape(depth, 1, n))
    return out[:, :b]


def _inproj_kernel(x_ref, sc_ref, sh_ref, w_ref, cos_ref, sin_ref,
                   a_ref, c_ref, r_ref, w8_ref, g_ref):
    h = (_rms(x_ref[...]) * sc_ref[...] + sh_ref[...]).astype(BF16)
    o = 0
    for c0 in range(0, N_A, 256):
        c1 = min(c0 + 256, N_A)
        a_ref[:, c0:c1] = _dot(h, w_ref[:, o + c0:o + c1]).astype(BF16)
    o += N_A
    c_ref[...] = _dot(h, w_ref[:, o:o + N_C])
    o += N_C
    for lo, n, kind, scale in _ROPE_SPANS:
        y = _dot(h, w_ref[:, o + lo:o + lo + n * LANES])
        yr = _dot(h, w_ref[:, o + N_R + lo:o + N_R + lo + n * LANES])
        cs = jnp.concatenate([cos_ref[:, kind * LANES:(kind + 1) * LANES]] * n, axis=1)
        sn = jnp.concatenate([sin_ref[:, kind * LANES:(kind + 1) * LANES]] * n, axis=1)
        r = y * cs + yr * sn
        if scale != 1.0:
            r = r * scale
        r_ref[:, lo:lo + n * LANES] = r.astype(BF16)
    o += 2 * N_R
    w8_ref[...] = _dot(h, w_ref[:, o:o + N_W]) * (IDX_DIM ** -0.5 * IDX_HEADS ** -0.5)
    o += N_W
    for c0 in range(0, N_G, 512):
        z = _dot(h, w_ref[:, o + c0:o + c0 + 512])
        g_ref[:, c0:c0 + 512] = (1.0 / (1.0 + jnp.exp(-z))).astype(BF16)


def _inproj(x, scale, shift, w, cos_t, sin_t, layer, seq):
    t, d = x.shape
    tm = 512
    per = seq // tm
    return pl.pallas_call(
        _inproj_kernel,
        grid=(t // tm,),
        in_specs=[
            pl.BlockSpec((tm, d), lambda i: (i, 0)),
            pl.BlockSpec((None, None, 1, d), lambda i: (layer, i // per, 0, 0)),
            pl.BlockSpec((None, None, 1, d), lambda i: (layer, i // per, 0, 0)),
            pl.BlockSpec((None, d, N_IN), lambda i: (layer, 0, 0), pipeline_mode=pl.Buffered(1)),
            pl.BlockSpec((tm, 3 * LANES), lambda i: (i, 0)),
            pl.BlockSpec((tm, 3 * LANES), lambda i: (i, 0)),
        ],
        out_specs=[
            pl.BlockSpec((tm, N_A), lambda i: (i, 0)),
            pl.BlockSpec((tm, N_C), lambda i: (i, 0)),
            pl.BlockSpec((tm, N_R), lambda i: (i, 0)),
            pl.BlockSpec((tm, N_W), lambda i: (i, 0)),
            pl.BlockSpec((tm, N_G), lambda i: (i, 0)),
        ],
        out_shape=[
            jax.ShapeDtypeStruct((t, N_A), BF16),
            jax.ShapeDtypeStruct((t, N_C), F32),
            jax.ShapeDtypeStruct((t, N_R), BF16),
            jax.ShapeDtypeStruct((t, N_W), F32),
            jax.ShapeDtypeStruct((t, N_G), BF16),
        ],
        compiler_params=_cparams(("arbitrary",)),
        name="inproj",
    )(x, scale, shift, w, cos_t, sin_t)


def _mla_up_kernel(c_ref, gq_ref, gkv_ref, wq_ref, wkv_ref, cos_ref, sin_ref, kr_ref,
                   q_ref, k_ref, v_ref):
    c = c_ref[...]
    nq = (_rms(c[:, :MLA_Q_RANK]) * gq_ref[...]).astype(BF16)
    nkv = (_rms(c[:, MLA_Q_RANK:]) * gkv_ref[...]).astype(BF16)
    scale = (MLA_NOPE + MLA_ROPE) ** -0.5 * LOG2E
    cs = cos_ref[...] * scale
    sn = sin_ref[...] * scale
    kr = kr_ref[...].astype(F32)
    for h in range(MLA_HEADS):
        lo = h * LANES
        y = _dot(nq, wq_ref[:, lo:lo + LANES])
        yr = _dot(nq, wq_ref[:, N_QM + lo:N_QM + lo + LANES])
        q_ref[:, lo:lo + LANES] = (y * cs + yr * sn).astype(BF16)
        k_ref[:, lo:lo + LANES] = (_dot(nkv, wkv_ref[:, lo:lo + LANES]) + kr).astype(BF16)
    v_ref[...] = _dot(nkv, wkv_ref[:, N_QM:N_QM + N_VM]).astype(BF16)


def _mla_up(cl, gq, gkv, wq, wkv, cos_t, sin_t, rp, layer):
    t = cl.shape[0]
    tm = 512
    return pl.pallas_call(
        _mla_up_kernel,
        grid=(t // tm,),
        in_specs=[
            pl.BlockSpec((tm, N_C), lambda i: (i, 0)),
            pl.BlockSpec((None, 1, MLA_Q_RANK), lambda i: (layer, 0, 0)),
            pl.BlockSpec((None, 1, MLA_KV_RANK), lambda i: (layer, 0, 0)),
            pl.BlockSpec((None, MLA_Q_RANK, 2 * N_QM), lambda i: (layer, 0, 0)),
            pl.BlockSpec((None, MLA_KV_RANK, N_QM + N_VM), lambda i: (layer, 0, 0)),
            pl.BlockSpec((tm, LANES), lambda i: (i, 2)),
            pl.BlockSpec((tm, LANES), lambda i: (i, 2)),
            pl.BlockSpec((tm, LANES), lambda i: (i, 8)),
        ],
        out_specs=[
            pl.BlockSpec((tm, N_QM), lambda i: (i, 0)),
            pl.BlockSpec((tm, N_QM), lambda i: (i, 0)),
            pl.BlockSpec((tm, N_VM), lambda i: (i, 0)),
        ],
        out_shape=[
            jax.ShapeDtypeStruct((t, N_QM), BF16),
            jax.ShapeDtypeStruct((t, N_QM), BF16),
            jax.ShapeDtypeStruct((t, N_VM), BF16),
        ],
        compiler_params=_cparams(("arbitrary",)),
        name="mla_up",
    )(cl, gq, gkv, wq, wkv, cos_t, sin_t, rp)


def _sb_kernel(q_ref, k_ref, v_ref, o_ref, *, tq):
    qi = pl.program_id(2)
    lane = lax.broadcasted_iota(I32, (1, LANES), 1)
    row = lax.broadcasted_iota(I32, (tq, tq), 0)
    col = lax.broadcasted_iota(I32, (tq, tq), 1)
    causal = col < row
    tri = jnp.where(row > col, 1.0, 0.0).astype(BF16)
    q = q_ref[...]
    outs = []
    for h in range(2):
        head = (lane >= h * SB_DIM) & (lane < (h + 1) * SB_DIM)
        qh = jnp.where(head, q, jnp.zeros_like(q)) * jnp.asarray(SB_DIM ** -0.5, BF16)

        def block(j, remain, acc, diagonal, qh=qh):
            start = pl.multiple_of(j * tq, tq)
            kb = k_ref[pl.ds(start, tq), :]
            vb = v_ref[pl.ds(start, tq), :]
            z = _dot_t(qh, kb)
            soft = jnp.log1p(jnp.exp(-jnp.abs(z)))
            log_stay = -(jnp.maximum(z, 0.0) + soft)
            if diagonal:
                log_stay = jnp.where(causal, log_stay, 0.0)
            hi = log_stay.astype(BF16)
            lo = (log_stay - hi.astype(F32)).astype(BF16)
            later = _dot(hi, tri) + _dot(lo, tri)
            log_a = (z + log_stay) + later + remain
            a = jnp.exp(log_a)
            if diagonal:
                a = jnp.where(causal, a, 0.0)
            acc = acc + _dot(a.astype(BF16), vb)
            remain = remain + jnp.sum(log_stay, axis=-1, keepdims=True)
            return remain, acc

        remain, acc = block(qi, jnp.zeros((tq, 1), F32), jnp.zeros((tq, LANES), F32), True)

        def cond(carry):
            j, remain, _ = carry
            return jnp.logical_and(j >= 0, jnp.max(remain) > -SB_UNDERFLOW)

        def body(carry, block=block):
            j, remain, acc = carry
            remain, acc = block(j, remain, acc, False)
            return j - 1, remain, acc

        _, _, acc = lax.while_loop(cond, body, (qi - 1, remain, acc))
        outs.append(acc)
    o_ref[...] = jnp.where(lane < SB_DIM, outs[0], outs[1]).astype(BF16)


def _sb_attention(a, batch, seq):
    t = a.shape[0]
    tq = 256
    nq = seq // tq
    pairs = SB_HEADS // 2
    return pl.pallas_call(
        functools.partial(_sb_kernel, tq=tq),
        grid=(batch, pairs, nq),
        in_specs=[
            pl.BlockSpec((tq, LANES), lambda b, p, i: (b * nq + i, 2 + p)),
            pl.BlockSpec((seq, LANES), lambda b, p, i: (b, 2 + pairs + p)),
            pl.BlockSpec((seq, LANES), lambda b, p, i: (b, 2 + 2 * pairs + p)),
        ],
        out_specs=pl.BlockSpec((tq, LANES), lambda b, p, i: (b * nq + i, p)),
        out_shape=jax.ShapeDtypeStruct((t, SB_HEADS * SB_DIM), BF16),
        compiler_params=_cparams(("arbitrary", "arbitrary", "arbitrary")),
        name="sb_attention",
    )(a, a, a)


def _softmax_step(s, vb, h, acc_ref, l_ref, m_ref):
    reps = s.shape[1] // LANES
    m_old = m_ref[h]
    m_new = jnp.maximum(m_old, jnp.max(s, axis=-1, keepdims=True))
    alpha = jnp.exp2(m_old - m_new)
    p = jnp.exp2(s - jnp.concatenate([m_new] * reps, axis=1))
    part = p[:, 0:LANES]
    for c in range(1, reps):
        part = part + p[:, c * LANES:(c + 1) * LANES]
    l_ref[h] = alpha * l_ref[h] + part
    acc_ref[h] = alpha * acc_ref[h] + _dot(p.astype(BF16), vb)
    m_ref[h] = m_new


def _softmax_init(acc_ref, l_ref, m_ref):
    acc_ref[...] = jnp.zeros_like(acc_ref)
    l_ref[...] = jnp.zeros_like(l_ref)
    m_ref[...] = jnp.full_like(m_ref, NEG_BIG)


def _softmax_out(h, acc_ref, l_ref):
    return acc_ref[h] / jnp.sum(l_ref[h], axis=-1, keepdims=True)


def _mla_kernel(q_ref, k_ref, v_ref, o_ref, acc_ref, l_ref, m_ref, *, tq, tkb):
    qi = pl.program_id(2)
    lane = lax.broadcasted_iota(I32, (1, LANES), 1)
    row = lax.broadcasted_iota(I32, (tq, tq), 0)
    col = lax.broadcasted_iota(I32, (tq, tq), 1)
    visible = (col >> CHUNK_SHIFT) <= (row >> CHUNK_SHIFT)
    q = q_ref[...]
    _softmax_init(acc_ref, l_ref, m_ref)

    def step(start, tk, diagonal):
        kb = k_ref[pl.ds(start, tk), :]
        vb = v_ref[pl.ds(start, tk), :]
        for h in range(2):
            s = _dot_t(q[:, h * LANES:(h + 1) * LANES], kb[:, h * LANES:(h + 1) * LANES])
            if diagonal:
                s = jnp.where(visible, s, NEG_BIG)
            _softmax_step(s, vb, h, acc_ref, l_ref, m_ref)

    n_wide = (qi * tq) // tkb

    def wide(j, c):
        step(pl.multiple_of(j * tkb, tkb), tkb, False)
        return c

    def narrow(j, c):
        step(pl.multiple_of(j * tq, tq), tq, False)
        return c

    lax.fori_loop(0, n_wide, wide, 0)
    lax.fori_loop(n_wide * (tkb // tq), qi, narrow, 0)
    step(pl.multiple_of(qi * tq, tq), tq, True)
    o_ref[...] = jnp.where(lane < MLA_V, _softmax_out(0, acc_ref, l_ref),
                           _softmax_out(1, acc_ref, l_ref)).astype(BF16)


def _mla_attention(qm, km, vm, batch, seq):
    t = qm.shape[0]
    tq = min(512, seq)
    tkb = min(1024, seq)
    nq = seq // tq
    pairs = MLA_HEADS // 2
    stat = pltpu.VMEM((2, tq, LANES), F32)
    return pl.pallas_call(
        functools.partial(_mla_kernel, tq=tq, tkb=tkb),
        grid=(batch, pairs, nq),
        in_specs=[
            pl.BlockSpec((tq, 2 * LANES), lambda b, p, i: (b * nq + i, p)),
            pl.BlockSpec((seq, 2 * LANES), lambda b, p, i: (b, p)),
            pl.BlockSpec((seq, LANES), lambda b, p, i: (b, p)),
        ],
        out_specs=pl.BlockSpec((tq, LANES), lambda b, p, i: (b * nq + i, p)),
        out_shape=jax.ShapeDtypeStruct((t, N_VM), BF16),
        scratch_shapes=[stat, stat, stat],
        compiler_params=_cparams(("arbitrary", "arbitrary", "arbitrary")),
        name="mla_attention",
    )(qm, km, vm)


SEARCH_BISECT_EVERY = 3
SEARCH_MAX_STEPS = 3 * 32 + 4


def _dsa_kernel(qd_ref, qx_ref, w_ref, kx_ref, kd_ref, vd_ref, o_ref,
                key_ref, qs_ref, wr_ref, top_ref, cand_ref, cnt_ref, acc_ref, l_ref, m_ref, *, tq, tk, n_select):
    it = pl.program_id(1)
    t0 = it * tq
    last = t0 // tk
    reps = tk // LANES
    lane = lax.broadcasted_iota(I32, (1, LANES), 1)
    lane2 = lax.broadcasted_iota(I32, (1, 2 * LANES), 1)
    rowid = t0 + lax.broadcasted_iota(I32, (tq, 1), 0)
    row_chunk = rowid >> CHUNK_SHIFT

    qx = qx_ref[...]
    w = w_ref[...]
    for h in range(IDX_HEADS):
        head = (lane2 >= h * IDX_DIM) & (lane2 < (h + 1) * IDX_DIM)
        qs_ref[h * tq:(h + 1) * tq, :] = jnp.where(head, qx, jnp.zeros_like(qx))
        wr_ref[h] = jnp.broadcast_to(w[:, h:h + 1], (tq, LANES))
    top_ref[...] = jnp.full_like(top_ref, INT_MIN)

    def score_block(j, diagonal):
        start = pl.multiple_of(j * tk, tk)
        d = _dot_t(qs_ref[...], kx_ref[pl.ds(start, tk), :])
        score = jnp.zeros((tq, tk), F32)
        for h in range(IDX_HEADS):
            wh = jnp.concatenate([wr_ref[h]] * reps, axis=1)
            score = score + wh * jnp.maximum(d[h * tq:(h + 1) * tq], 0.0)
        score = jnp.where(score == 0.0, 0.0, score)
        bits = pltpu.bitcast(score, I32)
        key = jnp.where(bits < 0, bits ^ jnp.int32(0x7FFFFFFF), bits)
        if diagonal:
            col_chunk = (start + lax.broadcasted_iota(I32, (1, tk), 1)) >> CHUNK_SHIFT
            key = jnp.where(col_chunk <= row_chunk, key, INT_MIN)
        key_ref[j] = key
        t1, t2 = top_ref[0], top_ref[1]
        for c in range(reps):
            x = key[:, c * LANES:(c + 1) * LANES]
            t2 = jnp.maximum(t2, jnp.minimum(t1, x))
            t1 = jnp.maximum(t1, x)
        top_ref[0] = t1
        top_ref[1] = t2

    def score_body(j, c):
        score_block(j, False)
        return c

    lax.fori_loop(0, last, score_body, 0)
    score_block(last, True)
    n_blocks = last + 1

    n_chunks = tq // LANES

    def to_lanes(rep):
        return jnp.concatenate([jnp.transpose(rep[c * LANES:(c + 1) * LANES, :])[0:1, :]
                                for c in range(n_chunks)], axis=1)

    def to_rows(row):
        return jnp.concatenate([jnp.transpose(jnp.broadcast_to(row[:, c * LANES:(c + 1) * LANES], (LANES, LANES)))
                                for c in range(n_chunks)], axis=0)

    def sweep(combine, start, finish):
        cnt_ref[...] = jnp.full_like(cnt_ref, start)

        def body(j, c):
            for half in range(n_chunks):
                rows = slice(half * LANES, (half + 1) * LANES)
                cb = jnp.concatenate([cand_ref[rows, :]] * reps, axis=1)
                cnt_ref[rows, :] = combine(cnt_ref[rows, :], key_ref[j, rows, :], cb)
            return c

        lax.fori_loop(0, n_blocks, body, 0)
        parts = cnt_ref[...]
        return jnp.concatenate([finish(jnp.transpose(parts[c * LANES:(c + 1) * LANES, :]))
                                for c in range(n_chunks)], axis=1)

    def add_ge(acc, keys, cb):
        ge = jnp.where(keys >= cb, 1, 0)
        part = ge[:, 0:LANES]
        for cc in range(1, reps):
            part = part + ge[:, cc * LANES:(cc + 1) * LANES]
        return acc + part

    def min_ge(acc, keys, cb):
        kept = jnp.where(keys >= cb, keys, jnp.int32(2 ** 31 - 1))
        part = kept[:, 0:LANES]
        for cc in range(1, reps):
            part = jnp.minimum(part, kept[:, cc * LANES:(cc + 1) * LANES])
        return jnp.minimum(acc, part)

    def count_prepared():
        return sweep(add_ge, 0, lambda x: jnp.sum(x, axis=0, keepdims=True))

    def count_ge(cand):
        cand_ref[...] = to_rows(cand)
        return count_prepared()

    def smallest_ge(cand):
        cand_ref[...] = to_rows(cand)
        return sweep(min_ge, 2 ** 31 - 1, lambda x: jnp.min(x, axis=0, keepdims=True))

    def key_value(k):
        return pltpu.bitcast(jnp.where(k < 0, k ^ jnp.int32(0x7FFFFFFF), k), F32)

    def value_key(v):
        bits = pltpu.bitcast(v, I32)
        return jnp.where(bits < 0, bits ^ jnp.int32(0x7FFFFFFF), bits)

    row_l = t0 + lax.broadcasted_iota(I32, (1, tq), 1)
    few = ((row_l >> CHUNK_SHIFT) + 1) * CHUNK <= n_select
    t2_min = jnp.broadcast_to(jnp.min(top_ref[1], axis=-1, keepdims=True), (tq, LANES))
    t1_max = jnp.broadcast_to(jnp.max(top_ref[0], axis=-1, keepdims=True), (tq, LANES))
    lo = to_lanes(t2_min)
    hi = to_lanes(t1_max) + 1
    c_lo = count_ge(lo)
    c_hi = jnp.zeros((1, tq), I32)
    c_pos = count_ge(jnp.full((1, tq), 1, I32))
    c_nn = count_ge(jnp.zeros((1, tq), I32))
    positive = c_pos >= n_select
    negative = c_nn < n_select
    at_zero = jnp.logical_not(jnp.logical_or(positive, negative))
    raise_lo = jnp.logical_and(positive, lo < 1)
    lower_hi = jnp.logical_and(negative, hi > 0)
    lo, c_lo = jnp.where(raise_lo, 1, lo), jnp.where(raise_lo, c_pos, c_lo)
    hi, c_hi = jnp.where(lower_hi, 0, hi), jnp.where(lower_hi, c_nn, c_hi)
    lo, c_lo = jnp.where(at_zero, 0, lo), jnp.where(at_zero, c_nn, c_lo)
    hi, c_hi = jnp.where(at_zero, 1, hi), jnp.where(at_zero, c_pos, c_hi)

    def propose(step, lo, c_lo, hi, c_hi):
        active = jnp.logical_and(jnp.logical_not(few), jnp.logical_and(c_lo > n_select + 1, hi - lo > 1))
        a = jnp.log(c_lo.astype(F32))
        b = jnp.log(c_hi.astype(F32) + 0.5)
        frac = jnp.clip((a - math.log(n_select + 0.5)) / (a - b), 1.0 / 64, 63.0 / 64)
        v_lo, v_hi = key_value(lo), key_value(hi)
        guess = value_key(v_lo + frac * (v_hi - v_lo))
        middle = lo + lax.shift_right_logical(hi - lo, 1)
        cand = jnp.where(step % SEARCH_BISECT_EVERY == SEARCH_BISECT_EVERY - 1, middle, guess)
        cand = jnp.minimum(jnp.maximum(cand, lo + 1), hi - 1)
        cand_ref[...] = to_rows(cand)
        return cand, jnp.where(active, 1, 0), jnp.max(jnp.where(active, 1, 0))

    def search_cond(carry):
        return jnp.logical_and(carry[0] < SEARCH_MAX_STEPS, carry[1] > 0)

    def search_body(carry):
        step, _, cand, active, lo, c_lo, hi, c_hi = carry
        c = count_prepared()
        up = jnp.logical_and(active > 0, c >= n_select)
        down = jnp.logical_and(active > 0, c < n_select)
        lo, c_lo = jnp.where(up, cand, lo), jnp.where(up, c, c_lo)
        hi, c_hi = jnp.where(down, cand, hi), jnp.where(down, c, c_hi)
        cand, active, busy = propose(step + 1, lo, c_lo, hi, c_hi)
        return step + 1, busy, cand, active, lo, c_lo, hi, c_hi

    cand, active, busy = propose(jnp.int32(0), lo, c_lo, hi, c_hi)
    lo, c_lo, hi, c_hi = lax.while_loop(
        search_cond, search_body, (jnp.int32(0), busy, cand, active, lo, c_lo, hi, c_hi))[4:]

    over = jnp.logical_and(jnp.logical_not(few), jnp.logical_and(c_lo == n_select + 1, hi - lo > 1))
    least = smallest_ge(lo)
    c_drop = count_ge(jnp.where(over, least + 1, lo))
    dropped = jnp.logical_and(over, c_drop == n_select)
    theta = jnp.where(dropped, least + 1, jnp.where(over, least, lo))
    theta = jnp.where(few, INT_MIN, theta)
    c_above = jnp.where(over, c_drop, c_hi)
    tied = jnp.logical_and(jnp.logical_not(few), jnp.logical_and(c_lo > n_select, jnp.logical_not(dropped)))

    @pl.when(jnp.max(jnp.where(tied, 1, 0)) > 0)
    def _():
        theta_r = to_rows(theta)[:, 0:1]
        tied_r = to_rows(jnp.where(tied, 1, 0))[:, 0:1] > 0
        need = to_rows((n_select - c_above).astype(F32))[:, 0:1]
        r = lax.broadcasted_iota(I32, (tk, tk), 0)
        c = lax.broadcasted_iota(I32, (tk, tk), 1)
        before = jnp.where(r < c, 1.0, 0.0).astype(BF16)

        def body(j, seen):
            key = key_ref[j]
            eq = jnp.logical_and(key == theta_r, tied_r)
            eqf = jnp.where(eq, 1.0, 0.0)
            rank = seen + _dot(eqf.astype(BF16), before)
            key_ref[j] = jnp.where(jnp.logical_and(eq, rank >= need), INT_MIN, key)
            return seen + jnp.sum(eqf, axis=-1, keepdims=True)

        lax.fori_loop(0, n_blocks, body, jnp.zeros((tq, 1), F32))

    cand_ref[...] = to_rows(jnp.maximum(theta, INT_MIN + 1))

    qd = qd_ref[...]
    q_heads = []
    for h in range(DSA_HEADS):
        blk = qd[:, (h // 2) * LANES:(h // 2 + 1) * LANES]
        head = (lane >= (h % 2) * DSA_DIM) & (lane < (h % 2 + 1) * DSA_DIM)
        q_heads.append(jnp.where(head, blk, jnp.zeros_like(blk)))
    _softmax_init(acc_ref, l_ref, m_ref)

    def attend(j, c):
        start = pl.multiple_of(j * tk, tk)
        sel = key_ref[j] >= jnp.concatenate([cand_ref[...]] * reps, axis=1)
        kb = kd_ref[pl.ds(start, tk), :]
        vb = vd_ref[pl.ds(start, tk), :]
        for h in range(DSA_HEADS):
            p0 = (h // 2) * LANES
            s = jnp.where(sel, _dot_t(q_heads[h], kb[:, p0:p0 + LANES]), NEG_BIG)
            _softmax_step(s, vb[:, p0:p0 + LANES], h, acc_ref, l_ref, m_ref)
        return c

    lax.fori_loop(0, n_blocks, attend, 0)
    for p in range(DSA_HEADS // 2):
        o_ref[:, p * LANES:(p + 1) * LANES] = jnp.where(
            lane < DSA_DIM, _softmax_out(2 * p, acc_ref, l_ref), _softmax_out(2 * p + 1, acc_ref, l_ref)).astype(BF16)


def _dsa_attention(a, rp, w8, batch, seq, n_select):
    t = a.shape[0]
    tq = 256
    tk = min(1024, seq)
    nq = seq // tq
    width = DSA_HEADS * DSA_DIM
    keys = lambda col: pl.BlockSpec((seq, width), lambda b, i: (b, col), pipeline_mode=pl.Buffered(1))
    stat = lambda n: pltpu.VMEM((n, tq, LANES), F32)
    return pl.pallas_call(
        functools.partial(_dsa_kernel, tq=tq, tk=tk, n_select=n_select),
        grid=(batch, nq),
        in_specs=[
            pl.BlockSpec((tq, width), lambda b, i: (b * nq + i, 0)),
            pl.BlockSpec((tq, width), lambda b, i: (b * nq + i, 2)),
            pl.BlockSpec((tq, N_W), lambda b, i: (b * nq + i, 0)),
            keys(3),
            keys(1),
            keys(0),
        ],
        out_specs=pl.BlockSpec((tq, width), lambda b, i: (b * nq + i, 0)),
        out_shape=jax.ShapeDtypeStruct((t, width), BF16),
        scratch_shapes=[
            pltpu.VMEM((seq // tk, tq, tk), I32),
            pltpu.VMEM((IDX_HEADS * tq, width), BF16),
            stat(IDX_HEADS),
            pltpu.VMEM((2, tq, LANES), I32),
            pltpu.VMEM((tq, LANES), I32),
            pltpu.VMEM((tq, LANES), I32),
            stat(DSA_HEADS), stat(DSA_HEADS), stat(DSA_HEADS),
        ],
        compiler_params=_cparams(("arbitrary", "arbitrary")),
        name="dsa_attention",
    )(rp, rp, w8, rp, rp, a)


def _merge_kernel(x_ref, osb_ref, omla_ref, odsa_ref, g_ref, gm_ref, wsb_ref, wmla_ref, wdsa_ref, wout_ref,
                  o_ref):
    d = D_MODEL
    merged = (g_ref[:, 0:d].astype(F32) * _dot(osb_ref[...], wsb_ref[...])
              + g_ref[:, d:2 * d].astype(F32) * _dot(omla_ref[...], wmla_ref[...])
              + g_ref[:, 2 * d:3 * d].astype(F32) * _dot(odsa_ref[...], wdsa_ref[...]))
    o_ref[...] = x_ref[...] + gm_ref[...] * _dot(merged.astype(BF16), wout_ref[...])


def _merge(x, osb, omla, odsa, gates, gm, wsb, wmla, wdsa, wout, layer, seq):
    t, d = x.shape
    tm = 512
    per = seq // tm
    row = lambda i: (i, 0)
    wspec = lambda k: pl.BlockSpec((None, k, d), lambda i: (layer, 0, 0))
    return pl.pallas_call(
        _merge_kernel,
        grid=(t // tm,),
        in_specs=[
            pl.BlockSpec((tm, d), row),
            pl.BlockSpec((tm, osb.shape[1]), row),
            pl.BlockSpec((tm, omla.shape[1]), row),
            pl.BlockSpec((tm, odsa.shape[1]), row),
            pl.BlockSpec((tm, N_G), row),
            pl.BlockSpec((None, None, 1, d), lambda i: (layer, i // per, 0, 0)),
            wspec(osb.shape[1]), wspec(omla.shape[1]), wspec(odsa.shape[1]), wspec(d),
        ],
        out_specs=pl.BlockSpec((tm, d), row),
        out_shape=jax.ShapeDtypeStruct((t, d), F32),
        compiler_params=_cparams(("arbitrary",)),
        name="merge",
    )(x, osb, omla, odsa, gates, gm, wsb, wmla, wdsa, wout)


def _router_kernel(x_ref, sc_ref, sh_ref, rw_ref, rb_ref, h_ref, ti_ref, tw_ref, tr_ref, cnt_ref, run_ref, *, tm):
    @pl.when(pl.program_id(0) == 0)
    def _():
        run_ref[...] = jnp.zeros_like(run_ref)

    h = _rms(x_ref[...]) * sc_ref[...] + sh_ref[...]
    h_ref[...] = h
    logits = jnp.dot(h, rw_ref[...], preferred_element_type=F32, precision=lax.Precision.HIGHEST) + rb_ref[...]
    lane = lax.broadcasted_iota(I32, (tm, LANES), 1)
    work = logits
    vals, hots = [], []
    for _ in range(TOP_K):
        m = jnp.max(work, axis=-1, keepdims=True)
        first = jnp.min(jnp.where(work == m, lane, LANES), axis=-1, keepdims=True)
        hot = lane == first
        vals.append(m)
        hots.append(hot)
        work = jnp.where(hot, -jnp.inf, work)
    exps = [jnp.exp(v - vals[0]) for v in vals]
    denom = exps[0] + exps[1] + exps[2] + exps[3]
    chosen = jnp.zeros((tm, LANES), F32)
    for hot in hots:
        chosen = chosen + jnp.where(hot, 1.0, 0.0)
    r = lax.broadcasted_iota(I32, (tm, tm), 0)
    c = lax.broadcasted_iota(I32, (tm, tm), 1)
    earlier = jnp.where(c < r, 1.0, 0.0).astype(BF16)
    rank_all = _dot(earlier, chosen.astype(BF16)) + run_ref[...]
    ti = jnp.zeros((tm, LANES), I32)
    tw = jnp.zeros((tm, LANES), F32)
    tr = jnp.zeros((tm, LANES), I32)
    for k in range(TOP_K):
        e_k = jnp.sum(jnp.where(hots[k], lane, 0), axis=-1, keepdims=True)
        r_k = jnp.sum(jnp.where(hots[k], rank_all, 0.0), axis=-1, keepdims=True).astype(I32)
        ti = jnp.where(lane == k, e_k, ti)
        tw = jnp.where(lane == k, exps[k] / denom, tw)
        tr = jnp.where(lane == k, r_k, tr)
    ti_ref[...] = ti
    tw_ref[...] = tw
    tr_ref[...] = tr
    run_ref[...] = run_ref[...] + jnp.sum(chosen, axis=0, keepdims=True)
    cnt_ref[...] = run_ref[...]


def _router(x, scale, shift, rw, rb, layer, seq):
    t, d = x.shape
    tm = 256
    per = seq // tm
    row = lambda i: (i, 0)
    return pl.pallas_call(
        functools.partial(_router_kernel, tm=tm),
        grid=(t // tm,),
        in_specs=[
            pl.BlockSpec((tm, d), row),
            pl.BlockSpec((None, None, 1, d), lambda i: (layer, i // per, 0, 0)),
            pl.BlockSpec((None, None, 1, d), lambda i: (layer, i // per, 0, 0)),
            pl.BlockSpec((None, d, LANES), lambda i: (layer, 0, 0)),
            pl.BlockSpec((None, 1, LANES), lambda i: (layer, 0, 0)),
        ],
        out_specs=[
            pl.BlockSpec((tm, d), row),
            pl.BlockSpec((tm, LANES), row),
            pl.BlockSpec((tm, LANES), row),
            pl.BlockSpec((tm, LANES), row),
            pl.BlockSpec((1, LANES), lambda i: (0, 0)),
        ],
        out_shape=[
            jax.ShapeDtypeStruct((t, d), F32),
            jax.ShapeDtypeStruct((t, LANES), I32),
            jax.ShapeDtypeStruct((t, LANES), F32),
            jax.ShapeDtypeStruct((t, LANES), I32),
            jax.ShapeDtypeStruct((1, LANES), F32),
        ],
        scratch_shapes=[pltpu.VMEM((1, LANES), F32)],
        compiler_params=_cparams(("arbitrary",)),
        name="router",
    )(x, scale, shift, rw, rb)


def _expert_kernel(te_ref, tf_ref, nv_ref, x_ref, wgu_ref, bgu_ref, wd_ref, bd_ref, o_ref,
                   wgu_bf, wd_bf):
    i = pl.program_id(0)

    @pl.when(i >= nv_ref[0])
    def _():
        o_ref[...] = jnp.zeros_like(o_ref)

    @pl.when(i < nv_ref[0])
    def _():
        @pl.when(tf_ref[i] == 1)
        def _():
            wgu_bf[...] = wgu_ref[...].astype(BF16)
            wd_bf[...] = wd_ref[...].astype(BF16)

        gu = _dot(x_ref[...].astype(BF16), wgu_bf[...]) + bgu_ref[...]
        gate = jnp.minimum(gu[:, :D_EXPERT], SWIGLU_LIMIT)
        up = jnp.clip(gu[:, D_EXPERT:], -SWIGLU_LIMIT, SWIGLU_LIMIT)
        act = (up + 1.0) * (gate * (1.0 / (1.0 + jnp.exp(-SWIGLU_ALPHA * gate))))
        o_ref[...] = _dot(act.astype(BF16), wd_bf[...]) + bd_ref[...]


def _experts(xs, tile_e, tile_first, n_valid, wgu, bgu, wd, bd, layer, tm):
    p, d = xs.shape
    n_tiles = p // tm
    grid_spec = pltpu.PrefetchScalarGridSpec(
        num_scalar_prefetch=3,
        grid=(n_tiles,),
        in_specs=[
            pl.BlockSpec((tm, d), lambda i, te, tf, nv: (i, 0)),
            pl.BlockSpec((None, None, d, 2 * D_EXPERT), lambda i, te, tf, nv: (layer, te[i], 0, 0)),
            pl.BlockSpec((None, None, 1, 2 * D_EXPERT), lambda i, te, tf, nv: (layer, te[i], 0, 0)),
            pl.BlockSpec((None, None, D_EXPERT, d), lambda i, te, tf, nv: (layer, te[i], 0, 0)),
            pl.BlockSpec((None, None, 1, d), lambda i, te, tf, nv: (layer, te[i], 0, 0)),
        ],
        out_specs=pl.BlockSpec((tm, d), lambda i, te, tf, nv: (i, 0)),
        scratch_shapes=[pltpu.VMEM((d, 2 * D_EXPERT), BF16), pltpu.VMEM((D_EXPERT, d), BF16)],
    )
    return pl.pallas_call(
        _expert_kernel,
        grid_spec=grid_spec,
        out_shape=jax.ShapeDtypeStruct((p, d), F32),
        compiler_params=_cparams(("arbitrary",)),
        name="experts",
    )(tile_e, tile_first, n_valid, xs, wgu, bgu, wd, bd)


def _combine_kernel(x_ref, y0_ref, y1_ref, y2_ref, y3_ref, tw_ref, g_ref, fg_ref, o_ref, *, final):
    tw = tw_ref[...]
    y = tw[:, 0:1] * y0_ref[...]
    for k, y_ref in enumerate((y1_ref, y2_ref, y3_ref), start=1):
        y = y + tw[:, k:k + 1] * y_ref[...]
    x = x_ref[...] + g_ref[...] * y
    if final:
        x = _rms(x) * fg_ref[...]
    o_ref[...] = x


def _combine(x, yg, tw, gf, final_g, layer, seq, final):
    t, d = x.shape
    tm = 256
    nt = t // tm
    per = seq // tm
    slot = lambda k: pl.BlockSpec((tm, d), lambda i: (k * nt + i, 0))
    return pl.pallas_call(
        functools.partial(_combine_kernel, final=final),
        grid=(nt,),
        in_specs=[
            pl.BlockSpec((tm, d), lambda i: (i, 0)),
            slot(0), slot(1), slot(2), slot(3),
            pl.BlockSpec((tm, LANES), lambda i: (i, 0)),
            pl.BlockSpec((None, None, 1, d), lambda i: (layer, i // per, 0, 0)),
            pl.BlockSpec((1, d), lambda i: (0, 0)),
        ],
        out_specs=pl.BlockSpec((tm, d), lambda i: (i, 0)),
        out_shape=jax.ShapeDtypeStruct((t, d), F32),
        compiler_params=_cparams(("arbitrary",)),
        name="combine",
    )(x, yg, yg, yg, yg, tw, gf, final_g)


def _rope_tables(positions):
    pos = positions.reshape(-1).astype(F32)

    def cs(dim):
        inv_freq = ROPE_THETA ** (-jnp.arange(0, dim, 2, dtype=F32) / dim)
        ang = pos[:, None] * inv_freq
        return jnp.cos(ang), jnp.sin(ang)

    c_d, s_d = cs(DSA_DIM)
    c_i, s_i = cs(IDX_DIM)
    c_m, s_m = cs(MLA_ROPE)
    t = pos.shape[0]
    ones, zeros = jnp.ones((t, 64), F32), jnp.zeros((t, 64), F32)
    cos_t = jnp.concatenate([jnp.tile(c_d, (1, 4)), jnp.tile(c_i, (1, 8)),
                             ones, c_m, c_m, ones[:, :32]], axis=1)
    sin_t = jnp.concatenate([jnp.tile(s_d, (1, 4)), jnp.tile(s_i, (1, 8)),
                             zeros, s_m, s_m, zeros[:, :32]], axis=1)
    return cos_t, sin_t


def _permute_cols(w, idx, sgn):
    return (w.at[..., jnp.asarray(idx)].get(mode="promise_in_bounds") * jnp.asarray(sgn)).astype(BF16)


def _dispatch_tables(ti, tr, counts, tm):
    t = ti.shape[0]
    p = t * TOP_K + N_EXPERTS * tm
    n_tiles = p // tm
    cnt = counts.astype(I32)
    padded = ((cnt + tm - 1) // tm) * tm
    ends = jnp.cumsum(padded)
    starts = ends - padded
    pos = starts[ti] + tr
    tok = jnp.repeat(jnp.arange(t, dtype=I32), TOP_K)
    row_src = jnp.zeros((p,), I32).at[pos.reshape(-1)].set(tok, unique_indices=True, mode="promise_in_bounds")
    tile_start = jnp.arange(n_tiles, dtype=I32) * tm
    n_valid = ends[-1] // tm
    tile_e = jnp.sum((ends[None, :] <= tile_start[:, None]).astype(I32), axis=1)
    tile_e = jnp.minimum(tile_e, N_EXPERTS - 1)
    last_e = jnp.max(jnp.where(tile_start < ends[-1], tile_e, 0))
    tile_e = jnp.where(tile_start < ends[-1], tile_e, last_e)
    tile_first = jnp.concatenate([jnp.ones((1,), I32), (tile_e[1:] != tile_e[:-1]).astype(I32)])
    return pos, row_src, tile_e, tile_first, n_valid.astype(I32).reshape(1)


def kernel(x, c, positions, ada_w, ada_b, norm_mix_g, w_in, mla_q_norm_g, mla_kv_norm_g, mla_w_uq, mla_w_ukv,
           w_sb_out, w_mla_out, w_dsa_out, w_out, norm_ffn_g, router_w, router_b, expert_w_gu, expert_b_gu,
           expert_w_down, expert_b_down, final_norm_g):
    batch, seq, d = x.shape
    depth = ada_w.shape[0]
    t = batch * seq
    n_select = min(DSA_TOPK_MAX, seq // 4)
    tm_e = 256

    mod = _ada_mod(c, ada_w, ada_b)
    sh_m, sc_m, g_m, sh_f, sc_f, g_f = [m[:, :, None, :] for m in jnp.split(mod, 6, axis=-1)]
    scale_m = norm_mix_g[:, None, None, :] * (1.0 + sc_m)
    scale_f = norm_ffn_g[:, None, None, :] * (1.0 + sc_f)

    cos_t, sin_t = _rope_tables(positions)
    w_in_p = _permute_cols(w_in, _IN_IDX, _IN_SGN)
    w_uq_p = _permute_cols(mla_w_uq, _UQ_IDX, _UQ_SGN)
    w_ukv_p = _permute_cols(mla_w_ukv, _UKV_IDX, _UKV_SGN)
    w_sb_b, w_mla_b, w_dsa_b, w_out_b = [w.astype(BF16) for w in (w_sb_out, w_mla_out, w_dsa_out, w_out)]
    gq = mla_q_norm_g[:, None, :]
    gkv = mla_kv_norm_g[:, None, :]
    rw_p = jnp.zeros((depth, d, LANES), F32).at[:, :, :N_EXPERTS].set(router_w)
    rb_p = jnp.full((depth, 1, LANES), -jnp.inf, F32).at[:, 0, :N_EXPERTS].set(router_b)
    bgu = expert_b_gu[:, :, None, :]
    bd = expert_b_down[:, :, None, :]
    fg = final_norm_g[None, :]

    xf = x.reshape(t, d)
    for l in range(depth):
        a, cl, rp, w8, gates = _inproj(xf, scale_m, sh_m, w_in_p, cos_t, sin_t, l, seq)
        o_sb = _sb_attention(a, batch, seq)
        qm, km, vm = _mla_up(cl, gq, gkv, w_uq_p, w_ukv_p, cos_t, sin_t, rp, l)
        o_mla = _mla_attention(qm, km, vm, batch, seq)
        o_dsa = _dsa_attention(a, rp, w8, batch, seq, n_select)
        xf = _merge(xf, o_sb, o_mla, o_dsa, gates, g_m, w_sb_b, w_mla_b, w_dsa_b, w_out_b, l, seq)

        hf, ti, tw, tr, counts = _router(xf, scale_f, sh_f, rw_p, rb_p, l, seq)
        pos, row_src, tile_e, tile_first, n_valid = _dispatch_tables(
            ti[:, :TOP_K], tr[:, :TOP_K], counts[0, :N_EXPERTS], tm_e)
        xs = hf.at[row_src].get(mode="promise_in_bounds")
        ys = _experts(xs, tile_e, tile_first, n_valid, expert_w_gu, bgu, expert_w_down, bd, l, tm_e)
        yg = ys.at[pos.T.reshape(-1)].get(mode="promise_in_bounds")
        xf = _combine(xf, yg, tw, g_f, fg, l, seq, l == depth - 1)
    return xf.reshape(batch, seq, d)
```

```python
import functools
import math

import numpy as np
import jax
import jax.numpy as jnp
from jax import lax
from jax.experimental import pallas as pl
from jax.experimental.pallas import tpu as pltpu

F32 = jnp.float32
BF16 = jnp.bfloat16
I32 = jnp.int32

D_MODEL = 1024
CHUNK = 64
CHUNK_SHIFT = 6
ROPE_THETA = 10000.0
NORM_EPS = 1e-6
SB_HEADS, SB_DIM = 6, 64
MLA_HEADS, MLA_NOPE, MLA_ROPE, MLA_V = 6, 64, 32, 64
MLA_Q_RANK, MLA_KV_RANK = 256, 128
DSA_HEADS, DSA_DIM = 4, 64
IDX_HEADS, IDX_DIM = 8, 32
DSA_TOPK_MAX = 256
N_EXPERTS, TOP_K = 32, 4
D_EXPERT = D_MODEL
SWIGLU_LIMIT = 7.0
SWIGLU_ALPHA = 1.702

LANES = 128
VMEM_LIMIT = 56 * 1024 * 1024

INT_MIN = -(2 ** 31)
NEG_BIG = -1e30
SB_UNDERFLOW = 104.0
LOG2E = math.log2(math.e)

_SPLIT = (384, 384, 384, 256, 128, 32, 256, 256, 256, 256, 32, 8, 3072)
_OFF = np.concatenate([[0], np.cumsum(_SPLIT)]).astype(np.int64)
(O_SBQ, O_SBK, O_SBV, O_CQ, O_CKV, O_KR, O_DQ, O_DK, O_DV, O_IQ, O_IK, O_IW, O_GATE, D_IN) = [
    int(v) for v in _OFF]

N_A = 1408
N_C = 384
N_R = 1152
N_W = 128
N_G = 3072
N_IN = N_A + N_C + 2 * N_R + N_W + N_G
_ROPE_SPANS = ((0, 2, 0, DSA_DIM ** -0.5 * LOG2E), (256, 2, 0, 1.0), (512, 2, 1, 1.0), (768, 2, 1, 1.0),
               (1024, 1, 2, 1.0))


def _rot_cols(base, n_heads, d):
    half = d // 2
    idx, sgn = [], []
    for h in range(n_heads):
        for j in range(d):
            if j < half:
                idx.append(base + h * d + j + half)
                sgn.append(-1.0)
            else:
                idx.append(base + h * d + j - half)
                sgn.append(1.0)
    return idx, sgn


def _in_layout():
    idx, sgn = [], []

    def plain(base, n):
        idx.extend(range(base, base + n))
        sgn.extend([1.0] * n)

    def pad(n):
        idx.extend([0] * n)
        sgn.extend([0.0] * n)

    plain(O_DV, 256); plain(O_SBQ, 384); plain(O_SBK, 384); plain(O_SBV, 384)
    plain(O_CQ, 256); plain(O_CKV, 128)
    plain(O_DQ, 256); plain(O_DK, 256); plain(O_IQ, 256)
    for _ in range(IDX_HEADS):
        plain(O_IK, IDX_DIM)
    pad(64); plain(O_KR, 32); pad(32)
    for base, nh, d in ((O_DQ, DSA_HEADS, DSA_DIM), (O_DK, DSA_HEADS, DSA_DIM), (O_IQ, IDX_HEADS, IDX_DIM)):
        i, s = _rot_cols(base, nh, d)
        idx.extend(i); sgn.extend(s)
    i, s = _rot_cols(O_IK, 1, IDX_DIM)
    for _ in range(IDX_HEADS):
        idx.extend(i); sgn.extend(s)
    pad(64)
    i, s = _rot_cols(O_KR, 1, MLA_ROPE)
    idx.extend(i); sgn.extend(s)
    pad(32)
    plain(O_IW, IDX_HEADS); pad(N_W - IDX_HEADS)
    plain(O_GATE, N_G)
    assert len(idx) == N_IN
    return np.asarray(idx, np.int32), np.asarray(sgn, np.float32)


_IN_IDX, _IN_SGN = _in_layout()


def _uq_layout():
    per = MLA_NOPE + MLA_ROPE
    idx, sgn = [], []
    for h in range(MLA_HEADS):
        idx.extend(range(h * per, h * per + per)); sgn.extend([1.0] * per)
        idx.extend([0] * 32); sgn.extend([0.0] * 32)
    for h in range(MLA_HEADS):
        idx.extend([0] * MLA_NOPE); sgn.extend([0.0] * MLA_NOPE)
        i, s = _rot_cols(h * per + MLA_NOPE, 1, MLA_ROPE)
        idx.extend(i); sgn.extend(s)
        idx.extend([0] * 32); sgn.extend([0.0] * 32)
    return np.asarray(idx, np.int32), np.asarray(sgn, np.float32)


def _ukv_layout():
    per = MLA_NOPE + MLA_V
    idx, sgn = [], []
    for h in range(MLA_HEADS):
        idx.extend(range(h * per, h * per + MLA_NOPE)); sgn.extend([1.0] * MLA_NOPE)
        idx.extend([0] * 64); sgn.extend([0.0] * 64)
    for h in range(MLA_HEADS):
        idx.extend(range(h * per + MLA_NOPE, h * per + per)); sgn.extend([1.0] * MLA_V)
    return np.asarray(idx, np.int32), np.asarray(sgn, np.float32)


_UQ_IDX, _UQ_SGN = _uq_layout()
_UKV_IDX, _UKV_SGN = _ukv_layout()
N_QM = MLA_HEADS * LANES
N_VM = MLA_HEADS * MLA_V


def _cparams(sem):
    return pltpu.CompilerParams(dimension_semantics=sem, vmem_limit_bytes=VMEM_LIMIT)


def _dot(a, b):
    return jnp.dot(a, b, preferred_element_type=F32)


def _dot_t(a, b):
    return lax.dot_general(a, b, (((1,), (1,)), ((), ())), preferred_element_type=F32)


def _rms(x):
    return x * lax.rsqrt(jnp.mean(x * x, axis=-1, keepdims=True) + NORM_EPS)


def _ada_kernel(c_ref, w_ref, b_ref, o_ref):
    c = c_ref[...]
    sc = c * (1.0 / (1.0 + jnp.exp(-c)))
    o_ref[...] = jnp.dot(sc, w_ref[...], preferred_element_type=F32,
                         precision=lax.Precision.HIGHEST) + b_ref[...]


def _ada_mod(c, ada_w, ada_b):
    depth, d, n = ada_w.shape
    b = c.shape[0]
    rows = 8
    cp = jnp.zeros((rows, d), F32).at[:b].set(c)
    tn = 2048
    out = pl.pallas_call(
        _ada_kernel,
        grid=(depth, n // tn),
        in_specs=[
            pl.BlockSpec((rows, d), lambda l, j: (0, 0)),
            pl.BlockSpec((None, d, tn), lambda l, j: (l, 0, j)),
            pl.BlockSpec((None, 1, tn), lambda l, j: (l, 0, j)),
        ],
        out_specs=pl.BlockSpec((None, rows, tn), lambda l, j: (l, 0, j)),
        out_shape=jax.ShapeDtypeStruct((depth, rows, n), F32),
        compiler_params=_cparams(("arbitrary", "arbitrary")),
        name="ada_mod",
    )(cp, ada_w, ada_b.reshape(depth, 1, n))
    return out[:, :b]


def _inproj_kernel(x_ref, sc_ref, sh_ref, w_ref, cos_ref, sin_ref,
                   a_ref, c_ref, r_ref, w8_ref, g_ref):
    h = (_rms(x_ref[...]) * sc_ref[...] + sh_ref[...]).astype(BF16)
    o = 0
    for c0 in range(0, N_A, 256):
        c1 = min(c0 + 256, N_A)
        a_ref[:, c0:c1] = _dot(h, w_ref[:, o + c0:o + c1]).astype(BF16)
    o += N_A
    c_ref[...] = _dot(h, w_ref[:, o:o + N_C])
    o += N_C
    for lo, n, kind, scale in _ROPE_SPANS:
        y = _dot(h, w_ref[:, o + lo:o + lo + n * LANES])
        yr = _dot(h, w_ref[:, o + N_R + lo:o + N_R + lo + n * LANES])
        cs = jnp.concatenate([cos_ref[:, kind * LANES:(kind + 1) * LANES]] * n, axis=1)
        sn = jnp.concatenate([sin_ref[:, kind * LANES:(kind + 1) * LANES]] * n, axis=1)
        r = y * cs + yr * sn
        if scale != 1.0:
            r = r * scale
        r_ref[:, lo:lo + n * LANES] = r.astype(BF16)
    o += 2 * N_R
    w8_ref[...] = _dot(h, w_ref[:, o:o + N_W]) * (IDX_DIM ** -0.5 * IDX_HEADS ** -0.5)
    o += N_W
    for c0 in range(0, N_G, 512):
        z = _dot(h, w_ref[:, o + c0:o + c0 + 512])
        g_ref[:, c0:c0 + 512] = (1.0 / (1.0 + jnp.exp(-z))).astype(BF16)


def _inproj(x, scale, shift, w, cos_t, sin_t, layer, seq):
    t, d = x.shape
    tm = 512
    per = seq // tm
    return pl.pallas_call(
        _inproj_kernel,
        grid=(t // tm,),
        in_specs=[
            pl.BlockSpec((tm, d), lambda i: (i, 0)),
            pl.BlockSpec((None, None, 1, d), lambda i: (layer, i // per, 0, 0)),
            pl.BlockSpec((None, None, 1, d), lambda i: (layer, i // per, 0, 0)),
            pl.BlockSpec((None, d, N_IN), lambda i: (layer, 0, 0), pipeline_mode=pl.Buffered(1)),
            pl.BlockSpec((tm, 3 * LANES), lambda i: (i, 0)),
            pl.BlockSpec((tm, 3 * LANES), lambda i: (i, 0)),
        ],
        out_specs=[
            pl.BlockSpec((tm, N_A), lambda i: (i, 0)),
            pl.BlockSpec((tm, N_C), lambda i: (i, 0)),
            pl.BlockSpec((tm, N_R), lambda i: (i, 0)),
            pl.BlockSpec((tm, N_W), lambda i: (i, 0)),
            pl.BlockSpec((tm, N_G), lambda i: (i, 0)),
        ],
        out_shape=[
            jax.ShapeDtypeStruct((t, N_A), BF16),
            jax.ShapeDtypeStruct((t, N_C), F32),
            jax.ShapeDtypeStruct((t, N_R), BF16),
            jax.ShapeDtypeStruct((t, N_W), F32),
            jax.ShapeDtypeStruct((t, N_G), BF16),
        ],
        compiler_params=_cparams(("arbitrary",)),
        name="inproj",
    )(x, scale, shift, w, cos_t, sin_t)


def _mla_up_kernel(c_ref, gq_ref, gkv_ref, wq_ref, wkv_ref, cos_ref, sin_ref, kr_ref,
                   q_ref, k_ref, v_ref):
    c = c_ref[...]
    nq = (_rms(c[:, :MLA_Q_RANK]) * gq_ref[...]).astype(BF16)
    nkv = (_rms(c[:, MLA_Q_RANK:]) * gkv_ref[...]).astype(BF16)
    scale = (MLA_NOPE + MLA_ROPE) ** -0.5 * LOG2E
    cs = cos_ref[...] * scale
    sn = sin_ref[...] * scale
    kr = kr_ref[...].astype(F32)
    for h in range(MLA_HEADS):
        lo = h * LANES
        y = _dot(nq, wq_ref[:, lo:lo + LANES])
        yr = _dot(nq, wq_ref[:, N_QM + lo:N_QM + lo + LANES])
        q_ref[:, lo:lo + LANES] = (y * cs + yr * sn).astype(BF16)
        k_ref[:, lo:lo + LANES] = (_dot(nkv, wkv_ref[:, lo:lo + LANES]) + kr).astype(BF16)
    v_ref[...] = _dot(nkv, wkv_ref[:, N_QM:N_QM + N_VM]).astype(BF16)


def _mla_up(cl, gq, gkv, wq, wkv, cos_t, sin_t, rp, layer):
    t = cl.shape[0]
    tm = 512
    return pl.pallas_call(
        _mla_up_kernel,
        grid=(t // tm,),
        in_specs=[
            pl.BlockSpec((tm, N_C), lambda i: (i, 0)),
            pl.BlockSpec((None, 1, MLA_Q_RANK), lambda i: (layer, 0, 0)),
            pl.BlockSpec((None, 1, MLA_KV_RANK), lambda i: (layer, 0, 0)),
            pl.BlockSpec((None, MLA_Q_RANK, 2 * N_QM), lambda i: (layer, 0, 0)),
            pl.BlockSpec((None, MLA_KV_RANK, N_QM + N_VM), lambda i: (layer, 0, 0)),
            pl.BlockSpec((tm, LANES), lambda i: (i, 2)),
            pl.BlockSpec((tm, LANES), lambda i: (i, 2)),
            pl.BlockSpec((tm, LANES), lambda i: (i, 8)),
        ],
        out_specs=[
            pl.BlockSpec((tm, N_QM), lambda i: (i, 0)),
            pl.BlockSpec((tm, N_QM), lambda i: (i, 0)),
            pl.BlockSpec((tm, N_VM), lambda i: (i, 0)),
        ],
        out_shape=[
            jax.ShapeDtypeStruct((t, N_QM), BF16),
            jax.ShapeDtypeStruct((t, N_QM), BF16),
            jax.ShapeDtypeStruct((t, N_VM), BF16),
        ],
        compiler_params=_cparams(("arbitrary",)),
        name="mla_up",
    )(cl, gq, gkv, wq, wkv, cos_t, sin_t, rp)


def _sb_kernel(q_ref, k_ref, v_ref, o_ref, acc_ref, rem_ref, *, tq):
    qi = pl.program_id(2)
    lane = lax.broadcasted_iota(I32, (1, LANES), 1)
    row = lax.broadcasted_iota(I32, (tq, tq), 0)
    col = lax.broadcasted_iota(I32, (tq, tq), 1)
    causal = col < row
    tri = jnp.where(row > col, 1.0, 0.0).astype(BF16)
    q = q_ref[...]
    q_heads = []
    for h in range(2):
        head = (lane >= h * SB_DIM) & (lane < (h + 1) * SB_DIM)
        q_heads.append(jnp.where(head, q, jnp.zeros_like(q)) * jnp.asarray(SB_DIM ** -0.5, BF16))
    acc_ref[...] = jnp.zeros_like(acc_ref)
    rem_ref[...] = jnp.zeros_like(rem_ref)

    def block(j, diagonal):
        start = pl.multiple_of(j * tq, tq)
        kb = k_ref[pl.ds(start, tq), :]
        vb = v_ref[pl.ds(start, tq), :]
        for h in range(2):
            z = _dot_t(q_heads[h], kb)
            soft = jnp.log1p(jnp.exp(-jnp.abs(z)))
            log_stay = -(jnp.maximum(z, 0.0) + soft)
            if diagonal:
                log_stay = jnp.where(causal, log_stay, 0.0)
            hi = log_stay.astype(BF16)
            lo = (log_stay - hi.astype(F32)).astype(BF16)
            later = _dot(hi, tri) + _dot(lo, tri)
            remain = rem_ref[h]
            log_a = (z + log_stay) + later + jnp.concatenate([remain] * (tq // LANES), axis=1)
            a = jnp.exp(log_a)
            if diagonal:
                a = jnp.where(causal, a, 0.0)
            acc_ref[h] += _dot(a.astype(BF16), vb)
            rem_ref[h] = remain + jnp.sum(log_stay, axis=-1, keepdims=True)

    block(qi, True)

    def cond(j):
        return jnp.logical_and(j >= 0, jnp.max(rem_ref[...]) > -SB_UNDERFLOW)

    def body(j):
        block(j, False)
        return j - 1

    lax.while_loop(cond, body, qi - 1)
    o_ref[...] = jnp.where(lane < SB_DIM, acc_ref[0], acc_ref[1]).astype(BF16)


def _sb_attention(a, batch, seq):
    t = a.shape[0]
    tq = 256
    nq = seq // tq
    pairs = SB_HEADS // 2
    stat = pltpu.VMEM((2, tq, LANES), F32)
    return pl.pallas_call(
        functools.partial(_sb_kernel, tq=tq),
        grid=(batch, pairs, nq),
        in_specs=[
            pl.BlockSpec((tq, LANES), lambda b, p, i: (b * nq + i, 2 + p)),
            pl.BlockSpec((seq, LANES), lambda b, p, i: (b, 2 + pairs + p)),
            pl.BlockSpec((seq, LANES), lambda b, p, i: (b, 2 + 2 * pairs + p)),
        ],
        out_specs=pl.BlockSpec((tq, LANES), lambda b, p, i: (b * nq + i, p)),
        out_shape=jax.ShapeDtypeStruct((t, SB_HEADS * SB_DIM), BF16),
        scratch_shapes=[stat, stat],
        compiler_params=_cparams(("arbitrary", "arbitrary", "arbitrary")),
        name="sb_attention",
    )(a, a, a)


def _softmax_step(s, vb, h, acc_ref, l_ref, m_ref):
    reps = s.shape[1] // LANES
    m_old = m_ref[h]
    m_new = jnp.maximum(m_old, jnp.max(s, axis=-1, keepdims=True))
    alpha = jnp.exp2(m_old - m_new)
    p = jnp.exp2(s - jnp.concatenate([m_new] * reps, axis=1))
    part = p[:, 0:LANES]
    for c in range(1, reps):
        part = part + p[:, c * LANES:(c + 1) * LANES]
    l_ref[h] = alpha * l_ref[h] + part
    acc_ref[h] = alpha * acc_ref[h] + _dot(p.astype(BF16), vb)
    m_ref[h] = m_new


def _softmax_init(acc_ref, l_ref, m_ref):
    acc_ref[...] = jnp.zeros_like(acc_ref)
    l_ref[...] = jnp.zeros_like(l_ref)
    m_ref[...] = jnp.full_like(m_ref, NEG_BIG)


def _softmax_out(h, acc_ref, l_ref):
    return acc_ref[h] / jnp.sum(l_ref[h], axis=-1, keepdims=True)


def _mla_kernel(q_ref, k_ref, v_ref, o_ref, acc_ref, l_ref, m_ref, *, tq, widths):
    qi = pl.program_id(2)
    lane = lax.broadcasted_iota(I32, (1, LANES), 1)
    row = lax.broadcasted_iota(I32, (tq, tq), 0)
    col = lax.broadcasted_iota(I32, (tq, tq), 1)
    visible = (col >> CHUNK_SHIFT) <= (row >> CHUNK_SHIFT)
    q = q_ref[...]
    _softmax_init(acc_ref, l_ref, m_ref)

    def step(start, tk, diagonal):
        kb = k_ref[pl.ds(start, tk), :]
        vb = v_ref[pl.ds(start, tk), :]
        for h in range(2):
            s = _dot_t(q[:, h * LANES:(h + 1) * LANES], kb[:, h * LANES:(h + 1) * LANES])
            if diagonal:
                s = jnp.where(visible, s, NEG_BIG)
            _softmax_step(s, vb, h, acc_ref, l_ref, m_ref)

    done = 0
    for width in widths:
        n_steps = (qi * tq - done) // width

        def body(j, c, width=width, done=done):
            step(pl.multiple_of(done + j * width, tq), width, False)
            return c

        lax.fori_loop(0, n_steps, body, 0)
        done = done + n_steps * width
    step(pl.multiple_of(qi * tq, tq), tq, True)
    o_ref[...] = jnp.where(lane < MLA_V, _softmax_out(0, acc_ref, l_ref),
                           _softmax_out(1, acc_ref, l_ref)).astype(BF16)


def _mla_attention(qm, km, vm, batch, seq):
    t = qm.shape[0]
    tq = min(512, seq)
    widths = tuple(w for w in (2048, 1024, 512) if tq <= w <= seq and w % tq == 0)
    nq = seq // tq
    pairs = MLA_HEADS // 2
    stat = pltpu.VMEM((2, tq, LANES), F32)
    return pl.pallas_call(
        functools.partial(_mla_kernel, tq=tq, widths=widths),
        grid=(batch, pairs, nq),
        in_specs=[
            pl.BlockSpec((tq, 2 * LANES), lambda b, p, i: (b * nq + i, p)),
            pl.BlockSpec((seq, 2 * LANES), lambda b, p, i: (b, p)),
            pl.BlockSpec((seq, LANES), lambda b, p, i: (b, p)),
        ],
        out_specs=pl.BlockSpec((tq, LANES), lambda b, p, i: (b * nq + i, p)),
        out_shape=jax.ShapeDtypeStruct((t, N_VM), BF16),
        scratch_shapes=[stat, stat, stat],
        compiler_params=_cparams(("arbitrary", "arbitrary", "arbitrary")),
        name="mla_attention",
    )(qm, km, vm)


SEARCH_BISECT_EVERY = 3
SEARCH_MAX_STEPS = 3 * 32 + 4


def _dsa_kernel(qd_ref, qx_ref, w_ref, kx_ref, kd_ref, vd_ref, o_ref,
                key_ref, qs_ref, wr_ref, top_ref, cand_ref, cnt_ref, cnt3_ref, acc_ref, l_ref, m_ref, *, tq, tk, n_select):
    it = pl.program_id(1)
    t0 = it * tq
    last = t0 // tk
    reps = tk // LANES
    lane = lax.broadcasted_iota(I32, (1, LANES), 1)
    lane2 = lax.broadcasted_iota(I32, (1, 2 * LANES), 1)
    rowid = t0 + lax.broadcasted_iota(I32, (tq, 1), 0)
    row_chunk = rowid >> CHUNK_SHIFT

    qx = qx_ref[...]
    w = w_ref[...]
    for h in range(IDX_HEADS):
        head = (lane2 >= h * IDX_DIM) & (lane2 < (h + 1) * IDX_DIM)
        qs_ref[h * tq:(h + 1) * tq, :] = jnp.where(head, qx, jnp.zeros_like(qx))
        wr_ref[h] = jnp.broadcast_to(w[:, h:h + 1], (tq, LANES))
    top_ref[...] = jnp.full_like(top_ref, INT_MIN)

    def score_block(j, diagonal):
        start = pl.multiple_of(j * tk, tk)
        d = _dot_t(qs_ref[...], kx_ref[pl.ds(start, tk), :])
        score = jnp.zeros((tq, tk), F32)
        for h in range(IDX_HEADS):
            wh = jnp.concatenate([wr_ref[h]] * reps, axis=1)
            score = score + wh * jnp.maximum(d[h * tq:(h + 1) * tq], 0.0)
        score = jnp.where(score == 0.0, 0.0, score)
        bits = pltpu.bitcast(score, I32)
        key = jnp.where(bits < 0, bits ^ jnp.int32(0x7FFFFFFF), bits)
        if diagonal:
            col_chunk = (start + lax.broadcasted_iota(I32, (1, tk), 1)) >> CHUNK_SHIFT
            key = jnp.where(col_chunk <= row_chunk, key, INT_MIN)
        key_ref[j] = key
        t1, t2 = top_ref[0], top_ref[1]
        for c in range(reps):
            x = key[:, c * LANES:(c + 1) * LANES]
            t2 = jnp.maximum(t2, jnp.minimum(t1, x))
            t1 = jnp.maximum(t1, x)
        top_ref[0] = t1
        top_ref[1] = t2

    def score_body(j, c):
        score_block(j, False)
        return c

    lax.fori_loop(0, last, score_body, 0)
    score_block(last, True)
    n_blocks = last + 1

    n_chunks = tq // LANES

    def to_lanes(rep):
        return jnp.concatenate([jnp.transpose(rep[c * LANES:(c + 1) * LANES, :])[0:1, :]
                                for c in range(n_chunks)], axis=1)

    def to_rows(row):
        return jnp.concatenate([jnp.transpose(jnp.broadcast_to(row[:, c * LANES:(c + 1) * LANES], (LANES, LANES)))
                                for c in range(n_chunks)], axis=0)

    def sweep(combine, start, finish):
        cnt_ref[...] = jnp.full_like(cnt_ref, start)

        def body(j, c):
            for half in range(n_chunks):
                rows = slice(half * LANES, (half + 1) * LANES)
                cb = jnp.concatenate([cand_ref[rows, :]] * reps, axis=1)
                cnt_ref[rows, :] = combine(cnt_ref[rows, :], key_ref[j, rows, :], cb)
            return c

        lax.fori_loop(0, n_blocks, body, 0)
        parts = cnt_ref[...]
        return jnp.concatenate([finish(jnp.transpose(parts[c * LANES:(c + 1) * LANES, :]))
                                for c in range(n_chunks)], axis=1)

    def add_ge(acc, keys, cb):
        ge = jnp.where(keys >= cb, 1, 0)
        part = ge[:, 0:LANES]
        for cc in range(1, reps):
            part = part + ge[:, cc * LANES:(cc + 1) * LANES]
        return acc + part

    def min_ge(acc, keys, cb):
        kept = jnp.where(keys >= cb, keys, jnp.int32(2 ** 31 - 1))
        part = kept[:, 0:LANES]
        for cc in range(1, reps):
            part = jnp.minimum(part, kept[:, cc * LANES:(cc + 1) * LANES])
        return jnp.minimum(acc, part)

    def count_prepared():
        return sweep(add_ge, 0, lambda x: jnp.sum(x, axis=0, keepdims=True))

    def count_with_signs(cand):
        cand_ref[...] = to_rows(cand)
        cnt3_ref[...] = jnp.zeros_like(cnt3_ref)

        def body(j, c):
            for half in range(n_chunks):
                rows = slice(half * LANES, (half + 1) * LANES)
                keys = key_ref[j, rows, :]
                cb = jnp.concatenate([cand_ref[rows, :]] * reps, axis=1)
                for slot, bound in enumerate((cb, 1, 0)):
                    cnt3_ref[slot, rows, :] = add_ge(cnt3_ref[slot, rows, :], keys, bound)
            return c

        lax.fori_loop(0, n_blocks, body, 0)
        return [jnp.concatenate([jnp.sum(jnp.transpose(cnt3_ref[slot, c * LANES:(c + 1) * LANES, :]), axis=0,
                                         keepdims=True) for c in range(n_chunks)], axis=1) for slot in range(3)]

    def count_ge(cand):
        cand_ref[...] = to_rows(cand)
        return count_prepared()

    def smallest_ge(cand):
        cand_ref[...] = to_rows(cand)
        return sweep(min_ge, 2 ** 31 - 1, lambda x: jnp.min(x, axis=0, keepdims=True))

    def key_value(k):
        return pltpu.bitcast(jnp.where(k < 0, k ^ jnp.int32(0x7FFFFFFF), k), F32)

    def value_key(v):
        bits = pltpu.bitcast(v, I32)
        return jnp.where(bits < 0, bits ^ jnp.int32(0x7FFFFFFF), bits)

    row_l = t0 + lax.broadcasted_iota(I32, (1, tq), 1)
    few = ((row_l >> CHUNK_SHIFT) + 1) * CHUNK <= n_select
    t2_min = jnp.broadcast_to(jnp.min(top_ref[1], axis=-1, keepdims=True), (tq, LANES))
    t1_max = jnp.broadcast_to(jnp.max(top_ref[0], axis=-1, keepdims=True), (tq, LANES))
    lo = to_lanes(t2_min)
    hi = to_lanes(t1_max) + 1
    c_lo, c_pos, c_nn = count_with_signs(lo)
    c_hi = jnp.zeros((1, tq), I32)
    positive = c_pos >= n_select
    negative = c_nn < n_select
    at_zero = jnp.logical_not(jnp.logical_or(positive, negative))
    raise_lo = jnp.logical_and(positive, lo < 1)
    lower_hi = jnp.logical_and(negative, hi > 0)
    lo, c_lo = jnp.where(raise_lo, 1, lo), jnp.where(raise_lo, c_pos, c_lo)
    hi, c_hi = jnp.where(lower_hi, 0, hi), jnp.where(lower_hi, c_nn, c_hi)
    lo, c_lo = jnp.where(at_zero, 0, lo), jnp.where(at_zero, c_nn, c_lo)
    hi, c_hi = jnp.where(at_zero, 1, hi), jnp.where(at_zero, c_pos, c_hi)

    def propose(step, lo, c_lo, hi, c_hi):
        active = jnp.logical_and(jnp.logical_not(few), jnp.logical_and(c_lo > n_select + 1, hi - lo > 1))
        a = jnp.log(c_lo.astype(F32))
        b = jnp.log(c_hi.astype(F32) + 0.5)
        frac = jnp.clip((a - math.log(n_select + 0.5)) / (a - b), 1.0 / 64, 63.0 / 64)
        v_lo, v_hi = key_value(lo), key_value(hi)
        guess = value_key(v_lo + frac * (v_hi - v_lo))
        middle = lo + lax.shift_right_logical(hi - lo, 1)
        cand = jnp.where(step % SEARCH_BISECT_EVERY == SEARCH_BISECT_EVERY - 1, middle, guess)
        cand = jnp.minimum(jnp.maximum(cand, lo + 1), hi - 1)
        cand_ref[...] = to_rows(cand)
        return cand, jnp.where(active, 1, 0), jnp.max(jnp.where(active, 1, 0))

    def search_cond(carry):
        return jnp.logical_and(carry[0] < SEARCH_MAX_STEPS, carry[1] > 0)

    def search_body(carry):
        step, _, cand, active, lo, c_lo, hi, c_hi = carry
        c = count_prepared()
        up = jnp.logical_and(active > 0, c >= n_select)
        down = jnp.logical_and(active > 0, c < n_select)
        lo, c_lo = jnp.where(up, cand, lo), jnp.where(up, c, c_lo)
        hi, c_hi = jnp.where(down, cand, hi), jnp.where(down, c, c_hi)
        cand, active, busy = propose(step + 1, lo, c_lo, hi, c_hi)
        return step + 1, busy, cand, active, lo, c_lo, hi, c_hi

    cand, active, busy = propose(jnp.int32(0), lo, c_lo, hi, c_hi)
    lo, c_lo, hi, c_hi = lax.while_loop(
        search_cond, search_body, (jnp.int32(0), busy, cand, active, lo, c_lo, hi, c_hi))[4:]

    over = jnp.logical_and(jnp.logical_not(few), jnp.logical_and(c_lo == n_select + 1, hi - lo > 1))
    least = smallest_ge(lo)
    c_drop = count_ge(jnp.where(over, least + 1, lo))
    dropped = jnp.logical_and(over, c_drop == n_select)
    theta = jnp.where(dropped, least + 1, jnp.where(over, least, lo))
    theta = jnp.where(few, INT_MIN, theta)
    c_above = jnp.where(over, c_drop, c_hi)
    tied = jnp.logical_and(jnp.logical_not(few), jnp.logical_and(c_lo > n_select, jnp.logical_not(dropped)))

    @pl.when(jnp.max(jnp.where(tied, 1, 0)) > 0)
    def _():
        theta_r = to_rows(theta)[:, 0:1]
        tied_r = to_rows(jnp.where(tied, 1, 0))[:, 0:1] > 0
        need = to_rows((n_select - c_above).astype(F32))[:, 0:1]
        r = lax.broadcasted_iota(I32, (tk, tk), 0)
        c = lax.broadcasted_iota(I32, (tk, tk), 1)
        before = jnp.where(r < c, 1.0, 0.0).astype(BF16)

        def body(j, seen):
            key = key_ref[j]
            eq = jnp.logical_and(key == theta_r, tied_r)
            eqf = jnp.where(eq, 1.0, 0.0)
            rank = seen + _dot(eqf.astype(BF16), before)
            key_ref[j] = jnp.where(jnp.logical_and(eq, rank >= need), INT_MIN, key)
            return seen + jnp.sum(eqf, axis=-1, keepdims=True)

        lax.fori_loop(0, n_blocks, body, jnp.zeros((tq, 1), F32))

    cand_ref[...] = to_rows(jnp.maximum(theta, INT_MIN + 1))

    qd = qd_ref[...]
    q_heads = []
    for h in range(DSA_HEADS):
        blk = qd[:, (h // 2) * LANES:(h // 2 + 1) * LANES]
        head = (lane >= (h % 2) * DSA_DIM) & (lane < (h % 2 + 1) * DSA_DIM)
        q_heads.append(jnp.where(head, blk, jnp.zeros_like(blk)))
    _softmax_init(acc_ref, l_ref, m_ref)

    def attend(j, c):
        start = pl.multiple_of(j * tk, tk)
        sel = key_ref[j] >= jnp.concatenate([cand_ref[...]] * reps, axis=1)
        kb = kd_ref[pl.ds(start, tk), :]
        vb = vd_ref[pl.ds(start, tk), :]
        for h in range(DSA_HEADS):
            p0 = (h // 2) * LANES
            s = jnp.where(sel, _dot_t(q_heads[h], kb[:, p0:p0 + LANES]), NEG_BIG)
            _softmax_step(s, vb[:, p0:p0 + LANES], h, acc_ref, l_ref, m_ref)
        return c

    lax.fori_loop(0, n_blocks, attend, 0)
    for p in range(DSA_HEADS // 2):
        o_ref[:, p * LANES:(p + 1) * LANES] = jnp.where(
            lane < DSA_DIM, _softmax_out(2 * p, acc_ref, l_ref), _softmax_out(2 * p + 1, acc_ref, l_ref)).astype(BF16)


def _dsa_attention(a, rp, w8, batch, seq, n_select):
    t = a.shape[0]
    tq = 256
    tk = min(1024, seq)
    nq = seq // tq
    width = DSA_HEADS * DSA_DIM
    keys = lambda col: pl.BlockSpec((seq, width), lambda b, i: (b, col), pipeline_mode=pl.Buffered(1))
    stat = lambda n: pltpu.VMEM((n, tq, LANES), F32)
    return pl.pallas_call(
        functools.partial(_dsa_kernel, tq=tq, tk=tk, n_select=n_select),
        grid=(batch, nq),
        in_specs=[
            pl.BlockSpec((tq, width), lambda b, i: (b * nq + i, 0)),
            pl.BlockSpec((tq, width), lambda b, i: (b * nq + i, 2)),
            pl.BlockSpec((tq, N_W), lambda b, i: (b * nq + i, 0)),
            keys(3),
            keys(1),
            keys(0),
        ],
        out_specs=pl.BlockSpec((tq, width), lambda b, i: (b * nq + i, 0)),
        out_shape=jax.ShapeDtypeStruct((t, width), BF16),
        scratch_shapes=[
            pltpu.VMEM((seq // tk, tq, tk), I32),
            pltpu.VMEM((IDX_HEADS * tq, width), BF16),
            stat(IDX_HEADS),
            pltpu.VMEM((2, tq, LANES), I32),
            pltpu.VMEM((tq, LANES), I32),
            pltpu.VMEM((tq, LANES), I32),
            pltpu.VMEM((3, tq, LANES), I32),
            stat(DSA_HEADS), stat(DSA_HEADS), stat(DSA_HEADS),
        ],
        compiler_params=_cparams(("arbitrary", "arbitrary")),
        name="dsa_attention",
    )(rp, rp, w8, rp, rp, a)


def _merge_kernel(x_ref, osb_ref, omla_ref, odsa_ref, g_ref, gm_ref, wsb_ref, wmla_ref, wdsa_ref, wout_ref,
                  o_ref):
    d = D_MODEL
    merged = (g_ref[:, 0:d].astype(F32) * _dot(osb_ref[...], wsb_ref[...])
              + g_ref[:, d:2 * d].astype(F32) * _dot(omla_ref[...], wmla_ref[...])
              + g_ref[:, 2 * d:3 * d].astype(F32) * _dot(odsa_ref[...], wdsa_ref[...]))
    o_ref[...] = x_ref[...] + gm_ref[...] * _dot(merged.astype(BF16), wout_ref[...])


def _merge(x, osb, omla, odsa, gates, gm, wsb, wmla, wdsa, wout, layer, seq):
    t, d = x.shape
    tm = 512
    per = seq // tm
    row = lambda i: (i, 0)
    wspec = lambda k: pl.BlockSpec((None, k, d), lambda i: (layer, 0, 0))
    return pl.pallas_call(
        _merge_kernel,
        grid=(t // tm,),
        in_specs=[
            pl.BlockSpec((tm, d), row),
            pl.BlockSpec((tm, osb.shape[1]), row),
            pl.BlockSpec((tm, omla.shape[1]), row),
            pl.BlockSpec((tm, odsa.shape[1]), row),
            pl.BlockSpec((tm, N_G), row),
            pl.BlockSpec((None, None, 1, d), lambda i: (layer, i // per, 0, 0)),
            wspec(osb.shape[1]), wspec(omla.shape[1]), wspec(odsa.shape[1]), wspec(d),
        ],
        out_specs=pl.BlockSpec((tm, d), row),
        out_shape=jax.ShapeDtypeStruct((t, d), F32),
        compiler_params=_cparams(("arbitrary",)),
        name="merge",
    )(x, osb, omla, odsa, gates, gm, wsb, wmla, wdsa, wout)


def _router_kernel(x_ref, sc_ref, sh_ref, rw_ref, rb_ref, h_ref, ti_ref, tw_ref, tr_ref, cnt_ref, run_ref, *, tm):
    @pl.when(pl.program_id(0) == 0)
    def _():
        run_ref[...] = jnp.zeros_like(run_ref)

    h = _rms(x_ref[...]) * sc_ref[...] + sh_ref[...]
    h_ref[...] = h
    logits = jnp.dot(h, rw_ref[...], preferred_element_type=F32, precision=lax.Precision.HIGHEST) + rb_ref[...]
    lane = lax.broadcasted_iota(I32, (tm, LANES), 1)
    work = logits
    vals, hots = [], []
    for _ in range(TOP_K):
        m = jnp.max(work, axis=-1, keepdims=True)
        first = jnp.min(jnp.where(work == m, lane, LANES), axis=-1, keepdims=True)
        hot = lane == first
        vals.append(m)
        hots.append(hot)
        work = jnp.where(hot, -jnp.inf, work)
    exps = [jnp.exp(v - vals[0]) for v in vals]
    denom = exps[0] + exps[1] + exps[2] + exps[3]
    chosen = jnp.zeros((tm, LANES), F32)
    for hot in hots:
        chosen = chosen + jnp.where(hot, 1.0, 0.0)
    r = lax.broadcasted_iota(I32, (tm, tm), 0)
    c = lax.broadcasted_iota(I32, (tm, tm), 1)
    earlier = jnp.where(c < r, 1.0, 0.0).astype(BF16)
    rank_all = _dot(earlier, chosen.astype(BF16)) + run_ref[...]
    ti = jnp.zeros((tm, LANES), I32)
    tw = jnp.zeros((tm, LANES), F32)
    tr = jnp.zeros((tm, LANES), I32)
    for k in range(TOP_K):
        e_k = jnp.sum(jnp.where(hots[k], lane, 0), axis=-1, keepdims=True)
        r_k = jnp.sum(jnp.where(hots[k], rank_all, 0.0), axis=-1, keepdims=True).astype(I32)
        ti = jnp.where(lane == k, e_k, ti)
        tw = jnp.where(lane == k, exps[k] / denom, tw)
        tr = jnp.where(lane == k, r_k, tr)
    ti_ref[...] = ti
    tw_ref[...] = tw
    tr_ref[...] = tr
    run_ref[...] = run_ref[...] + jnp.sum(chosen, axis=0, keepdims=True)
    cnt_ref[...] = run_ref[...]


def _router(x, scale, shift, rw, rb, layer, seq):
    t, d = x.shape
    tm = 256
    per = seq // tm
    row = lambda i: (i, 0)
    return pl.pallas_call(
        functools.partial(_router_kernel, tm=tm),
        grid=(t // tm,),
        in_specs=[
            pl.BlockSpec((tm, d), row),
            pl.BlockSpec((None, None, 1, d), lambda i: (layer, i // per, 0, 0)),
            pl.BlockSpec((None, None, 1, d), lambda i: (layer, i // per, 0, 0)),
            pl.BlockSpec((None, d, LANES), lambda i: (layer, 0, 0)),
            pl.BlockSpec((None, 1, LANES), lambda i: (layer, 0, 0)),
        ],
        out_specs=[
            pl.BlockSpec((tm, d), row),
            pl.BlockSpec((tm, LANES), row),
            pl.BlockSpec((tm, LANES), row),
            pl.BlockSpec((tm, LANES), row),
            pl.BlockSpec((1, LANES), lambda i: (0, 0)),
        ],
        out_shape=[
            jax.ShapeDtypeStruct((t, d), F32),
            jax.ShapeDtypeStruct((t, LANES), I32),
            jax.ShapeDtypeStruct((t, LANES), F32),
            jax.ShapeDtypeStruct((t, LANES), I32),
            jax.ShapeDtypeStruct((1, LANES), F32),
        ],
        scratch_shapes=[pltpu.VMEM((1, LANES), F32)],
        compiler_params=_cparams(("arbitrary",)),
        name="router",
    )(x, scale, shift, rw, rb)


def _expert_kernel(te_ref, tf_ref, nv_ref, x_ref, wgu_ref, bgu_ref, wd_ref, bd_ref, o_ref,
                   wgu_bf, wd_bf):
    i = pl.program_id(0)

    @pl.when(i >= nv_ref[0])
    def _():
        o_ref[...] = jnp.zeros_like(o_ref)

    @pl.when(i < nv_ref[0])
    def _():
        @pl.when(tf_ref[i] == 1)
        def _():
            wgu_bf[...] = wgu_ref[...].astype(BF16)
            wd_bf[...] = wd_ref[...].astype(BF16)

        gu = _dot(x_ref[...].astype(BF16), wgu_bf[...]) + bgu_ref[...]
        gate = jnp.minimum(gu[:, :D_EXPERT], SWIGLU_LIMIT)
        up = jnp.clip(gu[:, D_EXPERT:], -SWIGLU_LIMIT, SWIGLU_LIMIT)
        act = (up + 1.0) * (gate * (1.0 / (1.0 + jnp.exp(-SWIGLU_ALPHA * gate))))
        o_ref[...] = _dot(act.astype(BF16), wd_bf[...]) + bd_ref[...]


def _experts(xs, tile_e, tile_first, n_valid, wgu, bgu, wd, bd, layer, tm):
    p, d = xs.shape
    n_tiles = p // tm
    grid_spec = pltpu.PrefetchScalarGridSpec(
        num_scalar_prefetch=3,
        grid=(n_tiles,),
        in_specs=[
            pl.BlockSpec((tm, d), lambda i, te, tf, nv: (i, 0)),
            pl.BlockSpec((None, None, d, 2 * D_EXPERT), lambda i, te, tf, nv: (layer, te[i], 0, 0)),
            pl.BlockSpec((None, None, 1, 2 * D_EXPERT), lambda i, te, tf, nv: (layer, te[i], 0, 0)),
            pl.BlockSpec((None, None, D_EXPERT, d), lambda i, te, tf, nv: (layer, te[i], 0, 0)),
            pl.BlockSpec((None, None, 1, d), lambda i, te, tf, nv: (layer, te[i], 0, 0)),
        ],
        out_specs=pl.BlockSpec((tm, d), lambda i, te, tf, nv: (i, 0)),
        scratch_shapes=[pltpu.VMEM((d, 2 * D_EXPERT), BF16), pltpu.VMEM((D_EXPERT, d), BF16)],
    )
    return pl.pallas_call(
        _expert_kernel,
        grid_spec=grid_spec,
        out_shape=jax.ShapeDtypeStruct((p, d), F32),
        compiler_params=_cparams(("arbitrary",)),
        name="experts",
    )(tile_e, tile_first, n_valid, xs, wgu, bgu, wd, bd)


def _combine_kernel(x_ref, y0_ref, y1_ref, y2_ref, y3_ref, tw_ref, g_ref, fg_ref, o_ref, *, final):
    tw = tw_ref[...]
    y = tw[:, 0:1] * y0_ref[...]
    for k, y_ref in enumerate((y1_ref, y2_ref, y3_ref), start=1):
        y = y + tw[:, k:k + 1] * y_ref[...]
    x = x_ref[...] + g_ref[...] * y
    if final:
        x = _rms(x) * fg_ref[...]
    o_ref[...] = x


def _combine(x, yg, tw, gf, final_g, layer, seq, final):
    t, d = x.shape
    tm = 256
    nt = t // tm
    per = seq // tm
    slot = lambda k: pl.BlockSpec((tm, d), lambda i: (k * nt + i, 0))
    return pl.pallas_call(
        functools.partial(_combine_kernel, final=final),
        grid=(nt,),
        in_specs=[
            pl.BlockSpec((tm, d), lambda i: (i, 0)),
            slot(0), slot(1), slot(2), slot(3),
            pl.BlockSpec((tm, LANES), lambda i: (i, 0)),
            pl.BlockSpec((None, None, 1, d), lambda i: (layer, i // per, 0, 0)),
            pl.BlockSpec((1, d), lambda i: (0, 0)),
        ],
        out_specs=pl.BlockSpec((tm, d), lambda i: (i, 0)),
        out_shape=jax.ShapeDtypeStruct((t, d), F32),
        compiler_params=_cparams(("arbitrary",)),
        name="combine",
    )(x, yg, yg, yg, yg, tw, gf, final_g)


def _rope_tables(positions):
    pos = positions.reshape(-1).astype(F32)

    def cs(dim):
        inv_freq = ROPE_THETA ** (-jnp.arange(0, dim, 2, dtype=F32) / dim)
        ang = pos[:, None] * inv_freq
        return jnp.cos(ang), jnp.sin(ang)

    c_d, s_d = cs(DSA_DIM)
    c_i, s_i = cs(IDX_DIM)
    c_m, s_m = cs(MLA_ROPE)
    t = pos.shape[0]
    ones, zeros = jnp.ones((t, 64), F32), jnp.zeros((t, 64), F32)
    cos_t = jnp.concatenate([jnp.tile(c_d, (1, 4)), jnp.tile(c_i, (1, 8)),
                             ones, c_m, c_m, ones[:, :32]], axis=1)
    sin_t = jnp.concatenate([jnp.tile(s_d, (1, 4)), jnp.tile(s_i, (1, 8)),
                             zeros, s_m, s_m, zeros[:, :32]], axis=1)
    return cos_t, sin_t


def _permute_cols(w, idx, sgn):
    return (w.at[..., jnp.asarray(idx)].get(mode="promise_in_bounds") * jnp.asarray(sgn)).astype(BF16)


def _dispatch_tables(ti, tr, counts, tm):
    t = ti.shape[0]
    p = t * TOP_K + N_EXPERTS * tm
    n_tiles = p // tm
    cnt = counts.astype(I32)
    padded = ((cnt + tm - 1) // tm) * tm
    ends = jnp.cumsum(padded)
    starts = ends - padded
    pos = starts[ti] + tr
    tok = jnp.repeat(jnp.arange(t, dtype=I32), TOP_K)
    row_src = jnp.zeros((p,), I32).at[pos.reshape(-1)].set(tok, unique_indices=True, mode="promise_in_bounds")
    tile_start = jnp.arange(n_tiles, dtype=I32) * tm
    n_valid = ends[-1] // tm
    tile_e = jnp.sum((ends[None, :] <= tile_start[:, None]).astype(I32), axis=1)
    tile_e = jnp.minimum(tile_e, N_EXPERTS - 1)
    last_e = jnp.max(jnp.where(tile_start < ends[-1], tile_e, 0))
    tile_e = jnp.where(tile_start < ends[-1], tile_e, last_e)
    tile_first = jnp.concatenate([jnp.ones((1,), I32), (tile_e[1:] != tile_e[:-1]).astype(I32)])
    return pos, row_src, tile_e, tile_first, n_valid.astype(I32).reshape(1)


def kernel(x, c, positions, ada_w, ada_b, norm_mix_g, w_in, mla_q_norm_g, mla_kv_norm_g, mla_w_uq, mla_w_ukv,
           w_sb_out, w_mla_out, w_dsa_out, w_out, norm_ffn_g, router_w, router_b, expert_w_gu, expert_b_gu,
           expert_w_down, expert_b_down, final_norm_g):
    batch, seq, d = x.shape
    depth = ada_w.shape[0]
    t = batch * seq
    n_select = min(DSA_TOPK_MAX, seq // 4)
    tm_e = 256

    mod = _ada_mod(c, ada_w, ada_b)
    sh_m, sc_m, g_m, sh_f, sc_f, g_f = [m[:, :, None, :] for m in jnp.split(mod, 6, axis=-1)]
    scale_m = norm_mix_g[:, None, None, :] * (1.0 + sc_m)
    scale_f = norm_ffn_g[:, None, None, :] * (1.0 + sc_f)

    cos_t, sin_t = _rope_tables(positions)
    w_in_p = _permute_cols(w_in, _IN_IDX, _IN_SGN)
    w_uq_p = _permute_cols(mla_w_uq, _UQ_IDX, _UQ_SGN)
    w_ukv_p = _permute_cols(mla_w_ukv, _UKV_IDX, _UKV_SGN)
    w_sb_b, w_mla_b, w_dsa_b, w_out_b = [w.astype(BF16) for w in (w_sb_out, w_mla_out, w_dsa_out, w_out)]
    gq = mla_q_norm_g[:, None, :]
    gkv = mla_kv_norm_g[:, None, :]
    rw_p = jnp.zeros((depth, d, LANES), F32).at[:, :, :N_EXPERTS].set(router_w)
    rb_p = jnp.full((depth, 1, LANES), -jnp.inf, F32).at[:, 0, :N_EXPERTS].set(router_b)
    bgu = expert_b_gu[:, :, None, :]
    bd = expert_b_down[:, :, None, :]
    fg = final_norm_g[None, :]

    xf = x.reshape(t, d)
    for l in range(depth):
        a, cl, rp, w8, gates = _inproj(xf, scale_m, sh_m, w_in_p, cos_t, sin_t, l, seq)
        o_sb = _sb_attention(a, batch, seq)
        qm, km, vm = _mla_up(cl, gq, gkv, w_uq_p, w_ukv_p, cos_t, sin_t, rp, l)
        o_mla = _mla_attention(qm, km, vm, batch, seq)
        o_dsa = _dsa_attention(a, rp, w8, batch, seq, n_select)
        xf = _merge(xf, o_sb, o_mla, o_dsa, gates, g_m, w_sb_b, w_mla_b, w_dsa_b, w_out_b, l, seq)

        hf, ti, tw, tr, counts = _router(xf, scale_f, sh_f, rw_p, rb_p, l, seq)
        pos, row_src, tile_e, tile_first, n_valid = _dispatch_tables(
            ti[:, :TOP_K], tr[:, :TOP_K], counts[0, :N_EXPERTS], tm_e)
        xs = hf.at[row_src].get(mode="promise_in_bounds")
        ys = _experts(xs, tile_e, tile_first, n_valid, expert_w_gu, bgu, expert_w_down, bd, l, tm_e)
        yg = ys.at[pos.T.reshape(-1)].get(mode="promise_in_bounds")
        xf = _combine(xf, yg, tw, g_f, fg, l, seq, l == depth - 1)
    return xf.reshape(batch, seq, d)
```

```python
import functools
import math

import numpy as np
import jax
import jax.numpy as jnp
from jax import lax
from jax.experimental import pallas as pl
from jax.experimental.pallas import tpu as pltpu

F32 = jnp.float32
BF16 = jnp.bfloat16
I32 = jnp.int32

D_MODEL = 1024
CHUNK = 64
CHUNK_SHIFT = 6
ROPE_THETA = 10000.0
NORM_EPS = 1e-6
SB_HEADS, SB_DIM = 6, 64
MLA_HEADS, MLA_NOPE, MLA_ROPE, MLA_V = 6, 64, 32, 64
MLA_Q_RANK, MLA_KV_RANK = 256, 128
DSA_HEADS, DSA_DIM = 4, 64
IDX_HEADS, IDX_DIM = 8, 32
DSA_TOPK_MAX = 256
N_EXPERTS, TOP_K = 32, 4
D_EXPERT = D_MODEL
SWIGLU_LIMIT = 7.0
SWIGLU_ALPHA = 1.702

LANES = 128
VMEM_LIMIT = 56 * 1024 * 1024

INT_MIN = -(2 ** 31)
NEG_BIG = -1e30
SB_UNDERFLOW = 104.0
LOG2E = math.log2(math.e)

_SPLIT = (384, 384, 384, 256, 128, 32, 256, 256, 256, 256, 32, 8, 3072)
_OFF = np.concatenate([[0], np.cumsum(_SPLIT)]).astype(np.int64)
(O_SBQ, O_SBK, O_SBV, O_CQ, O_CKV, O_KR, O_DQ, O_DK, O_DV, O_IQ, O_IK, O_IW, O_GATE, D_IN) = [
    int(v) for v in _OFF]

N_A = 1408
N_C = 384
N_R = 1152
N_W = 128
N_G = 3072
N_IN = N_A + N_C + N_R + N_W + N_G
_ROPE_SPANS = ((0, 2, 0, DSA_DIM, DSA_DIM ** -0.5 * LOG2E), (256, 2, 0, DSA_DIM, 1.0), (512, 2, 1, IDX_DIM, 1.0),
               (768, 2, 1, IDX_DIM, 1.0), (1024, 1, 2, MLA_ROPE, 1.0))


def _rot_cols(base, n_heads, d):
    half = d // 2
    idx, sgn = [], []
    for h in range(n_heads):
        for j in range(d):
            if j < half:
                idx.append(base + h * d + j + half)
                sgn.append(-1.0)
            else:
                idx.append(base + h * d + j - half)
                sgn.append(1.0)
    return idx, sgn


def _in_layout():
    idx, sgn = [], []

    def plain(base, n):
        idx.extend(range(base, base + n))
        sgn.extend([1.0] * n)

    def pad(n):
        idx.extend([0] * n)
        sgn.extend([0.0] * n)

    plain(O_DV, 256); plain(O_SBQ, 384); plain(O_SBK, 384); plain(O_SBV, 384)
    plain(O_CQ, 256); plain(O_CKV, 128)
    plain(O_DQ, 256); plain(O_DK, 256); plain(O_IQ, 256)
    for _ in range(IDX_HEADS):
        plain(O_IK, IDX_DIM)
    pad(64); plain(O_KR, 32); pad(32)
    plain(O_IW, IDX_HEADS); pad(N_W - IDX_HEADS)
    plain(O_GATE, N_G)
    assert len(idx) == N_IN
    return np.asarray(idx, np.int32), np.asarray(sgn, np.float32)


_IN_IDX, _IN_SGN = _in_layout()


def _uq_layout():
    per = MLA_NOPE + MLA_ROPE
    idx, sgn = [], []
    for h in range(MLA_HEADS):
        idx.extend(range(h * per, h * per + per)); sgn.extend([1.0] * per)
        idx.extend([0] * 32); sgn.extend([0.0] * 32)
    for h in range(MLA_HEADS):
        idx.extend([0] * MLA_NOPE); sgn.extend([0.0] * MLA_NOPE)
        i, s = _rot_cols(h * per + MLA_NOPE, 1, MLA_ROPE)
        idx.extend(i); sgn.extend(s)
        idx.extend([0] * 32); sgn.extend([0.0] * 32)
    return np.asarray(idx, np.int32), np.asarray(sgn, np.float32)


def _ukv_layout():
    per = MLA_NOPE + MLA_V
    idx, sgn = [], []
    for h in range(MLA_HEADS):
        idx.extend(range(h * per, h * per + MLA_NOPE)); sgn.extend([1.0] * MLA_NOPE)
        idx.extend([0] * 64); sgn.extend([0.0] * 64)
    for h in range(MLA_HEADS):
        idx.extend(range(h * per + MLA_NOPE, h * per + per)); sgn.extend([1.0] * MLA_V)
    return np.asarray(idx, np.int32), np.asarray(sgn, np.float32)


_UQ_IDX, _UQ_SGN = _uq_layout()
_UKV_IDX, _UKV_SGN = _ukv_layout()
N_QM = MLA_HEADS * LANES
N_VM = MLA_HEADS * MLA_V


def _cparams(sem):
    return pltpu.CompilerParams(dimension_semantics=sem, vmem_limit_bytes=VMEM_LIMIT)


def _dot(a, b):
    return jnp.dot(a, b, preferred_element_type=F32)


def _dot_t(a, b):
    return lax.dot_general(a, b, (((1,), (1,)), ((), ())), preferred_element_type=F32)


def _rms(x):
    return x * lax.rsqrt(jnp.mean(x * x, axis=-1, keepdims=True) + NORM_EPS)


def _ada_kernel(c_ref, w_ref, b_ref, o_ref):
    c = c_ref[...]
    sc = c * (1.0 / (1.0 + jnp.exp(-c)))
    o_ref[...] = jnp.dot(sc, w_ref[...], preferred_element_type=F32,
                         precision=lax.Precision.HIGHEST) + b_ref[...]


def _ada_mod(c, ada_w, ada_b):
    depth, d, n = ada_w.shape
    b = c.shape[0]
    rows = 8
    cp = jnp.zeros((rows, d), F32).at[:b].set(c)
    tn = 2048
    out = pl.pallas_call(
        _ada_kernel,
        grid=(depth, n // tn),
        in_specs=[
            pl.BlockSpec((rows, d), lambda l, j: (0, 0)),
            pl.BlockSpec((None, d, tn), lambda l, j: (l, 0, j)),
            pl.BlockSpec((None, 1, tn), lambda l, j: (l, 0, j)),
        ],
        out_specs=pl.BlockSpec((None, rows, tn), lambda l, j: (l, 0, j)),
        out_shape=jax.ShapeDtypeStruct((depth, rows, n), F32),
        compiler_params=_cparams(("arbitrary", "arbitrary")),
        name="ada_mod",
    )(cp, ada_w, ada_b.reshape(depth, 1, n))
    return out[:, :b]


def _inproj_kernel(x_ref, sc_ref, sh_ref, w_ref, cos_ref, sin_ref,
                   a_ref, c_ref, r_ref, w8_ref, g_ref):
    h = (_rms(x_ref[...]) * sc_ref[...] + sh_ref[...]).astype(BF16)
    o = 0
    for c0 in range(0, N_A, 256):
        c1 = min(c0 + 256, N_A)
        a_ref[:, c0:c1] = _dot(h, w_ref[:, o + c0:o + c1]).astype(BF16)
    o += N_A
    c_ref[...] = _dot(h, w_ref[:, o:o + N_C])
    o += N_C
    for lo, n, kind, dim, scale in _ROPE_SPANS:
        y = _dot(h, w_ref[:, o + lo:o + lo + n * LANES])
        width, half = n * LANES, dim // 2
        first = (lax.broadcasted_iota(I32, (1, width), 1) & (dim - 1)) < half
        yr = jnp.where(first, -pltpu.roll(y, width - half, 1), pltpu.roll(y, half, 1))
        cs = jnp.concatenate([cos_ref[:, kind * LANES:(kind + 1) * LANES]] * n, axis=1)
        sn = jnp.concatenate([sin_ref[:, kind * LANES:(kind + 1) * LANES]] * n, axis=1)
        r = y * cs + yr * sn
        if scale != 1.0:
            r = r * scale
        r_ref[:, lo:lo + n * LANES] = r.astype(BF16)
    o += N_R
    w8_ref[...] = _dot(h, w_ref[:, o:o + N_W]) * (IDX_DIM ** -0.5 * IDX_HEADS ** -0.5)
    o += N_W
    for c0 in range(0, N_G, 512):
        z = _dot(h, w_ref[:, o + c0:o + c0 + 512])
        g_ref[:, c0:c0 + 512] = (1.0 / (1.0 + jnp.exp(-z))).astype(BF16)


def _inproj(x, scale, shift, w, cos_t, sin_t, layer, seq):
    t, d = x.shape
    tm = 512
    per = seq // tm
    return pl.pallas_call(
        _inproj_kernel,
        grid=(t // tm,),
        in_specs=[
            pl.BlockSpec((tm, d), lambda i: (i, 0)),
            pl.BlockSpec((None, None, 1, d), lambda i: (layer, i // per, 0, 0)),
            pl.BlockSpec((None, None, 1, d), lambda i: (layer, i // per, 0, 0)),
            pl.BlockSpec((None, d, N_IN), lambda i: (layer, 0, 0), pipeline_mode=pl.Buffered(1)),
            pl.BlockSpec((tm, 3 * LANES), lambda i: (i, 0)),
            pl.BlockSpec((tm, 3 * LANES), lambda i: (i, 0)),
        ],
        out_specs=[
            pl.BlockSpec((tm, N_A), lambda i: (i, 0)),
            pl.BlockSpec((tm, N_C), lambda i: (i, 0)),
            pl.BlockSpec((tm, N_R), lambda i: (i, 0)),
            pl.BlockSpec((tm, N_W), lambda i: (i, 0)),
            pl.BlockSpec((tm, N_G), lambda i: (i, 0)),
        ],
        out_shape=[
            jax.ShapeDtypeStruct((t, N_A), BF16),
            jax.ShapeDtypeStruct((t, N_C), F32),
            jax.ShapeDtypeStruct((t, N_R), BF16),
            jax.ShapeDtypeStruct((t, N_W), F32),
            jax.ShapeDtypeStruct((t, N_G), BF16),
        ],
        compiler_params=_cparams(("arbitrary",)),
        name="inproj",
    )(x, scale, shift, w, cos_t, sin_t)


def _mla_up_kernel(c_ref, gq_ref, gkv_ref, wq_ref, wkv_ref, cos_ref, sin_ref, kr_ref,
                   q_ref, k_ref, v_ref):
    c = c_ref[...]
    nq = (_rms(c[:, :MLA_Q_RANK]) * gq_ref[...]).astype(BF16)
    nkv = (_rms(c[:, MLA_Q_RANK:]) * gkv_ref[...]).astype(BF16)
    scale = (MLA_NOPE + MLA_ROPE) ** -0.5 * LOG2E
    cs = cos_ref[...] * scale
    sn = sin_ref[...] * scale
    kr = kr_ref[...].astype(F32)
    for h in range(MLA_HEADS):
        lo = h * LANES
        y = _dot(nq, wq_ref[:, lo:lo + LANES])
        yr = _dot(nq, wq_ref[:, N_QM + lo:N_QM + lo + LANES])
        q_ref[:, lo:lo + LANES] = (y * cs + yr * sn).astype(BF16)
        k_ref[:, lo:lo + LANES] = (_dot(nkv, wkv_ref[:, lo:lo + LANES]) + kr).astype(BF16)
    v_ref[...] = _dot(nkv, wkv_ref[:, N_QM:N_QM + N_VM]).astype(BF16)


def _mla_up(cl, gq, gkv, wq, wkv, cos_t, sin_t, rp, layer):
    t = cl.shape[0]
    tm = 512
    return pl.pallas_call(
        _mla_up_kernel,
        grid=(t // tm,),
        in_specs=[
            pl.BlockSpec((tm, N_C), lambda i: (i, 0)),
            pl.BlockSpec((None, 1, MLA_Q_RANK), lambda i: (layer, 0, 0)),
            pl.BlockSpec((None, 1, MLA_KV_RANK), lambda i: (layer, 0, 0)),
            pl.BlockSpec((None, MLA_Q_RANK, 2 * N_QM), lambda i: (layer, 0, 0)),
            pl.BlockSpec((None, MLA_KV_RANK, N_QM + N_VM), lambda i: (layer, 0, 0)),
            pl.BlockSpec((tm, LANES), lambda i: (i, 2)),
            pl.BlockSpec((tm, LANES), lambda i: (i, 2)),
            pl.BlockSpec((tm, LANES), lambda i: (i, 8)),
        ],
        out_specs=[
            pl.BlockSpec((tm, N_QM), lambda i: (i, 0)),
            pl.BlockSpec((tm, N_QM), lambda i: (i, 0)),
            pl.BlockSpec((tm, N_VM), lambda i: (i, 0)),
        ],
        out_shape=[
            jax.ShapeDtypeStruct((t, N_QM), BF16),
            jax.ShapeDtypeStruct((t, N_QM), BF16),
            jax.ShapeDtypeStruct((t, N_VM), BF16),
        ],
        compiler_params=_cparams(("arbitrary",)),
        name="mla_up",
    )(cl, gq, gkv, wq, wkv, cos_t, sin_t, rp)


def _sb_kernel(q_ref, k_ref, v_ref, o_ref, acc_ref, rem_ref, *, tq):
    qi = pl.program_id(2)
    lane = lax.broadcasted_iota(I32, (1, LANES), 1)
    row = lax.broadcasted_iota(I32, (tq, tq), 0)
    col = lax.broadcasted_iota(I32, (tq, tq), 1)
    causal = col < row
    tri = jnp.where(row > col, 1.0, 0.0).astype(BF16)
    q = q_ref[...]
    q_heads = []
    for h in range(2):
        head = (lane >= h * SB_DIM) & (lane < (h + 1) * SB_DIM)
        q_heads.append(jnp.where(head, q, jnp.zeros_like(q)) * jnp.asarray(SB_DIM ** -0.5, BF16))
    acc_ref[...] = jnp.zeros_like(acc_ref)
    rem_ref[...] = jnp.zeros_like(rem_ref)

    def block(j, diagonal, live=None):
        start = pl.multiple_of(j * tq, tq)
        kb = k_ref[pl.ds(start, tq), :]
        vb = v_ref[pl.ds(start, tq), :]
        for h in range(2):
            z = _dot_t(q_heads[h], kb)
            soft = jnp.log1p(jnp.exp(-jnp.abs(z)))
            log_stay = -(jnp.maximum(z, 0.0) + soft)
            if diagonal:
                log_stay = jnp.where(causal, log_stay, 0.0)
            if live is not None:
                log_stay = jnp.where(live, log_stay, 0.0)
            hi = log_stay.astype(BF16)
            lo = (log_stay - hi.astype(F32)).astype(BF16)
            later = _dot(hi, tri) + _dot(lo, tri)
            remain = rem_ref[h]
            log_a = (z + log_stay) + later + jnp.concatenate([remain] * (tq // LANES), axis=1)
            a = jnp.exp(log_a)
            if diagonal:
                a = jnp.where(causal, a, 0.0)
            if live is not None:
                a = jnp.where(live, a, 0.0)
            acc_ref[h] += _dot(a.astype(BF16), vb)
            rem_ref[h] = remain + jnp.sum(log_stay, axis=-1, keepdims=True)

    block(qi, True)
    block(jnp.maximum(qi - 1, 0), False, live=qi > 0)

    def cond(j):
        return jnp.logical_and(j >= 0, jnp.max(rem_ref[...]) > -SB_UNDERFLOW)

    def body(j):
        block(j, False)
        return j - 1

    lax.while_loop(cond, body, qi - 2)
    o_ref[...] = jnp.where(lane < SB_DIM, acc_ref[0], acc_ref[1]).astype(BF16)


def _sb_attention(a, batch, seq):
    t = a.shape[0]
    tq = 256
    nq = seq // tq
    pairs = SB_HEADS // 2
    stat = pltpu.VMEM((2, tq, LANES), F32)
    return pl.pallas_call(
        functools.partial(_sb_kernel, tq=tq),
        grid=(batch, pairs, nq),
        in_specs=[
            pl.BlockSpec((tq, LANES), lambda b, p, i: (b * nq + i, 2 + p)),
            pl.BlockSpec((seq, LANES), lambda b, p, i: (b, 2 + pairs + p)),
            pl.BlockSpec((seq, LANES), lambda b, p, i: (b, 2 + 2 * pairs + p)),
        ],
        out_specs=pl.BlockSpec((tq, LANES), lambda b, p, i: (b * nq + i, p)),
        out_shape=jax.ShapeDtypeStruct((t, SB_HEADS * SB_DIM), BF16),
        scratch_shapes=[stat, stat],
        compiler_params=_cparams(("arbitrary", "arbitrary", "arbitrary")),
        name="sb_attention",
    )(a, a, a)


def _softmax_step(s, vb, h, acc_ref, l_ref, m_ref):
    reps = s.shape[1] // LANES
    m_old = m_ref[h]
    m_new = jnp.maximum(m_old, jnp.max(s, axis=-1, keepdims=True))
    alpha = jnp.exp2(m_old - m_new)
    p = jnp.exp2(s - jnp.concatenate([m_new] * reps, axis=1))
    part = p[:, 0:LANES]
    for c in range(1, reps):
        part = part + p[:, c * LANES:(c + 1) * LANES]
    l_ref[h] = alpha * l_ref[h] + part
    acc_ref[h] = alpha * acc_ref[h] + _dot(p.astype(BF16), vb)
    m_ref[h] = m_new


def _softmax_init(acc_ref, l_ref, m_ref):
    acc_ref[...] = jnp.zeros_like(acc_ref)
    l_ref[...] = jnp.zeros_like(l_ref)
    m_ref[...] = jnp.full_like(m_ref, NEG_BIG)


def _softmax_out(h, acc_ref, l_ref):
    return acc_ref[h] / jnp.sum(l_ref[h], axis=-1, keepdims=True)


def _mla_kernel(q_ref, k_ref, v_ref, o_ref, acc_ref, l_ref, m_ref, *, tq, widths):
    qi = pl.program_id(2)
    lane = lax.broadcasted_iota(I32, (1, LANES), 1)
    row = lax.broadcasted_iota(I32, (tq, tq), 0)
    col = lax.broadcasted_iota(I32, (tq, tq), 1)
    visible = (col >> CHUNK_SHIFT) <= (row >> CHUNK_SHIFT)
    q = q_ref[...]
    _softmax_init(acc_ref, l_ref, m_ref)

    def step(start, tk, diagonal):
        kb = k_ref[pl.ds(start, tk), :]
        vb = v_ref[pl.ds(start, tk), :]
        for h in range(2):
            s = _dot_t(q[:, h * LANES:(h + 1) * LANES], kb[:, h * LANES:(h + 1) * LANES])
            if diagonal:
                s = jnp.where(visible, s, NEG_BIG)
            _softmax_step(s, vb, h, acc_ref, l_ref, m_ref)

    done = 0
    for width in widths:
        n_steps = (qi * tq - done) // width

        def body(j, c, width=width, done=done):
            step(pl.multiple_of(done + j * width, tq), width, False)
            return c

        lax.fori_loop(0, n_steps, body, 0)
        done = done + n_steps * width
    step(pl.multiple_of(qi * tq, tq), tq, True)
    o_ref[...] = jnp.where(lane < MLA_V, _softmax_out(0, acc_ref, l_ref),
                           _softmax_out(1, acc_ref, l_ref)).astype(BF16)


def _mla_attention(qm, km, vm, batch, seq):
    t = qm.shape[0]
    tq = min(512, seq)
    widths = tuple(w for w in (2048, 1024, 512) if tq <= w <= seq and w % tq == 0)
    nq = seq // tq
    pairs = MLA_HEADS // 2
    stat = pltpu.VMEM((2, tq, LANES), F32)
    return pl.pallas_call(
        functools.partial(_mla_kernel, tq=tq, widths=widths),
        grid=(batch, pairs, nq),
        in_specs=[
            pl.BlockSpec((tq, 2 * LANES), lambda b, p, i: (b * nq + i, p)),
            pl.BlockSpec((seq, 2 * LANES), lambda b, p, i: (b, p)),
            pl.BlockSpec((seq, LANES), lambda b, p, i: (b, p)),
        ],
        out_specs=pl.BlockSpec((tq, LANES), lambda b, p, i: (b * nq + i, p)),
        out_shape=jax.ShapeDtypeStruct((t, N_VM), BF16),
        scratch_shapes=[stat, stat, stat],
        compiler_params=_cparams(("arbitrary", "arbitrary", "arbitrary")),
        name="mla_attention",
    )(qm, km, vm)


SEARCH_BISECT_EVERY = 3
SEARCH_MAX_STEPS = 3 * 32 + 4


def _dsa_kernel(qd_ref, qx_ref, w_ref, kx_ref, kd_ref, vd_ref, o_ref,
                key_ref, qs_ref, wr_ref, top_ref, cand_ref, cnt_ref, cnt3_ref, acc_ref, l_ref, m_ref, *, tq, tk, n_select):
    it = pl.program_id(1)
    t0 = it * tq
    last = t0 // tk
    reps = tk // LANES
    lane = lax.broadcasted_iota(I32, (1, LANES), 1)
    lane2 = lax.broadcasted_iota(I32, (1, 2 * LANES), 1)
    rowid = t0 + lax.broadcasted_iota(I32, (tq, 1), 0)
    row_chunk = rowid >> CHUNK_SHIFT

    qx = qx_ref[...]
    w = w_ref[...]
    for h in range(IDX_HEADS):
        head = (lane2 >= h * IDX_DIM) & (lane2 < (h + 1) * IDX_DIM)
        qs_ref[h * tq:(h + 1) * tq, :] = jnp.where(head, qx, jnp.zeros_like(qx))
        wr_ref[h] = jnp.broadcast_to(w[:, h:h + 1], (tq, LANES))
    top_ref[...] = jnp.full_like(top_ref, INT_MIN)

    def score_block(j, diagonal):
        start = pl.multiple_of(j * tk, tk)
        d = _dot_t(qs_ref[...], kx_ref[pl.ds(start, tk), :])
        score = jnp.zeros((tq, tk), F32)
        for h in range(IDX_HEADS):
            wh = jnp.concatenate([wr_ref[h]] * reps, axis=1)
            score = score + wh * jnp.maximum(d[h * tq:(h + 1) * tq], 0.0)
        score = jnp.where(score == 0.0, 0.0, score)
        bits = pltpu.bitcast(score, I32)
        key = jnp.where(bits < 0, bits ^ jnp.int32(0x7FFFFFFF), bits)
        if diagonal:
            col_chunk = (start + lax.broadcasted_iota(I32, (1, tk), 1)) >> CHUNK_SHIFT
            key = jnp.where(col_chunk <= row_chunk, key, INT_MIN)
        key_ref[j] = key
        t1, t2 = top_ref[0], top_ref[1]
        for c in range(reps):
            x = key[:, c * LANES:(c + 1) * LANES]
            t2 = jnp.maximum(t2, jnp.minimum(t1, x))
            t1 = jnp.maximum(t1, x)
        top_ref[0] = t1
        top_ref[1] = t2

    def score_body(j, c):
        score_block(j, False)
        return c

    lax.fori_loop(0, last, score_body, 0)
    score_block(last, True)
    n_blocks = last + 1

    n_chunks = tq // LANES

    def to_lanes(rep):
        return jnp.concatenate([jnp.transpose(rep[c * LANES:(c + 1) * LANES, :])[0:1, :]
                                for c in range(n_chunks)], axis=1)

    def to_rows(row):
        return jnp.concatenate([jnp.transpose(jnp.broadcast_to(row[:, c * LANES:(c + 1) * LANES], (LANES, LANES)))
                                for c in range(n_chunks)], axis=0)

    def sweep(combine, start, finish):
        cnt_ref[...] = jnp.full_like(cnt_ref, start)

        def body(j, c):
            for half in range(n_chunks):
                rows = slice(half * LANES, (half + 1) * LANES)
                cb = jnp.concatenate([cand_ref[rows, :]] * reps, axis=1)
                cnt_ref[rows, :] = combine(cnt_ref[rows, :], key_ref[j, rows, :], cb)
            return c

        lax.fori_loop(0, n_blocks, body, 0)
        parts = cnt_ref[...]
        return jnp.concatenate([finish(jnp.transpose(parts[c * LANES:(c + 1) * LANES, :]))
                                for c in range(n_chunks)], axis=1)

    def add_ge(acc, keys, cb):
        ge = jnp.where(keys >= cb, 1, 0)
        part = ge[:, 0:LANES]
        for cc in range(1, reps):
            part = part + ge[:, cc * LANES:(cc + 1) * LANES]
        return acc + part

    def min_ge(acc, keys, cb):
        kept = jnp.where(keys >= cb, keys, jnp.int32(2 ** 31 - 1))
        part = kept[:, 0:LANES]
        for cc in range(1, reps):
            part = jnp.minimum(part, kept[:, cc * LANES:(cc + 1) * LANES])
        return jnp.minimum(acc, part)

    def count_prepared():
        return sweep(add_ge, 0, lambda x: jnp.sum(x, axis=0, keepdims=True))

    def count_with_signs(cand):
        cand_ref[...] = to_rows(cand)
        cnt3_ref[...] = jnp.zeros_like(cnt3_ref)

        def body(j, c):
            for half in range(n_chunks):
                rows = slice(half * LANES, (half + 1) * LANES)
                keys = key_ref[j, rows, :]
                cb = jnp.concatenate([cand_ref[rows, :]] * reps, axis=1)
                for slot, bound in enumerate((cb, 1, 0)):
                    cnt3_ref[slot, rows, :] = add_ge(cnt3_ref[slot, rows, :], keys, bound)
            return c

        lax.fori_loop(0, n_blocks, body, 0)
        return [jnp.concatenate([jnp.sum(jnp.transpose(cnt3_ref[slot, c * LANES:(c + 1) * LANES, :]), axis=0,
                                         keepdims=True) for c in range(n_chunks)], axis=1) for slot in range(3)]

    def count_ge(cand):
        cand_ref[...] = to_rows(cand)
        return count_prepared()

    def smallest_ge(cand):
        cand_ref[...] = to_rows(cand)
        return sweep(min_ge, 2 ** 31 - 1, lambda x: jnp.min(x, axis=0, keepdims=True))

    def key_value(k):
        return pltpu.bitcast(jnp.where(k < 0, k ^ jnp.int32(0x7FFFFFFF), k), F32)

    def value_key(v):
        bits = pltpu.bitcast(v, I32)
        return jnp.where(bits < 0, bits ^ jnp.int32(0x7FFFFFFF), bits)

    row_l = t0 + lax.broadcasted_iota(I32, (1, tq), 1)
    few = ((row_l >> CHUNK_SHIFT) + 1) * CHUNK <= n_select
    t2_min = jnp.broadcast_to(jnp.min(top_ref[1], axis=-1, keepdims=True), (tq, LANES))
    t1_max = jnp.broadcast_to(jnp.max(top_ref[0], axis=-1, keepdims=True), (tq, LANES))
    lo = to_lanes(t2_min)
    hi = to_lanes(t1_max) + 1
    c_lo, c_pos, c_nn = count_with_signs(lo)
    c_hi = jnp.zeros((1, tq), I32)
    positive = c_pos >= n_select
    negative = c_nn < n_select
    at_zero = jnp.logical_not(jnp.logical_or(positive, negative))
    raise_lo = jnp.logical_and(positive, lo < 1)
    lower_hi = jnp.logical_and(negative, hi > 0)
    lo, c_lo = jnp.where(raise_lo, 1, lo), jnp.where(raise_lo, c_pos, c_lo)
    hi, c_hi = jnp.where(lower_hi, 0, hi), jnp.where(lower_hi, c_nn, c_hi)
    lo, c_lo = jnp.where(at_zero, 0, lo), jnp.where(at_zero, c_nn, c_lo)
    hi, c_hi = jnp.where(at_zero, 1, hi), jnp.where(at_zero, c_pos, c_hi)

    def propose(step, lo, c_lo, hi, c_hi):
        active = jnp.logical_and(jnp.logical_not(few), jnp.logical_and(c_lo > n_select + 1, hi - lo > 1))
        a = jnp.log(c_lo.astype(F32))
        b = jnp.log(c_hi.astype(F32) + 0.5)
        frac = jnp.clip((a - math.log(n_select + 0.5)) / (a - b), 1.0 / 64, 63.0 / 64)
        v_lo, v_hi = key_value(lo), key_value(hi)
        guess = value_key(v_lo + frac * (v_hi - v_lo))
        middle = lo + lax.shift_right_logical(hi - lo, 1)
        cand = jnp.where(step % SEARCH_BISECT_EVERY == SEARCH_BISECT_EVERY - 1, middle, guess)
        cand = jnp.minimum(jnp.maximum(cand, lo + 1), hi - 1)
        cand_ref[...] = to_rows(cand)
        return cand, jnp.where(active, 1, 0), jnp.max(jnp.where(active, 1, 0))

    def search_cond(carry):
        return jnp.logical_and(carry[0] < SEARCH_MAX_STEPS, carry[1] > 0)

    def search_body(carry):
        step, _, cand, active, lo, c_lo, hi, c_hi = carry
        c = count_prepared()
        up = jnp.logical_and(active > 0, c >= n_select)
        down = jnp.logical_and(active > 0, c < n_select)
        lo, c_lo = jnp.where(up, cand, lo), jnp.where(up, c, c_lo)
        hi, c_hi = jnp.where(down, cand, hi), jnp.where(down, c, c_hi)
        cand, active, busy = propose(step + 1, lo, c_lo, hi, c_hi)
        return step + 1, busy, cand, active, lo, c_lo, hi, c_hi

    cand, active, busy = propose(jnp.int32(0), lo, c_lo, hi, c_hi)
    lo, c_lo, hi, c_hi = lax.while_loop(
        search_cond, search_body, (jnp.int32(0), busy, cand, active, lo, c_lo, hi, c_hi))[4:]

    over = jnp.logical_and(jnp.logical_not(few), jnp.logical_and(c_lo == n_select + 1, hi - lo > 1))
    least = smallest_ge(lo)
    c_drop = count_ge(jnp.where(over, least + 1, lo))
    dropped = jnp.logical_and(over, c_drop == n_select)
    theta = jnp.where(dropped, least + 1, jnp.where(over, least, lo))
    theta = jnp.where(few, INT_MIN, theta)
    c_above = jnp.where(over, c_drop, c_hi)
    tied = jnp.logical_and(jnp.logical_not(few), jnp.logical_and(c_lo > n_select, jnp.logical_not(dropped)))

    @pl.when(jnp.max(jnp.where(tied, 1, 0)) > 0)
    def _():
        theta_r = to_rows(theta)[:, 0:1]
        tied_r = to_rows(jnp.where(tied, 1, 0))[:, 0:1] > 0
        need = to_rows((n_select - c_above).astype(F32))[:, 0:1]
        r = lax.broadcasted_iota(I32, (tk, tk), 0)
        c = lax.broadcasted_iota(I32, (tk, tk), 1)
        before = jnp.where(r < c, 1.0, 0.0).astype(BF16)

        def body(j, seen):
            key = key_ref[j]
            eq = jnp.logical_and(key == theta_r, tied_r)
            eqf = jnp.where(eq, 1.0, 0.0)
            rank = seen + _dot(eqf.astype(BF16), before)
            key_ref[j] = jnp.where(jnp.logical_and(eq, rank >= need), INT_MIN, key)
            return seen + jnp.sum(eqf, axis=-1, keepdims=True)

        lax.fori_loop(0, n_blocks, body, jnp.zeros((tq, 1), F32))

    cand_ref[...] = to_rows(jnp.maximum(theta, INT_MIN + 1))

    qd = qd_ref[...]
    q_heads = []
    for h in range(DSA_HEADS):
        blk = qd[:, (h // 2) * LANES:(h // 2 + 1) * LANES]
        head = (lane >= (h % 2) * DSA_DIM) & (lane < (h % 2 + 1) * DSA_DIM)
        q_heads.append(jnp.where(head, blk, jnp.zeros_like(blk)))
    _softmax_init(acc_ref, l_ref, m_ref)

    def attend(j, c):
        start = pl.multiple_of(j * tk, tk)
        sel = key_ref[j] >= jnp.concatenate([cand_ref[...]] * reps, axis=1)
        kb = kd_ref[pl.ds(start, tk), :]
        vb = vd_ref[pl.ds(start, tk), :]
        for h in range(DSA_HEADS):
            p0 = (h // 2) * LANES
            s = jnp.where(sel, _dot_t(q_heads[h], kb[:, p0:p0 + LANES]), NEG_BIG)
            _softmax_step(s, vb[:, p0:p0 + LANES], h, acc_ref, l_ref, m_ref)
        return c

    lax.fori_loop(0, n_blocks, attend, 0)
    for p in range(DSA_HEADS // 2):
        o_ref[:, p * LANES:(p + 1) * LANES] = jnp.where(
            lane < DSA_DIM, _softmax_out(2 * p, acc_ref, l_ref), _softmax_out(2 * p + 1, acc_ref, l_ref)).astype(BF16)


def _dsa_attention(a, rp, w8, batch, seq, n_select):
    t = a.shape[0]
    tq = 256
    tk = min(1024, seq)
    nq = seq // tq
    width = DSA_HEADS * DSA_DIM
    keys = lambda col: pl.BlockSpec((seq, width), lambda b, i: (b, col), pipeline_mode=pl.Buffered(1))
    stat = lambda n: pltpu.VMEM((n, tq, LANES), F32)
    return pl.pallas_call(
        functools.partial(_dsa_kernel, tq=tq, tk=tk, n_select=n_select),
        grid=(batch, nq),
        in_specs=[
            pl.BlockSpec((tq, width), lambda b, i: (b * nq + i, 0)),
            pl.BlockSpec((tq, width), lambda b, i: (b * nq + i, 2)),
            pl.BlockSpec((tq, N_W), lambda b, i: (b * nq + i, 0)),
            keys(3),
            keys(1),
            keys(0),
        ],
        out_specs=pl.BlockSpec((tq, width), lambda b, i: (b * nq + i, 0)),
        out_shape=jax.ShapeDtypeStruct((t, width), BF16),
        scratch_shapes=[
            pltpu.VMEM((seq // tk, tq, tk), I32),
            pltpu.VMEM((IDX_HEADS * tq, width), BF16),
            stat(IDX_HEADS),
            pltpu.VMEM((2, tq, LANES), I32),
            pltpu.VMEM((tq, LANES), I32),
            pltpu.VMEM((tq, LANES), I32),
            pltpu.VMEM((3, tq, LANES), I32),
            stat(DSA_HEADS), stat(DSA_HEADS), stat(DSA_HEADS),
        ],
        compiler_params=_cparams(("arbitrary", "arbitrary")),
        name="dsa_attention",
    )(rp, rp, w8, rp, rp, a)


def _merge_kernel(x_ref, osb_ref, omla_ref, odsa_ref, g_ref, gm_ref, wsb_ref, wmla_ref, wdsa_ref, wout_ref,
                  o_ref):
    d = D_MODEL
    merged = (g_ref[:, 0:d].astype(F32) * _dot(osb_ref[...], wsb_ref[...])
              + g_ref[:, d:2 * d].astype(F32) * _dot(omla_ref[...], wmla_ref[...])
              + g_ref[:, 2 * d:3 * d].astype(F32) * _dot(odsa_ref[...], wdsa_ref[...]))
    o_ref[...] = x_ref[...] + gm_ref[...] * _dot(merged.astype(BF16), wout_ref[...])


def _merge(x, osb, omla, odsa, gates, gm, wsb, wmla, wdsa, wout, layer, seq):
    t, d = x.shape
    tm = 512
    per = seq // tm
    row = lambda i: (i, 0)
    wspec = lambda k: pl.BlockSpec((None, k, d), lambda i: (layer, 0, 0))
    return pl.pallas_call(
        _merge_kernel,
        grid=(t // tm,),
        in_specs=[
            pl.BlockSpec((tm, d), row),
            pl.BlockSpec((tm, osb.shape[1]), row),
            pl.BlockSpec((tm, omla.shape[1]), row),
            pl.BlockSpec((tm, odsa.shape[1]), row),
            pl.BlockSpec((tm, N_G), row),
            pl.BlockSpec((None, None, 1, d), lambda i: (layer, i // per, 0, 0)),
            wspec(osb.shape[1]), wspec(omla.shape[1]), wspec(odsa.shape[1]), wspec(d),
        ],
        out_specs=pl.BlockSpec((tm, d), row),
        out_shape=jax.ShapeDtypeStruct((t, d), F32),
        compiler_params=_cparams(("arbitrary",)),
        name="merge",
    )(x, osb, omla, odsa, gates, gm, wsb, wmla, wdsa, wout)


def _router_kernel(x_ref, sc_ref, sh_ref, rw_ref, rb_ref, h_ref, ti_ref, tw_ref, tr_ref, cnt_ref, run_ref, *, tm):
    @pl.when(pl.program_id(0) == 0)
    def _():
        run_ref[...] = jnp.zeros_like(run_ref)

    h = _rms(x_ref[...]) * sc_ref[...] + sh_ref[...]
    h_ref[...] = h
    logits = jnp.dot(h, rw_ref[...], preferred_element_type=F32, precision=lax.Precision.HIGHEST) + rb_ref[...]
    lane = lax.broadcasted_iota(I32, (tm, LANES), 1)
    work = logits
    vals, hots = [], []
    for _ in range(TOP_K):
        m = jnp.max(work, axis=-1, keepdims=True)
        first = jnp.min(jnp.where(work == m, lane, LANES), axis=-1, keepdims=True)
        hot = lane == first
        vals.append(m)
        hots.append(hot)
        work = jnp.where(hot, -jnp.inf, work)
    exps = [jnp.exp(v - vals[0]) for v in vals]
    denom = exps[0] + exps[1] + exps[2] + exps[3]
    chosen = jnp.zeros((tm, LANES), F32)
    for hot in hots:
        chosen = chosen + jnp.where(hot, 1.0, 0.0)
    r = lax.broadcasted_iota(I32, (tm, tm), 0)
    c = lax.broadcasted_iota(I32, (tm, tm), 1)
    earlier = jnp.where(c < r, 1.0, 0.0).astype(BF16)
    rank_all = _dot(earlier, chosen.astype(BF16)) + run_ref[...]
    ti = jnp.zeros((tm, LANES), I32)
    tw = jnp.zeros((tm, LANES), F32)
    tr = jnp.zeros((tm, LANES), I32)
    for k in range(TOP_K):
        e_k = jnp.sum(jnp.where(hots[k], lane, 0), axis=-1, keepdims=True)
        r_k = jnp.sum(jnp.where(hots[k], rank_all, 0.0), axis=-1, keepdims=True).astype(I32)
        ti = jnp.where(lane == k, e_k, ti)
        tw = jnp.where(lane == k, exps[k] / denom, tw)
        tr = jnp.where(lane == k, r_k, tr)
    ti_ref[...] = ti
    tw_ref[...] = tw
    tr_ref[...] = tr
    run_ref[...] = run_ref[...] + jnp.sum(chosen, axis=0, keepdims=True)
    cnt_ref[...] = run_ref[...]


def _router(x, scale, shift, rw, rb, layer, seq):
    t, d = x.shape
    tm = 256
    per = seq // tm
    row = lambda i: (i, 0)
    return pl.pallas_call(
        functools.partial(_router_kernel, tm=tm),
        grid=(t // tm,),
        in_specs=[
            pl.BlockSpec((tm, d), row),
            pl.BlockSpec((None, None, 1, d), lambda i: (layer, i // per, 0, 0)),
            pl.BlockSpec((None, None, 1, d), lambda i: (layer, i // per, 0, 0)),
            pl.BlockSpec((None, d, LANES), lambda i: (layer, 0, 0)),
            pl.BlockSpec((None, 1, LANES), lambda i: (layer, 0, 0)),
        ],
        out_specs=[
            pl.BlockSpec((tm, d), row),
            pl.BlockSpec((tm, LANES), row),
            pl.BlockSpec((tm, LANES), row),
            pl.BlockSpec((tm, LANES), row),
            pl.BlockSpec((1, LANES), lambda i: (0, 0)),
        ],
        out_shape=[
            jax.ShapeDtypeStruct((t, d), F32),
            jax.ShapeDtypeStruct((t, LANES), I32),
            jax.ShapeDtypeStruct((t, LANES), F32),
            jax.ShapeDtypeStruct((t, LANES), I32),
            jax.ShapeDtypeStruct((1, LANES), F32),
        ],
        scratch_shapes=[pltpu.VMEM((1, LANES), F32)],
        compiler_params=_cparams(("arbitrary",)),
        name="router",
    )(x, scale, shift, rw, rb)


def _expert_kernel(te_ref, tf_ref, nv_ref, x_ref, wgu_ref, bgu_ref, wd_ref, bd_ref, o_ref,
                   wgu_bf, wd_bf):
    i = pl.program_id(0)

    @pl.when(i >= nv_ref[0])
    def _():
        o_ref[...] = jnp.zeros_like(o_ref)

    @pl.when(i < nv_ref[0])
    def _():
        @pl.when(tf_ref[i] == 1)
        def _():
            wgu_bf[...] = wgu_ref[...].astype(BF16)
            wd_bf[...] = wd_ref[...].astype(BF16)

        gu = _dot(x_ref[...].astype(BF16), wgu_bf[...]) + bgu_ref[...]
        gate = jnp.minimum(gu[:, :D_EXPERT], SWIGLU_LIMIT)
        up = jnp.clip(gu[:, D_EXPERT:], -SWIGLU_LIMIT, SWIGLU_LIMIT)
        act = (up + 1.0) * (gate * (1.0 / (1.0 + jnp.exp(-SWIGLU_ALPHA * gate))))
        o_ref[...] = _dot(act.astype(BF16), wd_bf[...]) + bd_ref[...]


def _experts(xs, tile_e, tile_first, n_valid, wgu, bgu, wd, bd, layer, tm):
    p, d = xs.shape
    n_tiles = p // tm
    grid_spec = pltpu.PrefetchScalarGridSpec(
        num_scalar_prefetch=3,
        grid=(n_tiles,),
        in_specs=[
            pl.BlockSpec((tm, d), lambda i, te, tf, nv: (i, 0)),
            pl.BlockSpec((None, None, d, 2 * D_EXPERT), lambda i, te, tf, nv: (layer, te[i], 0, 0)),
            pl.BlockSpec((None, None, 1, 2 * D_EXPERT), lambda i, te, tf, nv: (layer, te[i], 0, 0)),
            pl.BlockSpec((None, None, D_EXPERT, d), lambda i, te, tf, nv: (layer, te[i], 0, 0)),
            pl.BlockSpec((None, None, 1, d), lambda i, te, tf, nv: (layer, te[i], 0, 0)),
        ],
        out_specs=pl.BlockSpec((tm, d), lambda i, te, tf, nv: (i, 0)),
        scratch_shapes=[pltpu.VMEM((d, 2 * D_EXPERT), BF16), pltpu.VMEM((D_EXPERT, d), BF16)],
    )
    return pl.pallas_call(
        _expert_kernel,
        grid_spec=grid_spec,
        out_shape=jax.ShapeDtypeStruct((p, d), F32),
        compiler_params=_cparams(("arbitrary",)),
        name="experts",
    )(tile_e, tile_first, n_valid, xs, wgu, bgu, wd, bd)


def _combine_kernel(x_ref, y0_ref, y1_ref, y2_ref, y3_ref, tw_ref, g_ref, fg_ref, o_ref, *, final):
    tw = tw_ref[...]
    y = tw[:, 0:1] * y0_ref[...]
    for k, y_ref in enumerate((y1_ref, y2_ref, y3_ref), start=1):
        y = y + tw[:, k:k + 1] * y_ref[...]
    x = x_ref[...] + g_ref[...] * y
    if final:
        x = _rms(x) * fg_ref[...]
    o_ref[...] = x


def _combine(x, yg, tw, gf, final_g, layer, seq, final):
    t, d = x.shape
    tm = 256
    nt = t // tm
    per = seq // tm
    slot = lambda k: pl.BlockSpec((tm, d), lambda i: (k * nt + i, 0))
    return pl.pallas_call(
        functools.partial(_combine_kernel, final=final),
        grid=(nt,),
        in_specs=[
            pl.BlockSpec((tm, d), lambda i: (i, 0)),
            slot(0), slot(1), slot(2), slot(3),
            pl.BlockSpec((tm, LANES), lambda i: (i, 0)),
            pl.BlockSpec((None, None, 1, d), lambda i: (layer, i // per, 0, 0)),
            pl.BlockSpec((1, d), lambda i: (0, 0)),
        ],
        out_specs=pl.BlockSpec((tm, d), lambda i: (i, 0)),
        out_shape=jax.ShapeDtypeStruct((t, d), F32),
        compiler_params=_cparams(("arbitrary",)),
        name="combine",
    )(x, yg, yg, yg, yg, tw, gf, final_g)


def _rope_tables(positions):
    pos = positions.reshape(-1).astype(F32)

    def cs(dim):
        inv_freq = ROPE_THETA ** (-jnp.arange(0, dim, 2, dtype=F32) / dim)
        ang = pos[:, None] * inv_freq
        return jnp.cos(ang), jnp.sin(ang)

    c_d, s_d = cs(DSA_DIM)
    c_i, s_i = cs(IDX_DIM)
    c_m, s_m = cs(MLA_ROPE)
    t = pos.shape[0]
    ones, zeros = jnp.ones((t, 64), F32), jnp.zeros((t, 64), F32)
    cos_t = jnp.concatenate([jnp.tile(c_d, (1, 4)), jnp.tile(c_i, (1, 8)),
                             ones, c_m, c_m, ones[:, :32]], axis=1)
    sin_t = jnp.concatenate([jnp.tile(s_d, (1, 4)), jnp.tile(s_i, (1, 8)),
                             zeros, s_m, s_m, zeros[:, :32]], axis=1)
    return cos_t, sin_t


def _permute_cols(w, idx, sgn):
    return (w.at[..., jnp.asarray(idx)].get(mode="promise_in_bounds") * jnp.asarray(sgn)).astype(BF16)


def _dispatch_tables(ti, tr, counts, tm):
    t = ti.shape[0]
    p = t * TOP_K + N_EXPERTS * tm
    n_tiles = p // tm
    cnt = counts.astype(I32)
    padded = ((cnt + tm - 1) // tm) * tm
    ends = jnp.cumsum(padded)
    starts = ends - padded
    pos = starts[ti] + tr
    tok = jnp.repeat(jnp.arange(t, dtype=I32), TOP_K)
    row_src = jnp.zeros((p,), I32).at[pos.reshape(-1)].set(tok, unique_indices=True, mode="promise_in_bounds")
    tile_start = jnp.arange(n_tiles, dtype=I32) * tm
    n_valid = ends[-1] // tm
    tile_e = jnp.sum((ends[None, :] <= tile_start[:, None]).astype(I32), axis=1)
    tile_e = jnp.minimum(tile_e, N_EXPERTS - 1)
    last_e = jnp.max(jnp.where(tile_start < ends[-1], tile_e, 0))
    tile_e = jnp.where(tile_start < ends[-1], tile_e, last_e)
    tile_first = jnp.concatenate([jnp.ones((1,), I32), (tile_e[1:] != tile_e[:-1]).astype(I32)])
    return pos, row_src, tile_e, tile_first, n_valid.astype(I32).reshape(1)


def kernel(x, c, positions, ada_w, ada_b, norm_mix_g, w_in, mla_q_norm_g, mla_kv_norm_g, mla_w_uq, mla_w_ukv,
           w_sb_out, w_mla_out, w_dsa_out, w_out, norm_ffn_g, router_w, router_b, expert_w_gu, expert_b_gu,
           expert_w_down, expert_b_down, final_norm_g):
    batch, seq, d = x.shape
    depth = ada_w.shape[0]
    t = batch * seq
    n_select = min(DSA_TOPK_MAX, seq // 4)
    tm_e = 256

    mod = _ada_mod(c, ada_w, ada_b)
    sh_m, sc_m, g_m, sh_f, sc_f, g_f = [m[:, :, None, :] for m in jnp.split(mod, 6, axis=-1)]
    scale_m = norm_mix_g[:, None, None, :] * (1.0 + sc_m)
    scale_f = norm_ffn_g[:, None, None, :] * (1.0 + sc_f)

    cos_t, sin_t = _rope_tables(positions)
    w_in_p = _permute_cols(w_in, _IN_IDX, _IN_SGN)
    w_uq_p = _permute_cols(mla_w_uq, _UQ_IDX, _UQ_SGN)
    w_ukv_p = _permute_cols(mla_w_ukv, _UKV_IDX, _UKV_SGN)
    w_sb_b, w_mla_b, w_dsa_b, w_out_b = [w.astype(BF16) for w in (w_sb_out, w_mla_out, w_dsa_out, w_out)]
    gq = mla_q_norm_g[:, None, :]
    gkv = mla_kv_norm_g[:, None, :]
    rw_p = jnp.zeros((depth, d, LANES), F32).at[:, :, :N_EXPERTS].set(router_w)
    rb_p = jnp.full((depth, 1, LANES), -jnp.inf, F32).at[:, 0, :N_EXPERTS].set(router_b)
    bgu = expert_b_gu[:, :, None, :]
    bd = expert_b_down[:, :, None, :]
    fg = final_norm_g[None, :]

    xf = x.reshape(t, d)
    for l in range(depth):
        a, cl, rp, w8, gates = _inproj(xf, scale_m, sh_m, w_in_p, cos_t, sin_t, l, seq)
        o_sb = _sb_attention(a, batch, seq)
        qm, km, vm = _mla_up(cl, gq, gkv, w_uq_p, w_ukv_p, cos_t, sin_t, rp, l)
        o_mla = _mla_attention(qm, km, vm, batch, seq)
        o_dsa = _dsa_attention(a, rp, w8, batch, seq, n_select)
        xf = _merge(xf, o_sb, o_mla, o_dsa, gates, g_m, w_sb_b, w_mla_b, w_dsa_b, w_out_b, l, seq)

        hf, ti, tw, tr, counts = _router(xf, scale_f, sh_f, rw_p, rb_p, l, seq)
        pos, row_src, tile_e, tile_first, n_valid = _dispatch_tables(
            ti[:, :TOP_K], tr[:, :TOP_K], counts[0, :N_EXPERTS], tm_e)
        xs = hf.at[row_src].get(mode="promise_in_bounds")
        ys = _experts(xs, tile_e, tile_first, n_valid, expert_w_gu, bgu, expert_w_down, bd, l, tm_e)
        yg = ys.at[pos.T.reshape(-1)].get(mode="promise_in_bounds")
        xf = _combine(xf, yg, tw, g_f, fg, l, seq, l == depth - 1)
    return xf.reshape(batch, seq, d)
```

```python
import functools
import math

import numpy as np
import jax
import jax.numpy as jnp
from jax import lax
from jax.experimental import pallas as pl
from jax.experimental.pallas import tpu as pltpu

F32 = jnp.float32
BF16 = jnp.bfloat16
I32 = jnp.int32

D_MODEL = 1024
CHUNK = 64
CHUNK_SHIFT = 6
ROPE_THETA = 10000.0
NORM_EPS = 1e-6
SB_HEADS, SB_DIM = 6, 64
MLA_HEADS, MLA_NOPE, MLA_ROPE, MLA_V = 6, 64, 32, 64
MLA_Q_RANK, MLA_KV_RANK = 256, 128
DSA_HEADS, DSA_DIM = 4, 64
IDX_HEADS, IDX_DIM = 8, 32
DSA_TOPK_MAX = 256
N_EXPERTS, TOP_K = 32, 4
D_EXPERT = D_MODEL
SWIGLU_LIMIT = 7.0
SWIGLU_ALPHA = 1.702

LANES = 128
VMEM_LIMIT = 56 * 1024 * 1024

INT_MIN = -(2 ** 31)
NEG_BIG = -1e30
SB_UNDERFLOW = 104.0
LOG2E = math.log2(math.e)

_SPLIT = (384, 384, 384, 256, 128, 32, 256, 256, 256, 256, 32, 8, 3072)
_OFF = np.concatenate([[0], np.cumsum(_SPLIT)]).astype(np.int64)
(O_SBQ, O_SBK, O_SBV, O_CQ, O_CKV, O_KR, O_DQ, O_DK, O_DV, O_IQ, O_IK, O_IW, O_GATE, D_IN) = [
    int(v) for v in _OFF]

N_A = 1408
N_C = 384
N_R = 1152
N_W = 128
N_G = 3072
N_IN = N_A + N_C + N_R + N_W + N_G
_ROPE_SPANS = ((0, 2, 0, DSA_DIM, DSA_DIM ** -0.5 * LOG2E), (256, 2, 0, DSA_DIM, 1.0), (512, 2, 1, IDX_DIM, 1.0),
               (768, 2, 1, IDX_DIM, 1.0), (1024, 1, 2, MLA_ROPE, 1.0))


def _rot_cols(base, n_heads, d):
    half = d // 2
    idx, sgn = [], []
    for h in range(n_heads):
        for j in range(d):
            if j < half:
                idx.append(base + h * d + j + half)
                sgn.append(-1.0)
            else:
                idx.append(base + h * d + j - half)
                sgn.append(1.0)
    return idx, sgn


def _in_layout():
    idx, sgn = [], []

    def plain(base, n):
        idx.extend(range(base, base + n))
        sgn.extend([1.0] * n)

    def pad(n):
        idx.extend([0] * n)
        sgn.extend([0.0] * n)

    plain(O_DV, 256); plain(O_SBQ, 384); plain(O_SBK, 384); plain(O_SBV, 384)
    plain(O_CQ, 256); plain(O_CKV, 128)
    plain(O_DQ, 256); plain(O_DK, 256); plain(O_IQ, 256)
    for _ in range(IDX_HEADS):
        plain(O_IK, IDX_DIM)
    pad(64); plain(O_KR, 32); pad(32)
    plain(O_IW, IDX_HEADS); pad(N_W - IDX_HEADS)
    plain(O_GATE, N_G)
    assert len(idx) == N_IN
    return np.asarray(idx, np.int32), np.asarray(sgn, np.float32)


_IN_IDX, _IN_SGN = _in_layout()


def _uq_layout():
    per = MLA_NOPE + MLA_ROPE
    idx, sgn = [], []
    for h in range(MLA_HEADS):
        idx.extend(range(h * per, h * per + per)); sgn.extend([1.0] * per)
        idx.extend([0] * 32); sgn.extend([0.0] * 32)
    for h in range(MLA_HEADS):
        idx.extend([0] * MLA_NOPE); sgn.extend([0.0] * MLA_NOPE)
        i, s = _rot_cols(h * per + MLA_NOPE, 1, MLA_ROPE)
        idx.extend(i); sgn.extend(s)
        idx.extend([0] * 32); sgn.extend([0.0] * 32)
    return np.asarray(idx, np.int32), np.asarray(sgn, np.float32)


def _ukv_layout():
    per = MLA_NOPE + MLA_V
    idx, sgn = [], []
    for h in range(MLA_HEADS):
        idx.extend(range(h * per, h * per + MLA_NOPE)); sgn.extend([1.0] * MLA_NOPE)
        idx.extend([0] * 64); sgn.extend([0.0] * 64)
    for h in range(MLA_HEADS):
        idx.extend(range(h * per + MLA_NOPE, h * per + per)); sgn.extend([1.0] * MLA_V)
    return np.asarray(idx, np.int32), np.asarray(sgn, np.float32)


_UQ_IDX, _UQ_SGN = _uq_layout()
_UKV_IDX, _UKV_SGN = _ukv_layout()
N_QM = MLA_HEADS * LANES
N_VM = MLA_HEADS * MLA_V


def _cparams(sem):
    return pltpu.CompilerParams(dimension_semantics=sem, vmem_limit_bytes=VMEM_LIMIT)


def _dot(a, b):
    return jnp.dot(a, b, preferred_element_type=F32)


def _dot_t(a, b):
    return lax.dot_general(a, b, (((1,), (1,)), ((), ())), preferred_element_type=F32)


def _rms(x):
    return x * lax.rsqrt(jnp.mean(x * x, axis=-1, keepdims=True) + NORM_EPS)


def _ada_kernel(c_ref, w_ref, b_ref, o_ref):
    c = c_ref[...]
    sc = c * (1.0 / (1.0 + jnp.exp(-c)))
    o_ref[...] = jnp.dot(sc, w_ref[...], preferred_element_type=F32,
                         precision=lax.Precision.HIGHEST) + b_ref[...]


def _ada_mod(c, ada_w, ada_b):
    depth, d, n = ada_w.shape
    b = c.shape[0]
    rows = 8
    cp = jnp.zeros((rows, d), F32).at[:b].set(c)
    tn = 2048
    out = pl.pallas_call(
        _ada_kernel,
        grid=(depth, n // tn),
        in_specs=[
            pl.BlockSpec((rows, d), lambda l, j: (0, 0)),
            pl.BlockSpec((None, d, tn), lambda l, j: (l, 0, j)),
            pl.BlockSpec((None, 1, tn), lambda l, j: (l, 0, j)),
        ],
        out_specs=pl.BlockSpec((None, rows, tn), lambda l, j: (l, 0, j)),
        out_shape=jax.ShapeDtypeStruct((depth, rows, n), F32),
        compiler_params=_cparams(("arbitrary", "arbitrary")),
        name="ada_mod",
    )(cp, ada_w, ada_b.reshape(depth, 1, n))
    return out[:, :b]


def _inproj_kernel(x_ref, sc_ref, sh_ref, w_ref, cos_ref, sin_ref,
                   a_ref, c_ref, r_ref, w8_ref, g_ref):
    h = (_rms(x_ref[...]) * sc_ref[...] + sh_ref[...]).astype(BF16)
    o = 0
    for c0 in range(0, N_A, 256):
        c1 = min(c0 + 256, N_A)
        a_ref[:, c0:c1] = _dot(h, w_ref[:, o + c0:o + c1]).astype(BF16)
    o += N_A
    c_ref[...] = _dot(h, w_ref[:, o:o + N_C])
    o += N_C
    for lo, n, kind, dim, scale in _ROPE_SPANS:
        y = _dot(h, w_ref[:, o + lo:o + lo + n * LANES])
        width, half = n * LANES, dim // 2
        first = (lax.broadcasted_iota(I32, (1, width), 1) & (dim - 1)) < half
        yr = jnp.where(first, -pltpu.roll(y, width - half, 1), pltpu.roll(y, half, 1))
        cs = jnp.concatenate([cos_ref[:, kind * LANES:(kind + 1) * LANES]] * n, axis=1)
        sn = jnp.concatenate([sin_ref[:, kind * LANES:(kind + 1) * LANES]] * n, axis=1)
        r = y * cs + yr * sn
        if scale != 1.0:
            r = r * scale
        r_ref[:, lo:lo + n * LANES] = r.astype(BF16)
    o += N_R
    w8_ref[...] = _dot(h, w_ref[:, o:o + N_W]) * (IDX_DIM ** -0.5 * IDX_HEADS ** -0.5)
    o += N_W
    for c0 in range(0, N_G, 512):
        z = _dot(h, w_ref[:, o + c0:o + c0 + 512])
        g_ref[:, c0:c0 + 512] = (1.0 / (1.0 + jnp.exp(-z))).astype(BF16)


def _inproj(x, scale, shift, w, cos_t, sin_t, layer, seq):
    t, d = x.shape
    tm = 512
    per = seq // tm
    return pl.pallas_call(
        _inproj_kernel,
        grid=(t // tm,),
        in_specs=[
            pl.BlockSpec((tm, d), lambda i: (i, 0)),
            pl.BlockSpec((None, None, 1, d), lambda i: (layer, i // per, 0, 0)),
            pl.BlockSpec((None, None, 1, d), lambda i: (layer, i // per, 0, 0)),
            pl.BlockSpec((None, d, N_IN), lambda i: (layer, 0, 0), pipeline_mode=pl.Buffered(1)),
            pl.BlockSpec((tm, 3 * LANES), lambda i: (i, 0)),
            pl.BlockSpec((tm, 3 * LANES), lambda i: (i, 0)),
        ],
        out_specs=[
            pl.BlockSpec((tm, N_A), lambda i: (i, 0)),
            pl.BlockSpec((tm, N_C), lambda i: (i, 0)),
            pl.BlockSpec((tm, N_R), lambda i: (i, 0)),
            pl.BlockSpec((tm, N_W), lambda i: (i, 0)),
            pl.BlockSpec((tm, N_G), lambda i: (i, 0)),
        ],
        out_shape=[
            jax.ShapeDtypeStruct((t, N_A), BF16),
            jax.ShapeDtypeStruct((t, N_C), F32),
            jax.ShapeDtypeStruct((t, N_R), BF16),
            jax.ShapeDtypeStruct((t, N_W), F32),
            jax.ShapeDtypeStruct((t, N_G), BF16),
        ],
        compiler_params=_cparams(("arbitrary",)),
        name="inproj",
    )(x, scale, shift, w, cos_t, sin_t)


def _mla_up_kernel(c_ref, gq_ref, gkv_ref, wq_ref, wkv_ref, cos_ref, sin_ref, kr_ref,
                   q_ref, k_ref, v_ref):
    c = c_ref[...]
    nq = (_rms(c[:, :MLA_Q_RANK]) * gq_ref[...]).astype(BF16)
    nkv = (_rms(c[:, MLA_Q_RANK:]) * gkv_ref[...]).astype(BF16)
    scale = (MLA_NOPE + MLA_ROPE) ** -0.5 * LOG2E
    cs = cos_ref[...] * scale
    sn = sin_ref[...] * scale
    kr = kr_ref[...].astype(F32)
    for h in range(MLA_HEADS):
        lo = h * LANES
        y = _dot(nq, wq_ref[:, lo:lo + LANES])
        yr = _dot(nq, wq_ref[:, N_QM + lo:N_QM + lo + LANES])
        q_ref[:, lo:lo + LANES] = (y * cs + yr * sn).astype(BF16)
        k_ref[:, lo:lo + LANES] = (_dot(nkv, wkv_ref[:, lo:lo + LANES]) + kr).astype(BF16)
    v_ref[...] = _dot(nkv, wkv_ref[:, N_QM:N_QM + N_VM]).astype(BF16)


def _mla_up(cl, gq, gkv, wq, wkv, cos_t, sin_t, rp, layer):
    t = cl.shape[0]
    tm = 512
    return pl.pallas_call(
        _mla_up_kernel,
        grid=(t // tm,),
        in_specs=[
            pl.BlockSpec((tm, N_C), lambda i: (i, 0)),
            pl.BlockSpec((None, 1, MLA_Q_RANK), lambda i: (layer, 0, 0)),
            pl.BlockSpec((None, 1, MLA_KV_RANK), lambda i: (layer, 0, 0)),
            pl.BlockSpec((None, MLA_Q_RANK, 2 * N_QM), lambda i: (layer, 0, 0)),
            pl.BlockSpec((None, MLA_KV_RANK, N_QM + N_VM), lambda i: (layer, 0, 0)),
            pl.BlockSpec((tm, LANES), lambda i: (i, 2)),
            pl.BlockSpec((tm, LANES), lambda i: (i, 2)),
            pl.BlockSpec((tm, LANES), lambda i: (i, 8)),
        ],
        out_specs=[
            pl.BlockSpec((tm, N_QM), lambda i: (i, 0)),
            pl.BlockSpec((tm, N_QM), lambda i: (i, 0)),
            pl.BlockSpec((tm, N_VM), lambda i: (i, 0)),
        ],
        out_shape=[
            jax.ShapeDtypeStruct((t, N_QM), BF16),
            jax.ShapeDtypeStruct((t, N_QM), BF16),
            jax.ShapeDtypeStruct((t, N_VM), BF16),
        ],
        compiler_params=_cparams(("arbitrary",)),
        name="mla_up",
    )(cl, gq, gkv, wq, wkv, cos_t, sin_t, rp)


def _sb_kernel(q_ref, k_ref, v_ref, o_ref, acc_ref, rem_ref, *, tq):
    qi = pl.program_id(2)
    lane = lax.broadcasted_iota(I32, (1, LANES), 1)
    row = lax.broadcasted_iota(I32, (tq, tq), 0)
    col = lax.broadcasted_iota(I32, (tq, tq), 1)
    causal = col < row
    tri = jnp.where(row > col, 1.0, 0.0).astype(BF16)
    q = q_ref[...]
    q_heads = []
    for h in range(2):
        head = (lane >= h * SB_DIM) & (lane < (h + 1) * SB_DIM)
        q_heads.append(jnp.where(head, q, jnp.zeros_like(q)) * jnp.asarray(SB_DIM ** -0.5, BF16))
    acc_ref[...] = jnp.zeros_like(acc_ref)
    rem_ref[...] = jnp.zeros_like(rem_ref)

    def block(j, diagonal, live=None):
        start = pl.multiple_of(j * tq, tq)
        kb = k_ref[pl.ds(start, tq), :]
        vb = v_ref[pl.ds(start, tq), :]
        for h in range(2):
            z = _dot_t(q_heads[h], kb)
            soft = jnp.log1p(jnp.exp(-jnp.abs(z)))
            log_stay = -(jnp.maximum(z, 0.0) + soft)
            if diagonal:
                log_stay = jnp.where(causal, log_stay, 0.0)
            if live is not None:
                log_stay = jnp.where(live, log_stay, 0.0)
            hi = log_stay.astype(BF16)
            lo = (log_stay - hi.astype(F32)).astype(BF16)
            later = _dot(hi, tri) + _dot(lo, tri)
            remain = rem_ref[h]
            log_a = (z + log_stay) + later + jnp.concatenate([remain] * (tq // LANES), axis=1)
            a = jnp.exp(log_a)
            if diagonal:
                a = jnp.where(causal, a, 0.0)
            if live is not None:
                a = jnp.where(live, a, 0.0)
            acc_ref[h] += _dot(a.astype(BF16), vb)
            rem_ref[h] = remain + jnp.sum(log_stay, axis=-1, keepdims=True)

    block(qi, True)
    block(jnp.maximum(qi - 1, 0), False, live=qi > 0)

    def cond(j):
        return jnp.logical_and(j >= 0, jnp.max(rem_ref[...]) > -SB_UNDERFLOW)

    def body(j):
        block(j, False)
        return j - 1

    lax.while_loop(cond, body, qi - 2)
    o_ref[...] = jnp.where(lane < SB_DIM, acc_ref[0], acc_ref[1]).astype(BF16)


def _sb_attention(a, batch, seq):
    t = a.shape[0]
    tq = 256
    nq = seq // tq
    pairs = SB_HEADS // 2
    stat = pltpu.VMEM((2, tq, LANES), F32)
    return pl.pallas_call(
        functools.partial(_sb_kernel, tq=tq),
        grid=(batch, pairs, nq),
        in_specs=[
            pl.BlockSpec((tq, LANES), lambda b, p, i: (b * nq + i, 2 + p)),
            pl.BlockSpec((seq, LANES), lambda b, p, i: (b, 2 + pairs + p)),
            pl.BlockSpec((seq, LANES), lambda b, p, i: (b, 2 + 2 * pairs + p)),
        ],
        out_specs=pl.BlockSpec((tq, LANES), lambda b, p, i: (b * nq + i, p)),
        out_shape=jax.ShapeDtypeStruct((t, SB_HEADS * SB_DIM), BF16),
        scratch_shapes=[stat, stat],
        compiler_params=_cparams(("arbitrary", "arbitrary", "arbitrary")),
        name="sb_attention",
    )(a, a, a)


def _softmax_step(s, vb, h, acc_ref, l_ref, m_ref):
    reps = s.shape[1] // LANES
    m_old = m_ref[h]
    m_new = jnp.maximum(m_old, jnp.max(s, axis=-1, keepdims=True))
    alpha = jnp.exp2(m_old - m_new)
    p = jnp.exp2(s - jnp.concatenate([m_new] * reps, axis=1))
    part = p[:, 0:LANES]
    for c in range(1, reps):
        part = part + p[:, c * LANES:(c + 1) * LANES]
    l_ref[h] = alpha * l_ref[h] + part
    acc_ref[h] = alpha * acc_ref[h] + _dot(p.astype(BF16), vb)
    m_ref[h] = m_new


def _softmax_init(acc_ref, l_ref, m_ref):
    acc_ref[...] = jnp.zeros_like(acc_ref)
    l_ref[...] = jnp.zeros_like(l_ref)
    m_ref[...] = jnp.full_like(m_ref, NEG_BIG)


def _softmax_out(h, acc_ref, l_ref):
    return acc_ref[h] / jnp.sum(l_ref[h], axis=-1, keepdims=True)


def _mla_kernel(q_ref, k_ref, v_ref, o_ref, acc_ref, l_ref, m_ref, *, tq, widths):
    qi = pl.program_id(2)
    lane = lax.broadcasted_iota(I32, (1, LANES), 1)
    row = lax.broadcasted_iota(I32, (tq, tq), 0)
    col = lax.broadcasted_iota(I32, (tq, tq), 1)
    visible = (col >> CHUNK_SHIFT) <= (row >> CHUNK_SHIFT)
    q = q_ref[...]
    _softmax_init(acc_ref, l_ref, m_ref)

    def step(start, tk, diagonal):
        kb = k_ref[pl.ds(start, tk), :]
        vb = v_ref[pl.ds(start, tk), :]
        for h in range(2):
            s = _dot_t(q[:, h * LANES:(h + 1) * LANES], kb[:, h * LANES:(h + 1) * LANES])
            if diagonal:
                s = jnp.where(visible, s, NEG_BIG)
            _softmax_step(s, vb, h, acc_ref, l_ref, m_ref)

    done = 0
    for width in widths:
        n_steps = (qi * tq - done) // width

        def body(j, c, width=width, done=done):
            step(pl.multiple_of(done + j * width, tq), width, False)
            return c

        lax.fori_loop(0, n_steps, body, 0)
        done = done + n_steps * width
    step(pl.multiple_of(qi * tq, tq), tq, True)
    o_ref[...] = jnp.where(lane < MLA_V, _softmax_out(0, acc_ref, l_ref),
                           _softmax_out(1, acc_ref, l_ref)).astype(BF16)


def _mla_attention(qm, km, vm, batch, seq):
    t = qm.shape[0]
    tq = min(512, seq)
    widths = tuple(w for w in (2048, 1024, 512) if tq <= w <= seq and w % tq == 0)
    nq = seq // tq
    pairs = MLA_HEADS // 2
    stat = pltpu.VMEM((2, tq, LANES), F32)
    return pl.pallas_call(
        functools.partial(_mla_kernel, tq=tq, widths=widths),
        grid=(batch, pairs, nq),
        in_specs=[
            pl.BlockSpec((tq, 2 * LANES), lambda b, p, i: (b * nq + i, p)),
            pl.BlockSpec((seq, 2 * LANES), lambda b, p, i: (b, p)),
            pl.BlockSpec((seq, LANES), lambda b, p, i: (b, p)),
        ],
        out_specs=pl.BlockSpec((tq, LANES), lambda b, p, i: (b * nq + i, p)),
        out_shape=jax.ShapeDtypeStruct((t, N_VM), BF16),
        scratch_shapes=[stat, stat, stat],
        compiler_params=_cparams(("arbitrary", "arbitrary", "arbitrary")),
        name="mla_attention",
    )(qm, km, vm)


SEARCH_BISECT_EVERY = 3
SEARCH_MAX_STEPS = 3 * 32 + 4


def _dsa_kernel(qd_ref, qx_ref, w_ref, kx_ref, kd_ref, vd_ref, o_ref,
                key_ref, qs_ref, wr_ref, top_ref, cand_ref, cnt_ref, cnt3_ref, acc_ref, l_ref, m_ref, *, tq, tk, n_select):
    it = pl.program_id(1)
    t0 = it * tq
    last = t0 // tk
    reps = tk // LANES
    lane = lax.broadcasted_iota(I32, (1, LANES), 1)
    lane2 = lax.broadcasted_iota(I32, (1, 2 * LANES), 1)
    rowid = t0 + lax.broadcasted_iota(I32, (tq, 1), 0)
    row_chunk = rowid >> CHUNK_SHIFT

    qx = qx_ref[...]
    w = w_ref[...]
    for h in range(IDX_HEADS):
        head = (lane2 >= h * IDX_DIM) & (lane2 < (h + 1) * IDX_DIM)
        qs_ref[h * tq:(h + 1) * tq, :] = jnp.where(head, qx, jnp.zeros_like(qx))
        wr_ref[h] = jnp.broadcast_to(w[:, h:h + 1], (tq, LANES))
    top_ref[...] = jnp.full_like(top_ref, INT_MIN)

    def score_block(j, diagonal):
        start = pl.multiple_of(j * tk, tk)
        d = _dot_t(qs_ref[...], kx_ref[pl.ds(start, tk), :])
        score = jnp.zeros((tq, tk), F32)
        for h in range(IDX_HEADS):
            wh = jnp.concatenate([wr_ref[h]] * reps, axis=1)
            score = score + wh * jnp.maximum(d[h * tq:(h + 1) * tq], 0.0)
        score = jnp.where(score == 0.0, 0.0, score)
        bits = pltpu.bitcast(score, I32)
        key = jnp.where(bits < 0, bits ^ jnp.int32(0x7FFFFFFF), bits)
        if diagonal:
            col_chunk = (start + lax.broadcasted_iota(I32, (1, tk), 1)) >> CHUNK_SHIFT
            key = jnp.where(col_chunk <= row_chunk, key, INT_MIN)
        key_ref[j] = key
        t1, t2 = top_ref[0], top_ref[1]
        for c in range(reps):
            x = key[:, c * LANES:(c + 1) * LANES]
            t2 = jnp.maximum(t2, jnp.minimum(t1, x))
            t1 = jnp.maximum(t1, x)
        top_ref[0] = t1
        top_ref[1] = t2

    def score_body(j, c):
        score_block(j, False)
        return c

    lax.fori_loop(0, last, score_body, 0)
    score_block(last, True)
    n_blocks = last + 1

    n_chunks = tq // LANES

    def to_lanes(rep):
        return jnp.concatenate([jnp.transpose(rep[c * LANES:(c + 1) * LANES, :])[0:1, :]
                                for c in range(n_chunks)], axis=1)

    def to_rows(row):
        return jnp.concatenate([jnp.transpose(jnp.broadcast_to(row[:, c * LANES:(c + 1) * LANES], (LANES, LANES)))
                                for c in range(n_chunks)], axis=0)

    def sweep(combine, start, finish):
        cnt_ref[...] = jnp.full_like(cnt_ref, start)

        def body(j, c):
            for half in range(n_chunks):
                rows = slice(half * LANES, (half + 1) * LANES)
                cb = jnp.concatenate([cand_ref[rows, :]] * reps, axis=1)
                cnt_ref[rows, :] = combine(cnt_ref[rows, :], key_ref[j, rows, :], cb)
            return c

        lax.fori_loop(0, n_blocks, body, 0)
        parts = cnt_ref[...]
        return jnp.concatenate([finish(jnp.transpose(parts[c * LANES:(c + 1) * LANES, :]))
                                for c in range(n_chunks)], axis=1)

    def add_ge(acc, keys, cb):
        ge = jnp.where(keys >= cb, 1, 0)
        part = ge[:, 0:LANES]
        for cc in range(1, reps):
            part = part + ge[:, cc * LANES:(cc + 1) * LANES]
        return acc + part

    def min_ge(acc, keys, cb):
        kept = jnp.where(keys >= cb, keys, jnp.int32(2 ** 31 - 1))
        part = kept[:, 0:LANES]
        for cc in range(1, reps):
            part = jnp.minimum(part, kept[:, cc * LANES:(cc + 1) * LANES])
        return jnp.minimum(acc, part)

    def count_prepared():
        return sweep(add_ge, 0, lambda x: jnp.sum(x, axis=0, keepdims=True))

    def count_with_signs(cand):
        cand_ref[...] = to_rows(cand)
        cnt3_ref[...] = jnp.zeros_like(cnt3_ref)

        def body(j, c):
            for half in range(n_chunks):
                rows = slice(half * LANES, (half + 1) * LANES)
                keys = key_ref[j, rows, :]
                cb = jnp.concatenate([cand_ref[rows, :]] * reps, axis=1)
                for slot, bound in enumerate((cb, 1, 0)):
                    cnt3_ref[slot, rows, :] = add_ge(cnt3_ref[slot, rows, :], keys, bound)
            return c

        lax.fori_loop(0, n_blocks, body, 0)
        return [jnp.concatenate([jnp.sum(jnp.transpose(cnt3_ref[slot, c * LANES:(c + 1) * LANES, :]), axis=0,
                                         keepdims=True) for c in range(n_chunks)], axis=1) for slot in range(3)]

    def count_ge(cand):
        cand_ref[...] = to_rows(cand)
        return count_prepared()

    def smallest_ge(cand):
        cand_ref[...] = to_rows(cand)
        return sweep(min_ge, 2 ** 31 - 1, lambda x: jnp.min(x, axis=0, keepdims=True))

    def key_value(k):
        return pltpu.bitcast(jnp.where(k < 0, k ^ jnp.int32(0x7FFFFFFF), k), F32)

    def value_key(v):
        bits = pltpu.bitcast(v, I32)
        return jnp.where(bits < 0, bits ^ jnp.int32(0x7FFFFFFF), bits)

    row_l = t0 + lax.broadcasted_iota(I32, (1, tq), 1)
    few = ((row_l >> CHUNK_SHIFT) + 1) * CHUNK <= n_select
    t2_min = jnp.broadcast_to(jnp.min(top_ref[1], axis=-1, keepdims=True), (tq, LANES))
    t1_max = jnp.broadcast_to(jnp.max(top_ref[0], axis=-1, keepdims=True), (tq, LANES))
    lo = to_lanes(t2_min)
    hi = to_lanes(t1_max) + 1
    c_lo, c_pos, c_nn = count_with_signs(lo)
    c_hi = jnp.zeros((1, tq), I32)
    positive = c_pos >= n_select
    negative = c_nn < n_select
    at_zero = jnp.logical_not(jnp.logical_or(positive, negative))
    raise_lo = jnp.logical_and(positive, lo < 1)
    lower_hi = jnp.logical_and(negative, hi > 0)
    lo, c_lo = jnp.where(raise_lo, 1, lo), jnp.where(raise_lo, c_pos, c_lo)
    hi, c_hi = jnp.where(lower_hi, 0, hi), jnp.where(lower_hi, c_nn, c_hi)
    lo, c_lo = jnp.where(at_zero, 0, lo), jnp.where(at_zero, c_nn, c_lo)
    hi, c_hi = jnp.where(at_zero, 1, hi), jnp.where(at_zero, c_pos, c_hi)

    def propose(step, lo, c_lo, hi, c_hi):
        active = jnp.logical_and(jnp.logical_not(few), jnp.logical_and(c_lo > n_select + 1, hi - lo > 1))
        a = jnp.log(c_lo.astype(F32))
        b = jnp.log(c_hi.astype(F32) + 0.5)
        frac = jnp.clip((a - math.log(n_select + 0.5)) / (a - b), 1.0 / 64, 63.0 / 64)
        v_lo, v_hi = key_value(lo), key_value(hi)
        guess = value_key(v_lo + frac * (v_hi - v_lo))
        middle = lo + lax.shift_right_logical(hi - lo, 1)
        cand = jnp.where(step % SEARCH_BISECT_EVERY == SEARCH_BISECT_EVERY - 1, middle, guess)
        cand = jnp.minimum(jnp.maximum(cand, lo + 1), hi - 1)
        cand_ref[...] = to_rows(cand)
        return cand, jnp.where(active, 1, 0), jnp.max(jnp.where(active, 1, 0))

    def search_cond(carry):
        return jnp.logical_and(carry[0] < SEARCH_MAX_STEPS, carry[1] > 0)

    def search_body(carry):
        step, _, cand, active, lo, c_lo, hi, c_hi = carry
        c = count_prepared()
        up = jnp.logical_and(active > 0, c >= n_select)
        down = jnp.logical_and(active > 0, c < n_select)
        lo, c_lo = jnp.where(up, cand, lo), jnp.where(up, c, c_lo)
        hi, c_hi = jnp.where(down, cand, hi), jnp.where(down, c, c_hi)
        cand, active, busy = propose(step + 1, lo, c_lo, hi, c_hi)
        return step + 1, busy, cand, active, lo, c_lo, hi, c_hi

    cand, active, busy = propose(jnp.int32(0), lo, c_lo, hi, c_hi)
    lo, c_lo, hi, c_hi = lax.while_loop(
        search_cond, search_body, (jnp.int32(0), busy, cand, active, lo, c_lo, hi, c_hi))[4:]

    over = jnp.logical_and(jnp.logical_not(few), jnp.logical_and(c_lo == n_select + 1, hi - lo > 1))
    least = smallest_ge(lo)
    c_drop = count_ge(jnp.where(over, least + 1, lo))
    dropped = jnp.logical_and(over, c_drop == n_select)
    theta = jnp.where(dropped, least + 1, jnp.where(over, least, lo))
    theta = jnp.where(few, INT_MIN, theta)
    c_above = jnp.where(over, c_drop, c_hi)
    tied = jnp.logical_and(jnp.logical_not(few), jnp.logical_and(c_lo > n_select, jnp.logical_not(dropped)))

    @pl.when(jnp.max(jnp.where(tied, 1, 0)) > 0)
    def _():
        theta_r = to_rows(theta)[:, 0:1]
        tied_r = to_rows(jnp.where(tied, 1, 0))[:, 0:1] > 0
        need = to_rows((n_select - c_above).astype(F32))[:, 0:1]
        r = lax.broadcasted_iota(I32, (tk, tk), 0)
        c = lax.broadcasted_iota(I32, (tk, tk), 1)
        before = jnp.where(r < c, 1.0, 0.0).astype(BF16)

        def body(j, seen):
            key = key_ref[j]
            eq = jnp.logical_and(key == theta_r, tied_r)
            eqf = jnp.where(eq, 1.0, 0.0)
            rank = seen + _dot(eqf.astype(BF16), before)
            key_ref[j] = jnp.where(jnp.logical_and(eq, rank >= need), INT_MIN, key)
            return seen + jnp.sum(eqf, axis=-1, keepdims=True)

        lax.fori_loop(0, n_blocks, body, jnp.zeros((tq, 1), F32))

    cand_ref[...] = to_rows(jnp.maximum(theta, INT_MIN + 1))

    qd = qd_ref[...]
    q_heads = []
    for h in range(DSA_HEADS):
        blk = qd[:, (h // 2) * LANES:(h // 2 + 1) * LANES]
        head = (lane >= (h % 2) * DSA_DIM) & (lane < (h % 2 + 1) * DSA_DIM)
        q_heads.append(jnp.where(head, blk, jnp.zeros_like(blk)))
    _softmax_init(acc_ref, l_ref, m_ref)

    def attend(j, c):
        start = pl.multiple_of(j * tk, tk)
        sel = key_ref[j] >= jnp.concatenate([cand_ref[...]] * reps, axis=1)
        kb = kd_ref[pl.ds(start, tk), :]
        vb = vd_ref[pl.ds(start, tk), :]
        for h in range(DSA_HEADS):
            p0 = (h // 2) * LANES
            s = jnp.where(sel, _dot_t(q_heads[h], kb[:, p0:p0 + LANES]), NEG_BIG)
            _softmax_step(s, vb[:, p0:p0 + LANES], h, acc_ref, l_ref, m_ref)
        return c

    lax.fori_loop(0, n_blocks, attend, 0)
    for p in range(DSA_HEADS // 2):
        o_ref[:, p * LANES:(p + 1) * LANES] = jnp.where(
            lane < DSA_DIM, _softmax_out(2 * p, acc_ref, l_ref), _softmax_out(2 * p + 1, acc_ref, l_ref)).astype(BF16)


def _dsa_attention(a, rp, w8, batch, seq, n_select):
    t = a.shape[0]
    tq = 256
    tk = min(1024, seq)
    nq = seq // tq
    width = DSA_HEADS * DSA_DIM
    keys = lambda col: pl.BlockSpec((seq, width), lambda b, i: (b, col), pipeline_mode=pl.Buffered(1))
    stat = lambda n: pltpu.VMEM((n, tq, LANES), F32)
    return pl.pallas_call(
        functools.partial(_dsa_kernel, tq=tq, tk=tk, n_select=n_select),
        grid=(batch, nq),
        in_specs=[
            pl.BlockSpec((tq, width), lambda b, i: (b * nq + i, 0)),
            pl.BlockSpec((tq, width), lambda b, i: (b * nq + i, 2)),
            pl.BlockSpec((tq, N_W), lambda b, i: (b * nq + i, 0)),
            keys(3),
            keys(1),
            keys(0),
        ],
        out_specs=pl.BlockSpec((tq, width), lambda b, i: (b * nq + i, 0)),
        out_shape=jax.ShapeDtypeStruct((t, width), BF16),
        scratch_shapes=[
            pltpu.VMEM((seq // tk, tq, tk), I32),
            pltpu.VMEM((IDX_HEADS * tq, width), BF16),
            stat(IDX_HEADS),
            pltpu.VMEM((2, tq, LANES), I32),
            pltpu.VMEM((tq, LANES), I32),
            pltpu.VMEM((tq, LANES), I32),
            pltpu.VMEM((3, tq, LANES), I32),
            stat(DSA_HEADS), stat(DSA_HEADS), stat(DSA_HEADS),
        ],
        compiler_params=_cparams(("arbitrary", "arbitrary")),
        name="dsa_attention",
    )(rp, rp, w8, rp, rp, a)


def _merge_kernel(x_ref, osb_ref, omla_ref, odsa_ref, g_ref, gm_ref, wsb_ref, wmla_ref, wdsa_ref, wout_ref,
                  o_ref):
    d = D_MODEL
    merged = (g_ref[:, 0:d].astype(F32) * _dot(osb_ref[...], wsb_ref[...])
              + g_ref[:, d:2 * d].astype(F32) * _dot(omla_ref[...], wmla_ref[...])
              + g_ref[:, 2 * d:3 * d].astype(F32) * _dot(odsa_ref[...], wdsa_ref[...]))
    o_ref[...] = x_ref[...] + gm_ref[...] * _dot(merged.astype(BF16), wout_ref[...])


def _merge(x, osb, omla, odsa, gates, gm, wsb, wmla, wdsa, wout, layer, seq):
    t, d = x.shape
    tm = 512
    per = seq // tm
    row = lambda i: (i, 0)
    wspec = lambda k: pl.BlockSpec((None, k, d), lambda i: (layer, 0, 0))
    return pl.pallas_call(
        _merge_kernel,
        grid=(t // tm,),
        in_specs=[
            pl.BlockSpec((tm, d), row),
            pl.BlockSpec((tm, osb.shape[1]), row),
            pl.BlockSpec((tm, omla.shape[1]), row),
            pl.BlockSpec((tm, odsa.shape[1]), row),
            pl.BlockSpec((tm, N_G), row),
            pl.BlockSpec((None, None, 1, d), lambda i: (layer, i // per, 0, 0)),
            wspec(osb.shape[1]), wspec(omla.shape[1]), wspec(odsa.shape[1]), wspec(d),
        ],
        out_specs=pl.BlockSpec((tm, d), row),
        out_shape=jax.ShapeDtypeStruct((t, d), F32),
        compiler_params=_cparams(("arbitrary",)),
        name="merge",
    )(x, osb, omla, odsa, gates, gm, wsb, wmla, wdsa, wout)


def _router_kernel(x_ref, sc_ref, sh_ref, rw_ref, rb_ref, h_ref, ti_ref, tw_ref, tr_ref, cnt_ref, run_ref, *, tm):
    @pl.when(pl.program_id(0) == 0)
    def _():
        run_ref[...] = jnp.zeros_like(run_ref)

    h = _rms(x_ref[...]) * sc_ref[...] + sh_ref[...]
    h_ref[...] = h
    logits = jnp.dot(h, rw_ref[...], preferred_element_type=F32, precision=lax.Precision.HIGHEST) + rb_ref[...]
    lane = lax.broadcasted_iota(I32, (tm, LANES), 1)
    work = logits
    vals, hots = [], []
    for _ in range(TOP_K):
        m = jnp.max(work, axis=-1, keepdims=True)
        first = jnp.min(jnp.where(work == m, lane, LANES), axis=-1, keepdims=True)
        hot = lane == first
        vals.append(m)
        hots.append(hot)
        work = jnp.where(hot, -jnp.inf, work)
    exps = [jnp.exp(v - vals[0]) for v in vals]
    denom = exps[0] + exps[1] + exps[2] + exps[3]
    chosen = jnp.zeros((tm, LANES), F32)
    for hot in hots:
        chosen = chosen + jnp.where(hot, 1.0, 0.0)
    r = lax.broadcasted_iota(I32, (tm, tm), 0)
    c = lax.broadcasted_iota(I32, (tm, tm), 1)
    earlier = jnp.where(c < r, 1.0, 0.0).astype(BF16)
    rank_all = _dot(earlier, chosen.astype(BF16)) + run_ref[...]
    ti = jnp.zeros((tm, LANES), I32)
    tw = jnp.zeros((tm, LANES), F32)
    tr = jnp.zeros((tm, LANES), I32)
    for k in range(TOP_K):
        e_k = jnp.sum(jnp.where(hots[k], lane, 0), axis=-1, keepdims=True)
        r_k = jnp.sum(jnp.where(hots[k], rank_all, 0.0), axis=-1, keepdims=True).astype(I32)
        ti = jnp.where(lane == k, e_k, ti)
        tw = jnp.where(lane == k, exps[k] / denom, tw)
        tr = jnp.where(lane == k, r_k, tr)
    ti_ref[...] = ti
    tw_ref[...] = tw
    tr_ref[...] = tr
    run_ref[...] = run_ref[...] + jnp.sum(chosen, axis=0, keepdims=True)
    cnt_ref[...] = run_ref[...]


def _router(x, scale, shift, rw, rb, layer, seq):
    t, d = x.shape
    tm = 256
    per = seq // tm
    row = lambda i: (i, 0)
    return pl.pallas_call(
        functools.partial(_router_kernel, tm=tm),
        grid=(t // tm,),
        in_specs=[
            pl.BlockSpec((tm, d), row),
            pl.BlockSpec((None, None, 1, d), lambda i: (layer, i // per, 0, 0)),
            pl.BlockSpec((None, None, 1, d), lambda i: (layer, i // per, 0, 0)),
            pl.BlockSpec((None, d, LANES), lambda i: (layer, 0, 0)),
            pl.BlockSpec((None, 1, LANES), lambda i: (layer, 0, 0)),
        ],
        out_specs=[
            pl.BlockSpec((tm, d), row),
            pl.BlockSpec((tm, LANES), row),
            pl.BlockSpec((tm, LANES), row),
            pl.BlockSpec((tm, LANES), row),
            pl.BlockSpec((1, LANES), lambda i: (0, 0)),
        ],
        out_shape=[
            jax.ShapeDtypeStruct((t, d), F32),
            jax.ShapeDtypeStruct((t, LANES), I32),
            jax.ShapeDtypeStruct((t, LANES), F32),
            jax.ShapeDtypeStruct((t, LANES), I32),
            jax.ShapeDtypeStruct((1, LANES), F32),
        ],
        scratch_shapes=[pltpu.VMEM((1, LANES), F32)],
        compiler_params=_cparams(("arbitrary",)),
        name="router",
    )(x, scale, shift, rw, rb)


def _expert_kernel(te_ref, tf_ref, nv_ref, x_ref, wgu_ref, bgu_ref, wd_ref, bd_ref, o_ref,
                   wgu_bf, wd_bf):
    i = pl.program_id(0)

    @pl.when(i >= nv_ref[0])
    def _():
        o_ref[...] = jnp.zeros_like(o_ref)

    @pl.when(i < nv_ref[0])
    def _():
        @pl.when(tf_ref[i] == 1)
        def _():
            wgu_bf[...] = wgu_ref[...].astype(BF16)
            wd_bf[...] = wd_ref[...].astype(BF16)

        gu = _dot(x_ref[...].astype(BF16), wgu_bf[...]) + bgu_ref[...]
        gate = jnp.minimum(gu[:, :D_EXPERT], SWIGLU_LIMIT)
        up = jnp.clip(gu[:, D_EXPERT:], -SWIGLU_LIMIT, SWIGLU_LIMIT)
        act = (up + 1.0) * (gate * (1.0 / (1.0 + jnp.exp(-SWIGLU_ALPHA * gate))))
        o_ref[...] = _dot(act.astype(BF16), wd_bf[...]) + bd_ref[...]


def _experts(xs, tile_e, tile_first, n_valid, wgu, bgu, wd, bd, layer, tm):
    p, d = xs.shape
    n_tiles = p // tm
    grid_spec = pltpu.PrefetchScalarGridSpec(
        num_scalar_prefetch=3,
        grid=(n_tiles,),
        in_specs=[
            pl.BlockSpec((tm, d), lambda i, te, tf, nv: (i, 0)),
            pl.BlockSpec((None, None, d, 2 * D_EXPERT), lambda i, te, tf, nv: (layer, te[i], 0, 0)),
            pl.BlockSpec((None, None, 1, 2 * D_EXPERT), lambda i, te, tf, nv: (layer, te[i], 0, 0)),
            pl.BlockSpec((None, None, D_EXPERT, d), lambda i, te, tf, nv: (layer, te[i], 0, 0)),
            pl.BlockSpec((None, None, 1, d), lambda i, te, tf, nv: (layer, te[i], 0, 0)),
        ],
        out_specs=pl.BlockSpec((tm, d), lambda i, te, tf, nv: (i, 0)),
        scratch_shapes=[pltpu.VMEM((d, 2 * D_EXPERT), BF16), pltpu.VMEM((D_EXPERT, d), BF16)],
    )
    return pl.pallas_call(
        _expert_kernel,
        grid_spec=grid_spec,
        out_shape=jax.ShapeDtypeStruct((p, d), F32),
        compiler_params=_cparams(("arbitrary",)),
        name="experts",
    )(tile_e, tile_first, n_valid, xs, wgu, bgu, wd, bd)


def _combine_kernel(x_ref, y0_ref, y1_ref, y2_ref, y3_ref, tw_ref, g_ref, fg_ref, o_ref, *, final):
    tw = tw_ref[...]
    y = tw[:, 0:1] * y0_ref[...]
    for k, y_ref in enumerate((y1_ref, y2_ref, y3_ref), start=1):
        y = y + tw[:, k:k + 1] * y_ref[...]
    x = x_ref[...] + g_ref[...] * y
    if final:
        x = _rms(x) * fg_ref[...]
    o_ref[...] = x


def _combine(x, yg, tw, gf, final_g, layer, seq, final):
    t, d = x.shape
    tm = 256
    nt = t // tm
    per = seq // tm
    slot = lambda k: pl.BlockSpec((tm, d), lambda i: (k * nt + i, 0))
    return pl.pallas_call(
        functools.partial(_combine_kernel, final=final),
        grid=(nt,),
        in_specs=[
            pl.BlockSpec((tm, d), lambda i: (i, 0)),
            slot(0), slot(1), slot(2), slot(3),
            pl.BlockSpec((tm, LANES), lambda i: (i, 0)),
            pl.BlockSpec((None, None, 1, d), lambda i: (layer, i // per, 0, 0)),
            pl.BlockSpec((1, d), lambda i: (0, 0)),
        ],
        out_specs=pl.BlockSpec((tm, d), lambda i: (i, 0)),
        out_shape=jax.ShapeDtypeStruct((t, d), F32),
        compiler_params=_cparams(("arbitrary",)),
        name="combine",
    )(x, yg, yg, yg, yg, tw, gf, final_g)


def _rope_tables(positions):
    pos = positions.reshape(-1).astype(F32)

    def cs(dim):
        inv_freq = ROPE_THETA ** (-jnp.arange(0, dim, 2, dtype=F32) / dim)
        ang = pos[:, None] * inv_freq
        return jnp.cos(ang), jnp.sin(ang)

    c_d, s_d = cs(DSA_DIM)
    c_i, s_i = cs(IDX_DIM)
    c_m, s_m = cs(MLA_ROPE)
    t = pos.shape[0]
    ones, zeros = jnp.ones((t, 64), F32), jnp.zeros((t, 64), F32)
    cos_t = jnp.concatenate([jnp.tile(c_d, (1, 4)), jnp.tile(c_i, (1, 8)),
                             ones, c_m, c_m, ones[:, :32]], axis=1)
    sin_t = jnp.concatenate([jnp.tile(s_d, (1, 4)), jnp.tile(s_i, (1, 8)),
                             zeros, s_m, s_m, zeros[:, :32]], axis=1)
    return cos_t, sin_t


def _permute_cols(w, idx, sgn):
    return (w.at[..., jnp.asarray(idx)].get(mode="promise_in_bounds") * jnp.asarray(sgn)).astype(BF16)


def _dispatch_tables(ti, tr, counts, tm):
    t = ti.shape[0]
    p = t * TOP_K + N_EXPERTS * tm
    n_tiles = p // tm
    cnt = counts.astype(I32)
    padded = ((cnt + tm - 1) // tm) * tm
    ends = jnp.cumsum(padded)
    starts = ends - padded
    pos = starts[ti] + tr
    tile_start = jnp.arange(n_tiles, dtype=I32) * tm
    n_valid = ends[-1] // tm
    tile_e = jnp.sum((ends[None, :] <= tile_start[:, None]).astype(I32), axis=1)
    tile_e = jnp.minimum(tile_e, N_EXPERTS - 1)
    last_e = jnp.max(jnp.where(tile_start < ends[-1], tile_e, 0))
    tile_e = jnp.where(tile_start < ends[-1], tile_e, last_e)
    tok = jnp.repeat(jnp.arange(t, dtype=I32), TOP_K)
    packed = jnp.concatenate([lax.sort_key_val(pos.reshape(-1), tok)[1], jnp.zeros((tm,), I32)])
    first = jnp.cumsum(cnt) - cnt
    tile_off = jnp.clip(first[tile_e] + tile_start - starts[tile_e], 0, t * TOP_K)
    row_src = jax.vmap(lambda o: lax.dynamic_slice(packed, (o,), (tm,)))(tile_off).reshape(p)
    tile_first = jnp.concatenate([jnp.ones((1,), I32), (tile_e[1:] != tile_e[:-1]).astype(I32)])
    return pos, row_src, tile_e, tile_first, n_valid.astype(I32).reshape(1)


def kernel(x, c, positions, ada_w, ada_b, norm_mix_g, w_in, mla_q_norm_g, mla_kv_norm_g, mla_w_uq, mla_w_ukv,
           w_sb_out, w_mla_out, w_dsa_out, w_out, norm_ffn_g, router_w, router_b, expert_w_gu, expert_b_gu,
           expert_w_down, expert_b_down, final_norm_g):
    batch, seq, d = x.shape
    depth = ada_w.shape[0]
    t = batch * seq
    n_select = min(DSA_TOPK_MAX, seq // 4)
    tm_e = 256

    mod = _ada_mod(c, ada_w, ada_b)
    sh_m, sc_m, g_m, sh_f, sc_f, g_f = [m[:, :, None, :] for m in jnp.split(mod, 6, axis=-1)]
    scale_m = norm_mix_g[:, None, None, :] * (1.0 + sc_m)
    scale_f = norm_ffn_g[:, None, None, :] * (1.0 + sc_f)

    cos_t, sin_t = _rope_tables(positions)
    w_in_p = _permute_cols(w_in, _IN_IDX, _IN_SGN)
    w_uq_p = _permute_cols(mla_w_uq, _UQ_IDX, _UQ_SGN)
    w_ukv_p = _permute_cols(mla_w_ukv, _UKV_IDX, _UKV_SGN)
    w_sb_b, w_mla_b, w_dsa_b, w_out_b = [w.astype(BF16) for w in (w_sb_out, w_mla_out, w_dsa_out, w_out)]
    gq = mla_q_norm_g[:, None, :]
    gkv = mla_kv_norm_g[:, None, :]
    rw_p = jnp.zeros((depth, d, LANES), F32).at[:, :, :N_EXPERTS].set(router_w)
    rb_p = jnp.full((depth, 1, LANES), -jnp.inf, F32).at[:, 0, :N_EXPERTS].set(router_b)
    bgu = expert_b_gu[:, :, None, :]
    bd = expert_b_down[:, :, None, :]
    fg = final_norm_g[None, :]

    xf = x.reshape(t, d)
    for l in range(depth):
        a, cl, rp, w8, gates = _inproj(xf, scale_m, sh_m, w_in_p, cos_t, sin_t, l, seq)
        o_sb = _sb_attention(a, batch, seq)
        qm, km, vm = _mla_up(cl, gq, gkv, w_uq_p, w_ukv_p, cos_t, sin_t, rp, l)
        o_mla = _mla_attention(qm, km, vm, batch, seq)
        o_dsa = _dsa_attention(a, rp, w8, batch, seq, n_select)
        xf = _merge(xf, o_sb, o_mla, o_dsa, gates, g_m, w_sb_b, w_mla_b, w_dsa_b, w_out_b, l, seq)

        hf, ti, tw, tr, counts = _router(xf, scale_f, sh_f, rw_p, rb_p, l, seq)
        pos, row_src, tile_e, tile_first, n_valid = _dispatch_tables(
            ti[:, :TOP_K], tr[:, :TOP_K], counts[0, :N_EXPERTS], tm_e)
        xs = hf.at[row_src].get(mode="promise_in_bounds")
        ys = _experts(xs, tile_e, tile_first, n_valid, expert_w_gu, bgu, expert_w_down, bd, l, tm_e)
        yg = ys.at[pos.T.reshape(-1)].get(mode="promise_in_bounds")
        xf = _combine(xf, yg, tw, g_f, fg, l, seq, l == depth - 1)
    return xf.reshape(batch, seq, d)
```

```python
import functools
import math

import numpy as np
import jax
import jax.numpy as jnp
from jax import lax
from jax.experimental import pallas as pl
from jax.experimental.pallas import tpu as pltpu

F32 = jnp.float32
BF16 = jnp.bfloat16
I32 = jnp.int32

D_MODEL = 1024
CHUNK = 64
CHUNK_SHIFT = 6
ROPE_THETA = 10000.0
NORM_EPS = 1e-6
SB_HEADS, SB_DIM = 6, 64
MLA_HEADS, MLA_NOPE, MLA_ROPE, MLA_V = 6, 64, 32, 64
MLA_Q_RANK, MLA_KV_RANK = 256, 128
DSA_HEADS, DSA_DIM = 4, 64
IDX_HEADS, IDX_DIM = 8, 32
DSA_TOPK_MAX = 256
N_EXPERTS, TOP_K = 32, 4
D_EXPERT = D_MODEL
SWIGLU_LIMIT = 7.0
SWIGLU_ALPHA = 1.702

LANES = 128
VMEM_LIMIT = 56 * 1024 * 1024

INT_MIN = -(2 ** 31)
NEG_BIG = -1e30
SB_UNDERFLOW = 104.0
LOG2E = math.log2(math.e)

_SPLIT = (384, 384, 384, 256, 128, 32, 256, 256, 256, 256, 32, 8, 3072)
_OFF = np.concatenate([[0], np.cumsum(_SPLIT)]).astype(np.int64)
(O_SBQ, O_SBK, O_SBV, O_CQ, O_CKV, O_KR, O_DQ, O_DK, O_DV, O_IQ, O_IK, O_IW, O_GATE, D_IN) = [
    int(v) for v in _OFF]

N_A = 1408
N_C = 384
N_R = 1152
N_W = 128
N_G = 3072
N_IN = N_A + N_C + N_R + N_W + N_G
_ROPE_SPANS = ((0, 2, 0, DSA_DIM, DSA_DIM ** -0.5 * LOG2E), (256, 2, 0, DSA_DIM, 1.0), (512, 2, 1, IDX_DIM, 1.0),
               (768, 2, 1, IDX_DIM, 1.0), (1024, 1, 2, MLA_ROPE, 1.0))


def _rot_cols(base, n_heads, d):
    half = d // 2
    idx, sgn = [], []
    for h in range(n_heads):
        for j in range(d):
            if j < half:
                idx.append(base + h * d + j + half)
                sgn.append(-1.0)
            else:
                idx.append(base + h * d + j - half)
                sgn.append(1.0)
    return idx, sgn


def _in_layout():
    idx, sgn = [], []

    def plain(base, n):
        idx.extend(range(base, base + n))
        sgn.extend([1.0] * n)

    def pad(n):
        idx.extend([0] * n)
        sgn.extend([0.0] * n)

    plain(O_DV, 256); plain(O_SBQ, 384); plain(O_SBK, 384); plain(O_SBV, 384)
    plain(O_CQ, 256); plain(O_CKV, 128)
    plain(O_DQ, 256); plain(O_DK, 256); plain(O_IQ, 256)
    for _ in range(IDX_HEADS):
        plain(O_IK, IDX_DIM)
    pad(64); plain(O_KR, 32); pad(32)
    plain(O_IW, IDX_HEADS); pad(N_W - IDX_HEADS)
    plain(O_GATE, N_G)
    assert len(idx) == N_IN
    return np.asarray(idx, np.int32), np.asarray(sgn, np.float32)


_IN_IDX, _IN_SGN = _in_layout()


def _uq_layout():
    per = MLA_NOPE + MLA_ROPE
    idx, sgn = [], []
    for h in range(MLA_HEADS):
        idx.extend(range(h * per, h * per + per)); sgn.extend([1.0] * per)
        idx.extend([0] * 32); sgn.extend([0.0] * 32)
    for h in range(MLA_HEADS):
        idx.extend([0] * MLA_NOPE); sgn.extend([0.0] * MLA_NOPE)
        i, s = _rot_cols(h * per + MLA_NOPE, 1, MLA_ROPE)
        idx.extend(i); sgn.extend(s)
        idx.extend([0] * 32); sgn.extend([0.0] * 32)
    return np.asarray(idx, np.int32), np.asarray(sgn, np.float32)


def _ukv_layout():
    per = MLA_NOPE + MLA_V
    idx, sgn = [], []
    for h in range(MLA_HEADS):
        idx.extend(range(h * per, h * per + MLA_NOPE)); sgn.extend([1.0] * MLA_NOPE)
        idx.extend([0] * 64); sgn.extend([0.0] * 64)
    for h in range(MLA_HEADS):
        idx.extend(range(h * per + MLA_NOPE, h * per + per)); sgn.extend([1.0] * MLA_V)
    return np.asarray(idx, np.int32), np.asarray(sgn, np.float32)


_UQ_IDX, _UQ_SGN = _uq_layout()
_UKV_IDX, _UKV_SGN = _ukv_layout()
N_QM = MLA_HEADS * LANES
N_VM = MLA_HEADS * MLA_V


def _cparams(sem):
    return pltpu.CompilerParams(dimension_semantics=sem, vmem_limit_bytes=VMEM_LIMIT)


def _dot(a, b):
    return jnp.dot(a, b, preferred_element_type=F32)


def _dot_t(a, b):
    return lax.dot_general(a, b, (((1,), (1,)), ((), ())), preferred_element_type=F32)


def _rms(x):
    return x * lax.rsqrt(jnp.mean(x * x, axis=-1, keepdims=True) + NORM_EPS)


def _ada_kernel(c_ref, w_ref, b_ref, o_ref):
    c = c_ref[...]
    sc = c * (1.0 / (1.0 + jnp.exp(-c)))
    o_ref[...] = jnp.dot(sc, w_ref[...], preferred_element_type=F32,
                         precision=lax.Precision.HIGHEST) + b_ref[...]


def _ada_mod(c, ada_w, ada_b):
    depth, d, n = ada_w.shape
    b = c.shape[0]
    rows = 8
    cp = jnp.zeros((rows, d), F32).at[:b].set(c)
    tn = 2048
    out = pl.pallas_call(
        _ada_kernel,
        grid=(depth, n // tn),
        in_specs=[
            pl.BlockSpec((rows, d), lambda l, j: (0, 0)),
            pl.BlockSpec((None, d, tn), lambda l, j: (l, 0, j)),
            pl.BlockSpec((None, 1, tn), lambda l, j: (l, 0, j)),
        ],
        out_specs=pl.BlockSpec((None, rows, tn), lambda l, j: (l, 0, j)),
        out_shape=jax.ShapeDtypeStruct((depth, rows, n), F32),
        compiler_params=_cparams(("arbitrary", "arbitrary")),
        name="ada_mod",
    )(cp, ada_w, ada_b.reshape(depth, 1, n))
    return out[:, :b]


def _inproj_kernel(x_ref, sc_ref, sh_ref, w_ref, cos_ref, sin_ref,
                   a_ref, c_ref, r_ref, w8_ref, g_ref):
    h = (_rms(x_ref[...]) * sc_ref[...] + sh_ref[...]).astype(BF16)
    o = 0
    for c0 in range(0, N_A, 256):
        c1 = min(c0 + 256, N_A)
        a_ref[:, c0:c1] = _dot(h, w_ref[:, o + c0:o + c1]).astype(BF16)
    o += N_A
    c_ref[...] = _dot(h, w_ref[:, o:o + N_C])
    o += N_C
    for lo, n, kind, dim, scale in _ROPE_SPANS:
        y = _dot(h, w_ref[:, o + lo:o + lo + n * LANES])
        width, half = n * LANES, dim // 2
        first = (lax.broadcasted_iota(I32, (1, width), 1) & (dim - 1)) < half
        yr = jnp.where(first, -pltpu.roll(y, width - half, 1), pltpu.roll(y, half, 1))
        cs = jnp.concatenate([cos_ref[:, kind * LANES:(kind + 1) * LANES]] * n, axis=1)
        sn = jnp.concatenate([sin_ref[:, kind * LANES:(kind + 1) * LANES]] * n, axis=1)
        r = y * cs + yr * sn
        if scale != 1.0:
            r = r * scale
        r_ref[:, lo:lo + n * LANES] = r.astype(BF16)
    o += N_R
    w8_ref[...] = _dot(h, w_ref[:, o:o + N_W]) * (IDX_DIM ** -0.5 * IDX_HEADS ** -0.5)
    o += N_W
    for c0 in range(0, N_G, 512):
        z = _dot(h, w_ref[:, o + c0:o + c0 + 512])
        g_ref[:, c0:c0 + 512] = (1.0 / (1.0 + jnp.exp(-z))).astype(BF16)


def _inproj(x, scale, shift, w, cos_t, sin_t, layer, seq):
    t, d = x.shape
    tm = 512
    per = seq // tm
    return pl.pallas_call(
        _inproj_kernel,
        grid=(t // tm,),
        in_specs=[
            pl.BlockSpec((tm, d), lambda i: (i, 0)),
            pl.BlockSpec((None, None, 1, d), lambda i: (layer, i // per, 0, 0)),
            pl.BlockSpec((None, None, 1, d), lambda i: (layer, i // per, 0, 0)),
            pl.BlockSpec((None, d, N_IN), lambda i: (layer, 0, 0), pipeline_mode=pl.Buffered(1)),
            pl.BlockSpec((tm, 3 * LANES), lambda i: (i, 0)),
            pl.BlockSpec((tm, 3 * LANES), lambda i: (i, 0)),
        ],
        out_specs=[
            pl.BlockSpec((tm, N_A), lambda i: (i, 0)),
            pl.BlockSpec((tm, N_C), lambda i: (i, 0)),
            pl.BlockSpec((tm, N_R), lambda i: (i, 0)),
            pl.BlockSpec((tm, N_W), lambda i: (i, 0)),
            pl.BlockSpec((tm, N_G), lambda i: (i, 0)),
        ],
        out_shape=[
            jax.ShapeDtypeStruct((t, N_A), BF16),
            jax.ShapeDtypeStruct((t, N_C), F32),
            jax.ShapeDtypeStruct((t, N_R), BF16),
            jax.ShapeDtypeStruct((t, N_W), F32),
            jax.ShapeDtypeStruct((t, N_G), BF16),
        ],
        compiler_params=_cparams(("arbitrary",)),
        name="inproj",
    )(x, scale, shift, w, cos_t, sin_t)


def _mla_up_kernel(c_ref, gq_ref, gkv_ref, wq_ref, wkv_ref, cos_ref, sin_ref, kr_ref,
                   q_ref, k_ref, v_ref):
    c = c_ref[...]
    nq = (_rms(c[:, :MLA_Q_RANK]) * gq_ref[...]).astype(BF16)
    nkv = (_rms(c[:, MLA_Q_RANK:]) * gkv_ref[...]).astype(BF16)
    scale = (MLA_NOPE + MLA_ROPE) ** -0.5 * LOG2E
    cs = cos_ref[...] * scale
    sn = sin_ref[...] * scale
    kr = kr_ref[...].astype(F32)
    for h in range(MLA_HEADS):
        lo = h * LANES
        y = _dot(nq, wq_ref[:, lo:lo + LANES])
        yr = _dot(nq, wq_ref[:, N_QM + lo:N_QM + lo + LANES])
        q_ref[:, lo:lo + LANES] = (y * cs + yr * sn).astype(BF16)
        k_ref[:, lo:lo + LANES] = (_dot(nkv, wkv_ref[:, lo:lo + LANES]) + kr).astype(BF16)
    v_ref[...] = _dot(nkv, wkv_ref[:, N_QM:N_QM + N_VM]).astype(BF16)


def _mla_up(cl, gq, gkv, wq, wkv, cos_t, sin_t, rp, layer):
    t = cl.shape[0]
    tm = 512
    return pl.pallas_call(
        _mla_up_kernel,
        grid=(t // tm,),
        in_specs=[
            pl.BlockSpec((tm, N_C), lambda i: (i, 0)),
            pl.BlockSpec((None, 1, MLA_Q_RANK), lambda i: (layer, 0, 0)),
            pl.BlockSpec((None, 1, MLA_KV_RANK), lambda i: (layer, 0, 0)),
            pl.BlockSpec((None, MLA_Q_RANK, 2 * N_QM), lambda i: (layer, 0, 0)),
            pl.BlockSpec((None, MLA_KV_RANK, N_QM + N_VM), lambda i: (layer, 0, 0)),
            pl.BlockSpec((tm, LANES), lambda i: (i, 2)),
            pl.BlockSpec((tm, LANES), lambda i: (i, 2)),
            pl.BlockSpec((tm, LANES), lambda i: (i, 8)),
        ],
        out_specs=[
            pl.BlockSpec((tm, N_QM), lambda i: (i, 0)),
            pl.BlockSpec((tm, N_QM), lambda i: (i, 0)),
            pl.BlockSpec((tm, N_VM), lambda i: (i, 0)),
        ],
        out_shape=[
            jax.ShapeDtypeStruct((t, N_QM), BF16),
            jax.ShapeDtypeStruct((t, N_QM), BF16),
            jax.ShapeDtypeStruct((t, N_VM), BF16),
        ],
        compiler_params=_cparams(("arbitrary",)),
        name="mla_up",
    )(cl, gq, gkv, wq, wkv, cos_t, sin_t, rp)


def _sb_kernel(q_ref, k_ref, v_ref, o_ref, acc_ref, rem_ref, *, tq):
    qi = pl.program_id(2)
    lane = lax.broadcasted_iota(I32, (1, LANES), 1)
    row = lax.broadcasted_iota(I32, (tq, tq), 0)
    col = lax.broadcasted_iota(I32, (tq, tq), 1)
    causal = col < row
    tri = jnp.where(row > col, 1.0, 0.0).astype(BF16)
    q = q_ref[...]
    q_heads = []
    for h in range(2):
        head = (lane >= h * SB_DIM) & (lane < (h + 1) * SB_DIM)
        q_heads.append(jnp.where(head, q, jnp.zeros_like(q)) * jnp.asarray(SB_DIM ** -0.5, BF16))
    acc_ref[...] = jnp.zeros_like(acc_ref)
    rem_ref[...] = jnp.zeros_like(rem_ref)

    def block(j, diagonal, live=None):
        start = pl.multiple_of(j * tq, tq)
        kb = k_ref[pl.ds(start, tq), :]
        vb = v_ref[pl.ds(start, tq), :]
        for h in range(2):
            z = _dot_t(q_heads[h], kb)
            soft = jnp.log1p(jnp.exp(-jnp.abs(z)))
            log_stay = -(jnp.maximum(z, 0.0) + soft)
            if diagonal:
                log_stay = jnp.where(causal, log_stay, 0.0)
            if live is not None:
                log_stay = jnp.where(live, log_stay, 0.0)
            hi = log_stay.astype(BF16)
            lo = (log_stay - hi.astype(F32)).astype(BF16)
            later = _dot(hi, tri) + _dot(lo, tri)
            remain = rem_ref[h]
            log_a = (z + log_stay) + later + jnp.concatenate([remain] * (tq // LANES), axis=1)
            a = jnp.exp(log_a)
            if diagonal:
                a = jnp.where(causal, a, 0.0)
            if live is not None:
                a = jnp.where(live, a, 0.0)
            acc_ref[h] += _dot(a.astype(BF16), vb)
            rem_ref[h] = remain + jnp.sum(log_stay, axis=-1, keepdims=True)

    block(qi, True)
    block(jnp.maximum(qi - 1, 0), False, live=qi > 0)

    def cond(j):
        return jnp.logical_and(j >= 0, jnp.max(rem_ref[...]) > -SB_UNDERFLOW)

    def body(j):
        block(j, False)
        return j - 1

    lax.while_loop(cond, body, qi - 2)
    o_ref[...] = jnp.where(lane < SB_DIM, acc_ref[0], acc_ref[1]).astype(BF16)


def _sb_attention(a, batch, seq):
    t = a.shape[0]
    tq = 256
    nq = seq // tq
    pairs = SB_HEADS // 2
    stat = pltpu.VMEM((2, tq, LANES), F32)
    return pl.pallas_call(
        functools.partial(_sb_kernel, tq=tq),
        grid=(batch, pairs, nq),
        in_specs=[
            pl.BlockSpec((tq, LANES), lambda b, p, i: (b * nq + i, 2 + p)),
            pl.BlockSpec((seq, LANES), lambda b, p, i: (b, 2 + pairs + p)),
            pl.BlockSpec((seq, LANES), lambda b, p, i: (b, 2 + 2 * pairs + p)),
        ],
        out_specs=pl.BlockSpec((tq, LANES), lambda b, p, i: (b * nq + i, p)),
        out_shape=jax.ShapeDtypeStruct((t, SB_HEADS * SB_DIM), BF16),
        scratch_shapes=[stat, stat],
        compiler_params=_cparams(("arbitrary", "arbitrary", "arbitrary")),
        name="sb_attention",
    )(a, a, a)


def _softmax_step(s, vb, h, acc_ref, l_ref, m_ref):
    reps = s.shape[1] // LANES
    m_old = m_ref[h]
    m_new = jnp.maximum(m_old, jnp.max(s, axis=-1, keepdims=True))
    alpha = jnp.exp2(m_old - m_new)
    p = jnp.exp2(s - jnp.concatenate([m_new] * reps, axis=1))
    part = p[:, 0:LANES]
    for c in range(1, reps):
        part = part + p[:, c * LANES:(c + 1) * LANES]
    l_ref[h] = alpha * l_ref[h] + part
    acc_ref[h] = alpha * acc_ref[h] + _dot(p.astype(BF16), vb)
    m_ref[h] = m_new


def _softmax_init(acc_ref, l_ref, m_ref):
    acc_ref[...] = jnp.zeros_like(acc_ref)
    l_ref[...] = jnp.zeros_like(l_ref)
    m_ref[...] = jnp.full_like(m_ref, NEG_BIG)


def _softmax_out(h, acc_ref, l_ref):
    return acc_ref[h] / jnp.sum(l_ref[h], axis=-1, keepdims=True)


def _mla_kernel(q_ref, k_ref, v_ref, o_ref, acc_ref, l_ref, m_ref, *, tq, widths):
    qi = pl.program_id(2)
    lane = lax.broadcasted_iota(I32, (1, LANES), 1)
    row = lax.broadcasted_iota(I32, (tq, tq), 0)
    col = lax.broadcasted_iota(I32, (tq, tq), 1)
    visible = (col >> CHUNK_SHIFT) <= (row >> CHUNK_SHIFT)
    q = q_ref[...]
    _softmax_init(acc_ref, l_ref, m_ref)

    def step(start, tk, diagonal):
        kb = k_ref[pl.ds(start, tk), :]
        vb = v_ref[pl.ds(start, tk), :]
        for h in range(2):
            s = _dot_t(q[:, h * LANES:(h + 1) * LANES], kb[:, h * LANES:(h + 1) * LANES])
            if diagonal:
                s = jnp.where(visible, s, NEG_BIG)
            _softmax_step(s, vb, h, acc_ref, l_ref, m_ref)

    done = 0
    for width in widths:
        n_steps = (qi * tq - done) // width

        def body(j, c, width=width, done=done):
            step(pl.multiple_of(done + j * width, tq), width, False)
            return c

        lax.fori_loop(0, n_steps, body, 0)
        done = done + n_steps * width
    step(pl.multiple_of(qi * tq, tq), tq, True)
    o_ref[...] = jnp.where(lane < MLA_V, _softmax_out(0, acc_ref, l_ref),
                           _softmax_out(1, acc_ref, l_ref)).astype(BF16)


def _mla_attention(qm, km, vm, batch, seq):
    t = qm.shape[0]
    tq = min(512, seq)
    widths = tuple(w for w in (2048, 1024, 512) if tq <= w <= seq and w % tq == 0)
    nq = seq // tq
    pairs = MLA_HEADS // 2
    stat = pltpu.VMEM((2, tq, LANES), F32)
    return pl.pallas_call(
        functools.partial(_mla_kernel, tq=tq, widths=widths),
        grid=(batch, pairs, nq),
        in_specs=[
            pl.BlockSpec((tq, 2 * LANES), lambda b, p, i: (b * nq + i, p)),
            pl.BlockSpec((seq, 2 * LANES), lambda b, p, i: (b, p)),
            pl.BlockSpec((seq, LANES), lambda b, p, i: (b, p)),
        ],
        out_specs=pl.BlockSpec((tq, LANES), lambda b, p, i: (b * nq + i, p)),
        out_shape=jax.ShapeDtypeStruct((t, N_VM), BF16),
        scratch_shapes=[stat, stat, stat],
        compiler_params=_cparams(("arbitrary", "arbitrary", "arbitrary")),
        name="mla_attention",
    )(qm, km, vm)


SEARCH_BISECT_EVERY = 3
SEARCH_MAX_STEPS = 3 * 32 + 4


def _dsa_kernel(qd_ref, qx_ref, w_ref, kx_ref, kd_ref, vd_ref, o_ref,
                key_ref, qs_ref, wr_ref, top_ref, cand_ref, cnt_ref, cnt3_ref, acc_ref, l_ref, m_ref, *, tq, tk, n_select):
    it = pl.program_id(1)
    t0 = it * tq
    last = t0 // tk
    reps = tk // LANES
    lane = lax.broadcasted_iota(I32, (1, LANES), 1)
    lane2 = lax.broadcasted_iota(I32, (1, 2 * LANES), 1)
    rowid = t0 + lax.broadcasted_iota(I32, (tq, 1), 0)
    row_chunk = rowid >> CHUNK_SHIFT

    qx = qx_ref[...]
    w = w_ref[...]
    for h in range(IDX_HEADS):
        head = (lane2 >= h * IDX_DIM) & (lane2 < (h + 1) * IDX_DIM)
        qs_ref[h * tq:(h + 1) * tq, :] = jnp.where(head, qx, jnp.zeros_like(qx))
        wr_ref[h] = jnp.broadcast_to(w[:, h:h + 1], (tq, LANES))
    top_ref[...] = jnp.full_like(top_ref, INT_MIN)

    def score_block(j, diagonal):
        start = pl.multiple_of(j * tk, tk)
        d = _dot_t(qs_ref[...], kx_ref[pl.ds(start, tk), :])
        score = jnp.zeros((tq, tk), F32)
        for h in range(IDX_HEADS):
            wh = jnp.concatenate([wr_ref[h]] * reps, axis=1)
            score = score + wh * jnp.maximum(d[h * tq:(h + 1) * tq], 0.0)
        score = jnp.where(score == 0.0, 0.0, score)
        bits = pltpu.bitcast(score, I32)
        key = jnp.where(bits < 0, bits ^ jnp.int32(0x7FFFFFFF), bits)
        if diagonal:
            col_chunk = (start + lax.broadcasted_iota(I32, (1, tk), 1)) >> CHUNK_SHIFT
            key = jnp.where(col_chunk <= row_chunk, key, INT_MIN)
        key_ref[j] = key
        t1, t2 = top_ref[0], top_ref[1]
        for c in range(reps):
            x = key[:, c * LANES:(c + 1) * LANES]
            t2 = jnp.maximum(t2, jnp.minimum(t1, x))
            t1 = jnp.maximum(t1, x)
        top_ref[0] = t1
        top_ref[1] = t2

    def score_body(j, c):
        score_block(j, False)
        return c

    lax.fori_loop(0, last, score_body, 0)
    score_block(last, True)
    n_blocks = last + 1

    n_chunks = tq // LANES

    def to_lanes(rep):
        return jnp.concatenate([jnp.transpose(rep[c * LANES:(c + 1) * LANES, :])[0:1, :]
                                for c in range(n_chunks)], axis=1)

    def to_rows(row):
        return jnp.concatenate([jnp.transpose(jnp.broadcast_to(row[:, c * LANES:(c + 1) * LANES], (LANES, LANES)))
                                for c in range(n_chunks)], axis=0)

    def sweep(combine, start, finish):
        cnt_ref[...] = jnp.full_like(cnt_ref, start)

        def body(j, c):
            for half in range(n_chunks):
                rows = slice(half * LANES, (half + 1) * LANES)
                cb = jnp.concatenate([cand_ref[rows, :]] * reps, axis=1)
                cnt_ref[rows, :] = combine(cnt_ref[rows, :], key_ref[j, rows, :], cb)
            return c

        lax.fori_loop(0, n_blocks, body, 0)
        parts = cnt_ref[...]
        return jnp.concatenate([finish(jnp.transpose(parts[c * LANES:(c + 1) * LANES, :]))
                                for c in range(n_chunks)], axis=1)

    def add_ge(acc, keys, cb):
        ge = jnp.where(keys >= cb, 1, 0)
        part = ge[:, 0:LANES]
        for cc in range(1, reps):
            part = part + ge[:, cc * LANES:(cc + 1) * LANES]
        return acc + part

    def min_ge(acc, keys, cb):
        kept = jnp.where(keys >= cb, keys, jnp.int32(2 ** 31 - 1))
        part = kept[:, 0:LANES]
        for cc in range(1, reps):
            part = jnp.minimum(part, kept[:, cc * LANES:(cc + 1) * LANES])
        return jnp.minimum(acc, part)

    def count_prepared():
        return sweep(add_ge, 0, lambda x: jnp.sum(x, axis=0, keepdims=True))

    def count_with_signs(cand):
        cand_ref[...] = to_rows(cand)
        cnt3_ref[...] = jnp.zeros_like(cnt3_ref)

        def body(j, c):
            for half in range(n_chunks):
                rows = slice(half * LANES, (half + 1) * LANES)
                keys = key_ref[j, rows, :]
                cb = jnp.concatenate([cand_ref[rows, :]] * reps, axis=1)
                for slot, bound in enumerate((cb, 1, 0)):
                    cnt3_ref[slot, rows, :] = add_ge(cnt3_ref[slot, rows, :], keys, bound)
            return c

        lax.fori_loop(0, n_blocks, body, 0)
        return [jnp.concatenate([jnp.sum(jnp.transpose(cnt3_ref[slot, c * LANES:(c + 1) * LANES, :]), axis=0,
                                         keepdims=True) for c in range(n_chunks)], axis=1) for slot in range(3)]

    def count_ge(cand):
        cand_ref[...] = to_rows(cand)
        return count_prepared()

    def smallest_ge(cand):
        cand_ref[...] = to_rows(cand)
        return sweep(min_ge, 2 ** 31 - 1, lambda x: jnp.min(x, axis=0, keepdims=True))

    def key_value(k):
        return pltpu.bitcast(jnp.where(k < 0, k ^ jnp.int32(0x7FFFFFFF), k), F32)

    def value_key(v):
        bits = pltpu.bitcast(v, I32)
        return jnp.where(bits < 0, bits ^ jnp.int32(0x7FFFFFFF), bits)

    row_l = t0 + lax.broadcasted_iota(I32, (1, tq), 1)
    few = ((row_l >> CHUNK_SHIFT) + 1) * CHUNK <= n_select
    t2_min = jnp.broadcast_to(jnp.min(top_ref[1], axis=-1, keepdims=True), (tq, LANES))
    t1_max = jnp.broadcast_to(jnp.max(top_ref[0], axis=-1, keepdims=True), (tq, LANES))
    lo = to_lanes(t2_min)
    hi = to_lanes(t1_max) + 1
    c_lo, c_pos, c_nn = count_with_signs(lo)
    c_hi = jnp.zeros((1, tq), I32)
    positive = c_pos >= n_select
    negative = c_nn < n_select
    at_zero = jnp.logical_not(jnp.logical_or(positive, negative))
    raise_lo = jnp.logical_and(positive, lo < 1)
    lower_hi = jnp.logical_and(negative, hi > 0)
    lo, c_lo = jnp.where(raise_lo, 1, lo), jnp.where(raise_lo, c_pos, c_lo)
    hi, c_hi = jnp.where(lower_hi, 0, hi), jnp.where(lower_hi, c_nn, c_hi)
    lo, c_lo = jnp.where(at_zero, 0, lo), jnp.where(at_zero, c_nn, c_lo)
    hi, c_hi = jnp.where(at_zero, 1, hi), jnp.where(at_zero, c_pos, c_hi)

    def propose(step, lo, c_lo, hi, c_hi):
        active = jnp.logical_and(jnp.logical_not(few), jnp.logical_and(c_lo > n_select + 1, hi - lo > 1))
        a = jnp.log(c_lo.astype(F32))
        b = jnp.log(c_hi.astype(F32) + 0.5)
        frac = jnp.clip((a - math.log(n_select + 0.5)) / (a - b), 1.0 / 64, 63.0 / 64)
        v_lo, v_hi = key_value(lo), key_value(hi)
        guess = value_key(v_lo + frac * (v_hi - v_lo))
        middle = lo + lax.shift_right_logical(hi - lo, 1)
        cand = jnp.where(step % SEARCH_BISECT_EVERY == SEARCH_BISECT_EVERY - 1, middle, guess)
        cand = jnp.minimum(jnp.maximum(cand, lo + 1), hi - 1)
        cand_ref[...] = to_rows(cand)
        return cand, jnp.where(active, 1, 0), jnp.max(jnp.where(active, 1, 0))

    def search_cond(carry):
        return jnp.logical_and(carry[0] < SEARCH_MAX_STEPS, carry[1] > 0)

    def search_body(carry):
        step, _, cand, active, lo, c_lo, hi, c_hi = carry
        c = count_prepared()
        up = jnp.logical_and(active > 0, c >= n_select)
        down = jnp.logical_and(active > 0, c < n_select)
        lo, c_lo = jnp.where(up, cand, lo), jnp.where(up, c, c_lo)
        hi, c_hi = jnp.where(down, cand, hi), jnp.where(down, c, c_hi)
        cand, active, busy = propose(step + 1, lo, c_lo, hi, c_hi)
        return step + 1, busy, cand, active, lo, c_lo, hi, c_hi

    cand, active, busy = propose(jnp.int32(0), lo, c_lo, hi, c_hi)
    lo, c_lo, hi, c_hi = lax.while_loop(
        search_cond, search_body, (jnp.int32(0), busy, cand, active, lo, c_lo, hi, c_hi))[4:]

    over = jnp.logical_and(jnp.logical_not(few), jnp.logical_and(c_lo == n_select + 1, hi - lo > 1))
    least = smallest_ge(lo)
    c_drop = count_ge(jnp.where(over, least + 1, lo))
    dropped = jnp.logical_and(over, c_drop == n_select)
    theta = jnp.where(dropped, least + 1, jnp.where(over, least, lo))
    theta = jnp.where(few, INT_MIN, theta)
    c_above = jnp.where(over, c_drop, c_hi)
    tied = jnp.logical_and(jnp.logical_not(few), jnp.logical_and(c_lo > n_select, jnp.logical_not(dropped)))

    @pl.when(jnp.max(jnp.where(tied, 1, 0)) > 0)
    def _():
        theta_r = to_rows(theta)[:, 0:1]
        tied_r = to_rows(jnp.where(tied, 1, 0))[:, 0:1] > 0
        need = to_rows((n_select - c_above).astype(F32))[:, 0:1]
        r = lax.broadcasted_iota(I32, (tk, tk), 0)
        c = lax.broadcasted_iota(I32, (tk, tk), 1)
        before = jnp.where(r < c, 1.0, 0.0).astype(BF16)

        def body(j, seen):
            key = key_ref[j]
            eq = jnp.logical_and(key == theta_r, tied_r)
            eqf = jnp.where(eq, 1.0, 0.0)
            rank = seen + _dot(eqf.astype(BF16), before)
            key_ref[j] = jnp.where(jnp.logical_and(eq, rank >= need), INT_MIN, key)
            return seen + jnp.sum(eqf, axis=-1, keepdims=True)

        lax.fori_loop(0, n_blocks, body, jnp.zeros((tq, 1), F32))

    cand_ref[...] = to_rows(jnp.maximum(theta, INT_MIN + 1))

    qd = qd_ref[...]
    q_heads = []
    for h in range(DSA_HEADS):
        blk = qd[:, (h // 2) * LANES:(h // 2 + 1) * LANES]
        head = (lane >= (h % 2) * DSA_DIM) & (lane < (h % 2 + 1) * DSA_DIM)
        q_heads.append(jnp.where(head, blk, jnp.zeros_like(blk)))
    _softmax_init(acc_ref, l_ref, m_ref)

    def attend(j, c):
        start = pl.multiple_of(j * tk, tk)
        sel = key_ref[j] >= jnp.concatenate([cand_ref[...]] * reps, axis=1)
        kb = kd_ref[pl.ds(start, tk), :]
        vb = vd_ref[pl.ds(start, tk), :]
        for h in range(DSA_HEADS):
            p0 = (h // 2) * LANES
            s = jnp.where(sel, _dot_t(q_heads[h], kb[:, p0:p0 + LANES]), NEG_BIG)
            _softmax_step(s, vb[:, p0:p0 + LANES], h, acc_ref, l_ref, m_ref)
        return c

    def attend_pair(j2, c):
        start = pl.multiple_of(j2 * 2 * tk, tk)
        keys = jnp.concatenate([key_ref[2 * j2], key_ref[2 * j2 + 1]], axis=1)
        sel = keys >= jnp.concatenate([cand_ref[...]] * (2 * reps), axis=1)
        kb = kd_ref[pl.ds(start, 2 * tk), :]
        vb = vd_ref[pl.ds(start, 2 * tk), :]
        for h in range(DSA_HEADS):
            p0 = (h // 2) * LANES
            s = jnp.where(sel, _dot_t(q_heads[h], kb[:, p0:p0 + LANES]), NEG_BIG)
            _softmax_step(s, vb[:, p0:p0 + LANES], h, acc_ref, l_ref, m_ref)
        return c

    n_pairs = n_blocks // 2
    lax.fori_loop(0, n_pairs, attend_pair, 0)
    lax.fori_loop(2 * n_pairs, n_blocks, attend, 0)
    for p in range(DSA_HEADS // 2):
        o_ref[:, p * LANES:(p + 1) * LANES] = jnp.where(
            lane < DSA_DIM, _softmax_out(2 * p, acc_ref, l_ref), _softmax_out(2 * p + 1, acc_ref, l_ref)).astype(BF16)


def _dsa_attention(a, rp, w8, batch, seq, n_select):
    t = a.shape[0]
    tq = 256
    tk = min(1024, seq)
    nq = seq // tq
    width = DSA_HEADS * DSA_DIM
    keys = lambda col: pl.BlockSpec((seq, width), lambda b, i: (b, col), pipeline_mode=pl.Buffered(1))
    stat = lambda n: pltpu.VMEM((n, tq, LANES), F32)
    return pl.pallas_call(
        functools.partial(_dsa_kernel, tq=tq, tk=tk, n_select=n_select),
        grid=(batch, nq),
        in_specs=[
            pl.BlockSpec((tq, width), lambda b, i: (b * nq + i, 0)),
            pl.BlockSpec((tq, width), lambda b, i: (b * nq + i, 2)),
            pl.BlockSpec((tq, N_W), lambda b, i: (b * nq + i, 0)),
            keys(3),
            keys(1),
            keys(0),
        ],
        out_specs=pl.BlockSpec((tq, width), lambda b, i: (b * nq + i, 0)),
        out_shape=jax.ShapeDtypeStruct((t, width), BF16),
        scratch_shapes=[
            pltpu.VMEM((seq // tk, tq, tk), I32),
            pltpu.VMEM((IDX_HEADS * tq, width), BF16),
            stat(IDX_HEADS),
            pltpu.VMEM((2, tq, LANES), I32),
            pltpu.VMEM((tq, LANES), I32),
            pltpu.VMEM((tq, LANES), I32),
            pltpu.VMEM((3, tq, LANES), I32),
            stat(DSA_HEADS), stat(DSA_HEADS), stat(DSA_HEADS),
        ],
        compiler_params=_cparams(("arbitrary", "arbitrary")),
        name="dsa_attention",
    )(rp, rp, w8, rp, rp, a)


def _merge_kernel(x_ref, osb_ref, omla_ref, odsa_ref, g_ref, gm_ref, wsb_ref, wmla_ref, wdsa_ref, wout_ref,
                  o_ref):
    d = D_MODEL
    merged = (g_ref[:, 0:d].astype(F32) * _dot(osb_ref[...], wsb_ref[...])
              + g_ref[:, d:2 * d].astype(F32) * _dot(omla_ref[...], wmla_ref[...])
              + g_ref[:, 2 * d:3 * d].astype(F32) * _dot(odsa_ref[...], wdsa_ref[...]))
    o_ref[...] = x_ref[...] + gm_ref[...] * _dot(merged.astype(BF16), wout_ref[...])


def _merge(x, osb, omla, odsa, gates, gm, wsb, wmla, wdsa, wout, layer, seq):
    t, d = x.shape
    tm = 512
    per = seq // tm
    row = lambda i: (i, 0)
    wspec = lambda k: pl.BlockSpec((None, k, d), lambda i: (layer, 0, 0))
    return pl.pallas_call(
        _merge_kernel,
        grid=(t // tm,),
        in_specs=[
            pl.BlockSpec((tm, d), row),
            pl.BlockSpec((tm, osb.shape[1]), row),
            pl.BlockSpec((tm, omla.shape[1]), row),
            pl.BlockSpec((tm, odsa.shape[1]), row),
            pl.BlockSpec((tm, N_G), row),
            pl.BlockSpec((None, None, 1, d), lambda i: (layer, i // per, 0, 0)),
            wspec(osb.shape[1]), wspec(omla.shape[1]), wspec(odsa.shape[1]), wspec(d),
        ],
        out_specs=pl.BlockSpec((tm, d), row),
        out_shape=jax.ShapeDtypeStruct((t, d), F32),
        compiler_params=_cparams(("arbitrary",)),
        name="merge",
    )(x, osb, omla, odsa, gates, gm, wsb, wmla, wdsa, wout)


def _router_kernel(x_ref, sc_ref, sh_ref, rw_ref, rb_ref, h_ref, ti_ref, tw_ref, tr_ref, cnt_ref, run_ref, *, tm):
    @pl.when(pl.program_id(0) == 0)
    def _():
        run_ref[...] = jnp.zeros_like(run_ref)

    h = _rms(x_ref[...]) * sc_ref[...] + sh_ref[...]
    h_ref[...] = h
    logits = jnp.dot(h, rw_ref[...], preferred_element_type=F32, precision=lax.Precision.HIGHEST) + rb_ref[...]
    lane = lax.broadcasted_iota(I32, (tm, LANES), 1)
    work = logits
    vals, hots = [], []
    for _ in range(TOP_K):
        m = jnp.max(work, axis=-1, keepdims=True)
        first = jnp.min(jnp.where(work == m, lane, LANES), axis=-1, keepdims=True)
        hot = lane == first
        vals.append(m)
        hots.append(hot)
        work = jnp.where(hot, -jnp.inf, work)
    exps = [jnp.exp(v - vals[0]) for v in vals]
    denom = exps[0] + exps[1] + exps[2] + exps[3]
    chosen = jnp.zeros((tm, LANES), F32)
    for hot in hots:
        chosen = chosen + jnp.where(hot, 1.0, 0.0)
    r = lax.broadcasted_iota(I32, (tm, tm), 0)
    c = lax.broadcasted_iota(I32, (tm, tm), 1)
    earlier = jnp.where(c < r, 1.0, 0.0).astype(BF16)
    rank_all = _dot(earlier, chosen.astype(BF16)) + run_ref[...]
    ti = jnp.zeros((tm, LANES), I32)
    tw = jnp.zeros((tm, LANES), F32)
    tr = jnp.zeros((tm, LANES), I32)
    for k in range(TOP_K):
        e_k = jnp.sum(jnp.where(hots[k], lane, 0), axis=-1, keepdims=True)
        r_k = jnp.sum(jnp.where(hots[k], rank_all, 0.0), axis=-1, keepdims=True).astype(I32)
        ti = jnp.where(lane == k, e_k, ti)
        tw = jnp.where(lane == k, exps[k] / denom, tw)
        tr = jnp.where(lane == k, r_k, tr)
    ti_ref[...] = ti
    tw_ref[...] = tw
    tr_ref[...] = tr
    run_ref[...] = run_ref[...] + jnp.sum(chosen, axis=0, keepdims=True)
    cnt_ref[...] = run_ref[...]


def _router(x, scale, shift, rw, rb, layer, seq):
    t, d = x.shape
    tm = 256
    per = seq // tm
    row = lambda i: (i, 0)
    return pl.pallas_call(
        functools.partial(_router_kernel, tm=tm),
        grid=(t // tm,),
        in_specs=[
            pl.BlockSpec((tm, d), row),
            pl.BlockSpec((None, None, 1, d), lambda i: (layer, i // per, 0, 0)),
            pl.BlockSpec((None, None, 1, d), lambda i: (layer, i // per, 0, 0)),
            pl.BlockSpec((None, d, LANES), lambda i: (layer, 0, 0)),
            pl.BlockSpec((None, 1, LANES), lambda i: (layer, 0, 0)),
        ],
        out_specs=[
            pl.BlockSpec((tm, d), row),
            pl.BlockSpec((tm, LANES), row),
            pl.BlockSpec((tm, LANES), row),
            pl.BlockSpec((tm, LANES), row),
            pl.BlockSpec((1, LANES), lambda i: (0, 0)),
        ],
        out_shape=[
            jax.ShapeDtypeStruct((t, d), F32),
            jax.ShapeDtypeStruct((t, LANES), I32),
            jax.ShapeDtypeStruct((t, LANES), F32),
            jax.ShapeDtypeStruct((t, LANES), I32),
            jax.ShapeDtypeStruct((1, LANES), F32),
        ],
        scratch_shapes=[pltpu.VMEM((1, LANES), F32)],
        compiler_params=_cparams(("arbitrary",)),
        name="router",
    )(x, scale, shift, rw, rb)


def _expert_kernel(te_ref, tf_ref, nv_ref, x_ref, wgu_ref, bgu_ref, wd_ref, bd_ref, o_ref,
                   wgu_bf, wd_bf):
    i = pl.program_id(0)

    @pl.when(i >= nv_ref[0])
    def _():
        o_ref[...] = jnp.zeros_like(o_ref)

    @pl.when(i < nv_ref[0])
    def _():
        @pl.when(tf_ref[i] == 1)
        def _():
            wgu_bf[...] = wgu_ref[...].astype(BF16)
            wd_bf[...] = wd_ref[...].astype(BF16)

        gu = _dot(x_ref[...].astype(BF16), wgu_bf[...]) + bgu_ref[...]
        gate = jnp.minimum(gu[:, :D_EXPERT], SWIGLU_LIMIT)
        up = jnp.clip(gu[:, D_EXPERT:], -SWIGLU_LIMIT, SWIGLU_LIMIT)
        act = (up + 1.0) * (gate * (1.0 / (1.0 + jnp.exp(-SWIGLU_ALPHA * gate))))
        o_ref[...] = _dot(act.astype(BF16), wd_bf[...]) + bd_ref[...]


def _experts(xs, tile_e, tile_first, n_valid, wgu, bgu, wd, bd, layer, tm):
    p, d = xs.shape
    n_tiles = p // tm
    grid_spec = pltpu.PrefetchScalarGridSpec(
        num_scalar_prefetch=3,
        grid=(n_tiles,),
        in_specs=[
            pl.BlockSpec((tm, d), lambda i, te, tf, nv: (i, 0)),
            pl.BlockSpec((None, None, d, 2 * D_EXPERT), lambda i, te, tf, nv: (layer, te[i], 0, 0)),
            pl.BlockSpec((None, None, 1, 2 * D_EXPERT), lambda i, te, tf, nv: (layer, te[i], 0, 0)),
            pl.BlockSpec((None, None, D_EXPERT, d), lambda i, te, tf, nv: (layer, te[i], 0, 0)),
            pl.BlockSpec((None, None, 1, d), lambda i, te, tf, nv: (layer, te[i], 0, 0)),
        ],
        out_specs=pl.BlockSpec((tm, d), lambda i, te, tf, nv: (i, 0)),
        scratch_shapes=[pltpu.VMEM((d, 2 * D_EXPERT), BF16), pltpu.VMEM((D_EXPERT, d), BF16)],
    )
    return pl.pallas_call(
        _expert_kernel,
        grid_spec=grid_spec,
        out_shape=jax.ShapeDtypeStruct((p, d), F32),
        compiler_params=_cparams(("arbitrary",)),
        name="experts",
    )(tile_e, tile_first, n_valid, xs, wgu, bgu, wd, bd)


def _combine_kernel(x_ref, y0_ref, y1_ref, y2_ref, y3_ref, tw_ref, g_ref, fg_ref, o_ref, *, final):
    tw = tw_ref[...]
    y = tw[:, 0:1] * y0_ref[...]
    for k, y_ref in enumerate((y1_ref, y2_ref, y3_ref), start=1):
        y = y + tw[:, k:k + 1] * y_ref[...]
    x = x_ref[...] + g_ref[...] * y
    if final:
        x = _rms(x) * fg_ref[...]
    o_ref[...] = x


def _combine(x, yg, tw, gf, final_g, layer, seq, final):
    t, d = x.shape
    tm = 256
    nt = t // tm
    per = seq // tm
    slot = lambda k: pl.BlockSpec((tm, d), lambda i: (k * nt + i, 0))
    return pl.pallas_call(
        functools.partial(_combine_kernel, final=final),
        grid=(nt,),
        in_specs=[
            pl.BlockSpec((tm, d), lambda i: (i, 0)),
            slot(0), slot(1), slot(2), slot(3),
            pl.BlockSpec((tm, LANES), lambda i: (i, 0)),
            pl.BlockSpec((None, None, 1, d), lambda i: (layer, i // per, 0, 0)),
            pl.BlockSpec((1, d), lambda i: (0, 0)),
        ],
        out_specs=pl.BlockSpec((tm, d), lambda i: (i, 0)),
        out_shape=jax.ShapeDtypeStruct((t, d), F32),
        compiler_params=_cparams(("arbitrary",)),
        name="combine",
    )(x, yg, yg, yg, yg, tw, gf, final_g)


def _rope_tables(positions):
    pos = positions.reshape(-1).astype(F32)

    def cs(dim):
        inv_freq = ROPE_THETA ** (-jnp.arange(0, dim, 2, dtype=F32) / dim)
        ang = pos[:, None] * inv_freq
        return jnp.cos(ang), jnp.sin(ang)

    c_d, s_d = cs(DSA_DIM)
    c_i, s_i = cs(IDX_DIM)
    c_m, s_m = cs(MLA_ROPE)
    t = pos.shape[0]
    ones, zeros = jnp.ones((t, 64), F32), jnp.zeros((t, 64), F32)
    cos_t = jnp.concatenate([jnp.tile(c_d, (1, 4)), jnp.tile(c_i, (1, 8)),
                             ones, c_m, c_m, ones[:, :32]], axis=1)
    sin_t = jnp.concatenate([jnp.tile(s_d, (1, 4)), jnp.tile(s_i, (1, 8)),
                             zeros, s_m, s_m, zeros[:, :32]], axis=1)
    return cos_t, sin_t


def _permute_cols(w, idx, sgn):
    return (w.at[..., jnp.asarray(idx)].get(mode="promise_in_bounds") * jnp.asarray(sgn)).astype(BF16)


def _dispatch_tables(ti, tr, counts, tm):
    t = ti.shape[0]
    p = t * TOP_K + N_EXPERTS * tm
    n_tiles = p // tm
    cnt = counts.astype(I32)
    padded = ((cnt + tm - 1) // tm) * tm
    ends = jnp.cumsum(padded)
    starts = ends - padded
    pos = starts[ti] + tr
    tok = jnp.repeat(jnp.arange(t, dtype=I32), TOP_K)
    row_src = jnp.zeros((p,), I32).at[pos.reshape(-1)].set(tok, unique_indices=True, mode="promise_in_bounds")
    tile_start = jnp.arange(n_tiles, dtype=I32) * tm
    n_valid = ends[-1] // tm
    tile_e = jnp.sum((ends[None, :] <= tile_start[:, None]).astype(I32), axis=1)
    tile_e = jnp.minimum(tile_e, N_EXPERTS - 1)
    last_e = jnp.max(jnp.where(tile_start < ends[-1], tile_e, 0))
    tile_e = jnp.where(tile_start < ends[-1], tile_e, last_e)
    tile_first = jnp.concatenate([jnp.ones((1,), I32), (tile_e[1:] != tile_e[:-1]).astype(I32)])
    return pos, row_src, tile_e, tile_first, n_valid.astype(I32).reshape(1)


def kernel(x, c, positions, ada_w, ada_b, norm_mix_g, w_in, mla_q_norm_g, mla_kv_norm_g, mla_w_uq, mla_w_ukv,
           w_sb_out, w_mla_out, w_dsa_out, w_out, norm_ffn_g, router_w, router_b, expert_w_gu, expert_b_gu,
           expert_w_down, expert_b_down, final_norm_g):
    batch, seq, d = x.shape
    depth = ada_w.shape[0]
    t = batch * seq
    n_select = min(DSA_TOPK_MAX, seq // 4)
    tm_e = 256

    mod = _ada_mod(c, ada_w, ada_b)
    sh_m, sc_m, g_m, sh_f, sc_f, g_f = [m[:, :, None, :] for m in jnp.split(mod, 6, axis=-1)]
    scale_m = norm_mix_g[:, None, None, :] * (1.0 + sc_m)
    scale_f = norm_ffn_g[:, None, None, :] * (1.0 + sc_f)

    cos_t, sin_t = _rope_tables(positions)
    w_in_p = _permute_cols(w_in, _IN_IDX, _IN_SGN)
    w_uq_p = _permute_cols(mla_w_uq, _UQ_IDX, _UQ_SGN)
    w_ukv_p = _permute_cols(mla_w_ukv, _UKV_IDX, _UKV_SGN)
    w_sb_b, w_mla_b, w_dsa_b, w_out_b = [w.astype(BF16) for w in (w_sb_out, w_mla_out, w_dsa_out, w_out)]
    gq = mla_q_norm_g[:, None, :]
    gkv = mla_kv_norm_g[:, None, :]
    rw_p = jnp.zeros((depth, d, LANES), F32).at[:, :, :N_EXPERTS].set(router_w)
    rb_p = jnp.full((depth, 1, LANES), -jnp.inf, F32).at[:, 0, :N_EXPERTS].set(router_b)
    bgu = expert_b_gu[:, :, None, :]
    bd = expert_b_down[:, :, None, :]
    fg = final_norm_g[None, :]

    xf = x.reshape(t, d)
    for l in range(depth):
        a, cl, rp, w8, gates = _inproj(xf, scale_m, sh_m, w_in_p, cos_t, sin_t, l, seq)
        o_sb = _sb_attention(a, batch, seq)
        qm, km, vm = _mla_up(cl, gq, gkv, w_uq_p, w_ukv_p, cos_t, sin_t, rp, l)
        o_mla = _mla_attention(qm, km, vm, batch, seq)
        o_dsa = _dsa_attention(a, rp, w8, batch, seq, n_select)
        xf = _merge(xf, o_sb, o_mla, o_dsa, gates, g_m, w_sb_b, w_mla_b, w_dsa_b, w_out_b, l, seq)

        hf, ti, tw, tr, counts = _router(xf, scale_f, sh_f, rw_p, rb_p, l, seq)
        pos, row_src, tile_e, tile_first, n_valid = _dispatch_tables(
            ti[:, :TOP_K], tr[:, :TOP_K], counts[0, :N_EXPERTS], tm_e)
        xs = hf.at[row_src].get(mode="promise_in_bounds")
        ys = _experts(xs, tile_e, tile_first, n_valid, expert_w_gu, bgu, expert_w_down, bd, l, tm_e)
        yg = ys.at[pos.T.reshape(-1)].get(mode="promise_in_bounds")
        xf = _combine(xf, yg, tw, g_f, fg, l, seq, l == depth - 1)
    return xf.reshape(batch, seq, d)
```

```python
import functools
import math

import numpy as np
import jax
import jax.numpy as jnp
from jax import lax
from jax.experimental import pallas as pl
from jax.experimental.pallas import tpu as pltpu

F32 = jnp.float32
BF16 = jnp.bfloat16
I32 = jnp.int32

D_MODEL = 1024
CHUNK = 64
CHUNK_SHIFT = 6
ROPE_THETA = 10000.0
NORM_EPS = 1e-6
SB_HEADS, SB_DIM = 6, 64
MLA_HEADS, MLA_NOPE, MLA_ROPE, MLA_V = 6, 64, 32, 64
MLA_Q_RANK, MLA_KV_RANK = 256, 128
DSA_HEADS, DSA_DIM = 4, 64
IDX_HEADS, IDX_DIM = 8, 32
DSA_TOPK_MAX = 256
N_EXPERTS, TOP_K = 32, 4
D_EXPERT = D_MODEL
SWIGLU_LIMIT = 7.0
SWIGLU_ALPHA = 1.702

LANES = 128
VMEM_LIMIT = 56 * 1024 * 1024

INT_MIN = -(2 ** 31)
NEG_BIG = -1e30
SB_UNDERFLOW = 104.0
LOG2E = math.log2(math.e)

_SPLIT = (384, 384, 384, 256, 128, 32, 256, 256, 256, 256, 32, 8, 3072)
_OFF = np.concatenate([[0], np.cumsum(_SPLIT)]).astype(np.int64)
(O_SBQ, O_SBK, O_SBV, O_CQ, O_CKV, O_KR, O_DQ, O_DK, O_DV, O_IQ, O_IK, O_IW, O_GATE, D_IN) = [
    int(v) for v in _OFF]

N_A = 1408
N_C = 384
N_R = 1152
N_W = 128
N_G = 3072
N_IN = N_A + N_C + N_R + N_W + N_G
_ROPE_SPANS = ((0, 2, 0, DSA_DIM, DSA_DIM ** -0.5 * LOG2E), (256, 2, 0, DSA_DIM, 1.0), (512, 2, 1, IDX_DIM, 1.0),
               (768, 2, 1, IDX_DIM, 1.0), (1024, 1, 2, MLA_ROPE, 1.0))


def _rot_cols(base, n_heads, d):
    half = d // 2
    idx, sgn = [], []
    for h in range(n_heads):
        for j in range(d):
            if j < half:
                idx.append(base + h * d + j + half)
                sgn.append(-1.0)
            else:
                idx.append(base + h * d + j - half)
                sgn.append(1.0)
    return idx, sgn


def _in_layout():
    idx, sgn = [], []

    def plain(base, n):
        idx.extend(range(base, base + n))
        sgn.extend([1.0] * n)

    def pad(n):
        idx.extend([0] * n)
        sgn.extend([0.0] * n)

    plain(O_DV, 256); plain(O_SBQ, 384); plain(O_SBK, 384); plain(O_SBV, 384)
    plain(O_CQ, 256); plain(O_CKV, 128)
    plain(O_DQ, 256); plain(O_DK, 256); plain(O_IQ, 256)
    for _ in range(IDX_HEADS):
        plain(O_IK, IDX_DIM)
    pad(64); plain(O_KR, 32); pad(32)
    plain(O_IW, IDX_HEADS); pad(N_W - IDX_HEADS)
    plain(O_GATE, N_G)
    assert len(idx) == N_IN
    return np.asarray(idx, np.int32), np.asarray(sgn, np.float32)


_IN_IDX, _IN_SGN = _in_layout()


def _uq_layout():
    per = MLA_NOPE + MLA_ROPE
    idx, sgn = [], []
    for h in range(MLA_HEADS):
        idx.extend(range(h * per, h * per + per)); sgn.extend([1.0] * per)
        idx.extend([0] * 32); sgn.extend([0.0] * 32)
    for h in range(MLA_HEADS):
        idx.extend([0] * MLA_NOPE); sgn.extend([0.0] * MLA_NOPE)
        i, s = _rot_cols(h * per + MLA_NOPE, 1, MLA_ROPE)
        idx.extend(i); sgn.extend(s)
        idx.extend([0] * 32); sgn.extend([0.0] * 32)
    return np.asarray(idx, np.int32), np.asarray(sgn, np.float32)


def _ukv_layout():
    per = MLA_NOPE + MLA_V
    idx, sgn = [], []
    for h in range(MLA_HEADS):
        idx.extend(range(h * per, h * per + MLA_NOPE)); sgn.extend([1.0] * MLA_NOPE)
        idx.extend([0] * 64); sgn.extend([0.0] * 64)
    for h in range(MLA_HEADS):
        idx.extend(range(h * per + MLA_NOPE, h * per + per)); sgn.extend([1.0] * MLA_V)
    return np.asarray(idx, np.int32), np.asarray(sgn, np.float32)


_UQ_IDX, _UQ_SGN = _uq_layout()
_UKV_IDX, _UKV_SGN = _ukv_layout()
N_QM = MLA_HEADS * LANES
N_VM = MLA_HEADS * MLA_V


def _cparams(sem):
    return pltpu.CompilerParams(dimension_semantics=sem, vmem_limit_bytes=VMEM_LIMIT)


def _dot(a, b):
    return jnp.dot(a, b, preferred_element_type=F32)


def _dot_t(a, b):
    return lax.dot_general(a, b, (((1,), (1,)), ((), ())), preferred_element_type=F32)


def _rms(x):
    return x * lax.rsqrt(jnp.mean(x * x, axis=-1, keepdims=True) + NORM_EPS)


def _ada_kernel(c_ref, w_ref, b_ref, o_ref):
    c = c_ref[...]
    sc = c * (1.0 / (1.0 + jnp.exp(-c)))
    o_ref[...] = jnp.dot(sc, w_ref[...], preferred_element_type=F32,
                         precision=lax.Precision.HIGHEST) + b_ref[...]


def _ada_mod(c, ada_w, ada_b):
    depth, d, n = ada_w.shape
    b = c.shape[0]
    rows = 8
    cp = jnp.zeros((rows, d), F32).at[:b].set(c)
    tn = 2048
    out = pl.pallas_call(
        _ada_kernel,
        grid=(depth, n // tn),
        in_specs=[
            pl.BlockSpec((rows, d), lambda l, j: (0, 0)),
            pl.BlockSpec((None, d, tn), lambda l, j: (l, 0, j)),
            pl.BlockSpec((None, 1, tn), lambda l, j: (l, 0, j)),
        ],
        out_specs=pl.BlockSpec((None, rows, tn), lambda l, j: (l, 0, j)),
        out_shape=jax.ShapeDtypeStruct((depth, rows, n), F32),
        compiler_params=_cparams(("arbitrary", "arbitrary")),
        name="ada_mod",
    )(cp, ada_w, ada_b.reshape(depth, 1, n))
    return out[:, :b]


def _inproj_kernel(x_ref, sc_ref, sh_ref, w_ref, cos_ref, sin_ref,
                   a_ref, c_ref, r_ref, w8_ref, g_ref):
    h = (_rms(x_ref[...]) * sc_ref[...] + sh_ref[...]).astype(BF16)
    o = 0
    for c0 in range(0, N_A, 256):
        c1 = min(c0 + 256, N_A)
        a_ref[:, c0:c1] = _dot(h, w_ref[:, o + c0:o + c1]).astype(BF16)
    o += N_A
    c_ref[...] = _dot(h, w_ref[:, o:o + N_C])
    o += N_C
    for lo, n, kind, dim, scale in _ROPE_SPANS:
        y = _dot(h, w_ref[:, o + lo:o + lo + n * LANES])
        width, half = n * LANES, dim // 2
        first = (lax.broadcasted_iota(I32, (1, width), 1) & (dim - 1)) < half
        yr = jnp.where(first, -pltpu.roll(y, width - half, 1), pltpu.roll(y, half, 1))
        cs = jnp.concatenate([cos_ref[:, kind * LANES:(kind + 1) * LANES]] * n, axis=1)
        sn = jnp.concatenate([sin_ref[:, kind * LANES:(kind + 1) * LANES]] * n, axis=1)
        r = y * cs + yr * sn
        if scale != 1.0:
            r = r * scale
        r_ref[:, lo:lo + n * LANES] = r.astype(BF16)
    o += N_R
    w8_ref[...] = _dot(h, w_ref[:, o:o + N_W]) * (IDX_DIM ** -0.5 * IDX_HEADS ** -0.5)
    o += N_W
    for c0 in range(0, N_G, 512):
        z = _dot(h, w_ref[:, o + c0:o + c0 + 512])
        g_ref[:, c0:c0 + 512] = (1.0 / (1.0 + jnp.exp(-z))).astype(BF16)


def _inproj(x, scale, shift, w, cos_t, sin_t, layer, seq):
    t, d = x.shape
    tm = 512
    per = seq // tm
    return pl.pallas_call(
        _inproj_kernel,
        grid=(t // tm,),
        in_specs=[
            pl.BlockSpec((tm, d), lambda i: (i, 0)),
            pl.BlockSpec((None, None, 1, d), lambda i: (layer, i // per, 0, 0)),
            pl.BlockSpec((None, None, 1, d), lambda i: (layer, i // per, 0, 0)),
            pl.BlockSpec((None, d, N_IN), lambda i: (layer, 0, 0), pipeline_mode=pl.Buffered(1)),
            pl.BlockSpec((tm, 3 * LANES), lambda i: (i, 0)),
            pl.BlockSpec((tm, 3 * LANES), lambda i: (i, 0)),
        ],
        out_specs=[
            pl.BlockSpec((tm, N_A), lambda i: (i, 0)),
            pl.BlockSpec((tm, N_C), lambda i: (i, 0)),
            pl.BlockSpec((tm, N_R), lambda i: (i, 0)),
            pl.BlockSpec((tm, N_W), lambda i: (i, 0)),
            pl.BlockSpec((tm, N_G), lambda i: (i, 0)),
        ],
        out_shape=[
            jax.ShapeDtypeStruct((t, N_A), BF16),
            jax.ShapeDtypeStruct((t, N_C), F32),
            jax.ShapeDtypeStruct((t, N_R), BF16),
            jax.ShapeDtypeStruct((t, N_W), F32),
            jax.ShapeDtypeStruct((t, N_G), BF16),
        ],
        compiler_params=_cparams(("arbitrary",)),
        name="inproj",
    )(x, scale, shift, w, cos_t, sin_t)


def _mla_up_kernel(c_ref, gq_ref, gkv_ref, wq_ref, wkv_ref, cos_ref, sin_ref, kr_ref,
                   q_ref, k_ref, v_ref):
    c = c_ref[...]
    nq = (_rms(c[:, :MLA_Q_RANK]) * gq_ref[...]).astype(BF16)
    nkv = (_rms(c[:, MLA_Q_RANK:]) * gkv_ref[...]).astype(BF16)
    scale = (MLA_NOPE + MLA_ROPE) ** -0.5 * LOG2E
    cs = cos_ref[...] * scale
    sn = sin_ref[...] * scale
    kr = kr_ref[...].astype(F32)
    for h in range(MLA_HEADS):
        lo = h * LANES
        y = _dot(nq, wq_ref[:, lo:lo + LANES])
        yr = _dot(nq, wq_ref[:, N_QM + lo:N_QM + lo + LANES])
        q_ref[:, lo:lo + LANES] = (y * cs + yr * sn).astype(BF16)
        k_ref[:, lo:lo + LANES] = (_dot(nkv, wkv_ref[:, lo:lo + LANES]) + kr).astype(BF16)
    v_ref[...] = _dot(nkv, wkv_ref[:, N_QM:N_QM + N_VM]).astype(BF16)


def _mla_up(cl, gq, gkv, wq, wkv, cos_t, sin_t, rp, layer):
    t = cl.shape[0]
    tm = 512
    return pl.pallas_call(
        _mla_up_kernel,
        grid=(t // tm,),
        in_specs=[
            pl.BlockSpec((tm, N_C), lambda i: (i, 0)),
            pl.BlockSpec((None, 1, MLA_Q_RANK), lambda i: (layer, 0, 0)),
            pl.BlockSpec((None, 1, MLA_KV_RANK), lambda i: (layer, 0, 0)),
            pl.BlockSpec((None, MLA_Q_RANK, 2 * N_QM), lambda i: (layer, 0, 0)),
            pl.BlockSpec((None, MLA_KV_RANK, N_QM + N_VM), lambda i: (layer, 0, 0)),
            pl.BlockSpec((tm, LANES), lambda i: (i, 2)),
            pl.BlockSpec((tm, LANES), lambda i: (i, 2)),
            pl.BlockSpec((tm, LANES), lambda i: (i, 8)),
        ],
        out_specs=[
            pl.BlockSpec((tm, N_QM), lambda i: (i, 0)),
            pl.BlockSpec((tm, N_QM), lambda i: (i, 0)),
            pl.BlockSpec((tm, N_VM), lambda i: (i, 0)),
        ],
        out_shape=[
            jax.ShapeDtypeStruct((t, N_QM), BF16),
            jax.ShapeDtypeStruct((t, N_QM), BF16),
            jax.ShapeDtypeStruct((t, N_VM), BF16),
        ],
        compiler_params=_cparams(("arbitrary",)),
        name="mla_up",
    )(cl, gq, gkv, wq, wkv, cos_t, sin_t, rp)


def _sb_kernel(q_ref, k_ref, v_ref, o_ref, acc_ref, rem_ref, *, tq):
    qi = pl.program_id(2)
    lane = lax.broadcasted_iota(I32, (1, LANES), 1)
    row = lax.broadcasted_iota(I32, (tq, tq), 0)
    col = lax.broadcasted_iota(I32, (tq, tq), 1)
    causal = col < row
    tri = jnp.where(row > col, 1.0, 0.0).astype(BF16)
    q = q_ref[...]
    q_heads = []
    for h in range(2):
        head = (lane >= h * SB_DIM) & (lane < (h + 1) * SB_DIM)
        q_heads.append(jnp.where(head, q, jnp.zeros_like(q)) * jnp.asarray(SB_DIM ** -0.5, BF16))
    acc_ref[...] = jnp.zeros_like(acc_ref)
    rem_ref[...] = jnp.zeros_like(rem_ref)

    def block(j, diagonal, live=None):
        start = pl.multiple_of(j * tq, tq)
        kb = k_ref[pl.ds(start, tq), :]
        vb = v_ref[pl.ds(start, tq), :]
        for h in range(2):
            z = _dot_t(q_heads[h], kb)
            soft = jnp.log1p(jnp.exp(-jnp.abs(z)))
            log_stay = -(jnp.maximum(z, 0.0) + soft)
            if diagonal:
                log_stay = jnp.where(causal, log_stay, 0.0)
            if live is not None:
                log_stay = jnp.where(live, log_stay, 0.0)
            hi = log_stay.astype(BF16)
            lo = (log_stay - hi.astype(F32)).astype(BF16)
            later = _dot(hi, tri) + _dot(lo, tri)
            remain = rem_ref[h]
            log_a = (z + log_stay) + later + jnp.concatenate([remain] * (tq // LANES), axis=1)
            a = jnp.exp(log_a)
            if diagonal:
                a = jnp.where(causal, a, 0.0)
            if live is not None:
                a = jnp.where(live, a, 0.0)
            acc_ref[h] += _dot(a.astype(BF16), vb)
            rem_ref[h] = remain + jnp.sum(log_stay, axis=-1, keepdims=True)

    block(qi, True)
    block(jnp.maximum(qi - 1, 0), False, live=qi > 0)

    def cond(j):
        return jnp.logical_and(j >= 0, jnp.max(rem_ref[...]) > -SB_UNDERFLOW)

    def body(j):
        block(j, False)
        return j - 1

    lax.while_loop(cond, body, qi - 2)
    o_ref[...] = jnp.where(lane < SB_DIM, acc_ref[0], acc_ref[1]).astype(BF16)


def _sb_attention(a, batch, seq):
    t = a.shape[0]
    tq = 256
    nq = seq // tq
    pairs = SB_HEADS // 2
    stat = pltpu.VMEM((2, tq, LANES), F32)
    return pl.pallas_call(
        functools.partial(_sb_kernel, tq=tq),
        grid=(batch, pairs, nq),
        in_specs=[
            pl.BlockSpec((tq, LANES), lambda b, p, i: (b * nq + i, 2 + p)),
            pl.BlockSpec((seq, LANES), lambda b, p, i: (b, 2 + pairs + p)),
            pl.BlockSpec((seq, LANES), lambda b, p, i: (b, 2 + 2 * pairs + p)),
        ],
        out_specs=pl.BlockSpec((tq, LANES), lambda b, p, i: (b * nq + i, p)),
        out_shape=jax.ShapeDtypeStruct((t, SB_HEADS * SB_DIM), BF16),
        scratch_shapes=[stat, stat],
        compiler_params=_cparams(("arbitrary", "arbitrary", "arbitrary")),
        name="sb_attention",
    )(a, a, a)


def _softmax_step(s, vb, h, acc_ref, l_ref, m_ref):
    reps = s.shape[1] // LANES
    m_old = m_ref[h]
    m_new = jnp.maximum(m_old, jnp.max(s, axis=-1, keepdims=True))
    alpha = jnp.exp2(m_old - m_new)
    p = jnp.exp2(s - jnp.concatenate([m_new] * reps, axis=1))
    part = p[:, 0:LANES]
    for c in range(1, reps):
        part = part + p[:, c * LANES:(c + 1) * LANES]
    l_ref[h] = alpha * l_ref[h] + part
    acc_ref[h] = alpha * acc_ref[h] + _dot(p.astype(BF16), vb)
    m_ref[h] = m_new


def _softmax_init(acc_ref, l_ref, m_ref):
    acc_ref[...] = jnp.zeros_like(acc_ref)
    l_ref[...] = jnp.zeros_like(l_ref)
    m_ref[...] = jnp.full_like(m_ref, NEG_BIG)


def _softmax_out(h, acc_ref, l_ref):
    return acc_ref[h] / jnp.sum(l_ref[h], axis=-1, keepdims=True)


def _mla_kernel(q_ref, k_ref, v_ref, o_ref, acc_ref, l_ref, m_ref, *, tq, widths):
    qi = pl.program_id(2)
    lane = lax.broadcasted_iota(I32, (1, LANES), 1)
    row = lax.broadcasted_iota(I32, (tq, tq), 0)
    col = lax.broadcasted_iota(I32, (tq, tq), 1)
    visible = (col >> CHUNK_SHIFT) <= (row >> CHUNK_SHIFT)
    q = q_ref[...]
    _softmax_init(acc_ref, l_ref, m_ref)

    def step(start, tk, diagonal):
        kb = k_ref[pl.ds(start, tk), :]
        vb = v_ref[pl.ds(start, tk), :]
        for h in range(2):
            s = _dot_t(q[:, h * LANES:(h + 1) * LANES], kb[:, h * LANES:(h + 1) * LANES])
            if diagonal:
                s = jnp.where(visible, s, NEG_BIG)
            _softmax_step(s, vb, h, acc_ref, l_ref, m_ref)

    done = 0
    for width in widths:
        n_steps = (qi * tq - done) // width

        def body(j, c, width=width, done=done):
            step(pl.multiple_of(done + j * width, tq), width, False)
            return c

        lax.fori_loop(0, n_steps, body, 0)
        done = done + n_steps * width
    step(pl.multiple_of(qi * tq, tq), tq, True)
    o_ref[...] = jnp.where(lane < MLA_V, _softmax_out(0, acc_ref, l_ref),
                           _softmax_out(1, acc_ref, l_ref)).astype(BF16)


def _mla_attention(qm, km, vm, batch, seq):
    t = qm.shape[0]
    tq = min(512, seq)
    widths = tuple(w for w in (2048, 1024, 512) if tq <= w <= seq and w % tq == 0)
    nq = seq // tq
    pairs = MLA_HEADS // 2
    stat = pltpu.VMEM((2, tq, LANES), F32)
    return pl.pallas_call(
        functools.partial(_mla_kernel, tq=tq, widths=widths),
        grid=(batch, pairs, nq),
        in_specs=[
            pl.BlockSpec((tq, 2 * LANES), lambda b, p, i: (b * nq + i, p)),
            pl.BlockSpec((seq, 2 * LANES), lambda b, p, i: (b, p)),
            pl.BlockSpec((seq, LANES), lambda b, p, i: (b, p)),
        ],
        out_specs=pl.BlockSpec((tq, LANES), lambda b, p, i: (b * nq + i, p)),
        out_shape=jax.ShapeDtypeStruct((t, N_VM), BF16),
        scratch_shapes=[stat, stat, stat],
        compiler_params=_cparams(("arbitrary", "arbitrary", "arbitrary")),
        name="mla_attention",
    )(qm, km, vm)


SEARCH_BISECT_EVERY = 3
SEARCH_MAX_STEPS = 3 * 32 + 4


def _dsa_kernel(qd_ref, qx_ref, w_ref, kx_ref, kd_ref, vd_ref, o_ref,
                key_ref, qs_ref, wr_ref, top_ref, cand_ref, cnt_ref, cnt3_ref, acc_ref, l_ref, m_ref, *, tq, tk, n_select):
    it = pl.program_id(1)
    t0 = it * tq
    last = t0 // tk
    reps = tk // LANES
    lane = lax.broadcasted_iota(I32, (1, LANES), 1)
    lane2 = lax.broadcasted_iota(I32, (1, 2 * LANES), 1)
    rowid = t0 + lax.broadcasted_iota(I32, (tq, 1), 0)
    row_chunk = rowid >> CHUNK_SHIFT

    qx = qx_ref[...]
    w = w_ref[...]
    for h in range(IDX_HEADS):
        head = (lane2 >= h * IDX_DIM) & (lane2 < (h + 1) * IDX_DIM)
        qs_ref[h * tq:(h + 1) * tq, :] = jnp.where(head, qx, jnp.zeros_like(qx))
        wr_ref[h] = jnp.broadcast_to(w[:, h:h + 1], (tq, LANES))
    top_ref[...] = jnp.full_like(top_ref, INT_MIN)

    def score_block(j, diagonal):
        start = pl.multiple_of(j * tk, tk)
        d = _dot_t(qs_ref[...], kx_ref[pl.ds(start, tk), :])
        score = jnp.zeros((tq, tk), F32)
        for h in range(IDX_HEADS):
            wh = jnp.concatenate([wr_ref[h]] * reps, axis=1)
            score = score + wh * jnp.maximum(d[h * tq:(h + 1) * tq], 0.0)
        score = jnp.where(score == 0.0, 0.0, score)
        bits = pltpu.bitcast(score, I32)
        key = jnp.where(bits < 0, bits ^ jnp.int32(0x7FFFFFFF), bits)
        if diagonal:
            col_chunk = (start + lax.broadcasted_iota(I32, (1, tk), 1)) >> CHUNK_SHIFT
            key = jnp.where(col_chunk <= row_chunk, key, INT_MIN)
        key_ref[j] = key
        t1, t2 = top_ref[0], top_ref[1]
        for c in range(reps):
            x = key[:, c * LANES:(c + 1) * LANES]
            t2 = jnp.maximum(t2, jnp.minimum(t1, x))
            t1 = jnp.maximum(t1, x)
        top_ref[0] = t1
        top_ref[1] = t2

    def score_body(j, c):
        score_block(j, False)
        return c

    lax.fori_loop(0, last, score_body, 0)
    score_block(last, True)
    n_blocks = last + 1

    n_chunks = tq // LANES

    def to_lanes(rep):
        return jnp.concatenate([jnp.transpose(rep[c * LANES:(c + 1) * LANES, :])[0:1, :]
                                for c in range(n_chunks)], axis=1)

    def to_rows(row):
        return jnp.concatenate([jnp.transpose(jnp.broadcast_to(row[:, c * LANES:(c + 1) * LANES], (LANES, LANES)))
                                for c in range(n_chunks)], axis=0)

    def sweep(combine, start, finish):
        cnt_ref[...] = jnp.full_like(cnt_ref, start)

        def body(j, c):
            for half in range(n_chunks):
                rows = slice(half * LANES, (half + 1) * LANES)
                cb = jnp.concatenate([cand_ref[rows, :]] * reps, axis=1)
                cnt_ref[rows, :] = combine(cnt_ref[rows, :], key_ref[j, rows, :], cb)
            return c

        lax.fori_loop(0, n_blocks, body, 0)
        parts = cnt_ref[...]
        return jnp.concatenate([finish(jnp.transpose(parts[c * LANES:(c + 1) * LANES, :]))
                                for c in range(n_chunks)], axis=1)

    def add_ge(acc, keys, cb):
        ge = jnp.where(keys >= cb, 1, 0)
        part = ge[:, 0:LANES]
        for cc in range(1, reps):
            part = part + ge[:, cc * LANES:(cc + 1) * LANES]
        return acc + part

    def min_ge(acc, keys, cb):
        kept = jnp.where(keys >= cb, keys, jnp.int32(2 ** 31 - 1))
        part = kept[:, 0:LANES]
        for cc in range(1, reps):
            part = jnp.minimum(part, kept[:, cc * LANES:(cc + 1) * LANES])
        return jnp.minimum(acc, part)

    def count_prepared():
        return sweep(add_ge, 0, lambda x: jnp.sum(x, axis=0, keepdims=True))

    def count_with_signs(cand):
        cand_ref[...] = to_rows(cand)
        cnt3_ref[...] = jnp.zeros_like(cnt3_ref)

        def body(j, c):
            for half in range(n_chunks):
                rows = slice(half * LANES, (half + 1) * LANES)
                keys = key_ref[j, rows, :]
                cb = jnp.concatenate([cand_ref[rows, :]] * reps, axis=1)
                for slot, bound in enumerate((cb, 1, 0)):
                    cnt3_ref[slot, rows, :] = add_ge(cnt3_ref[slot, rows, :], keys, bound)
            return c

        lax.fori_loop(0, n_blocks, body, 0)
        return [jnp.concatenate([jnp.sum(jnp.transpose(cnt3_ref[slot, c * LANES:(c + 1) * LANES, :]), axis=0,
                                         keepdims=True) for c in range(n_chunks)], axis=1) for slot in range(3)]

    def count_ge(cand):
        cand_ref[...] = to_rows(cand)
        return count_prepared()

    def smallest_ge(cand):
        cand_ref[...] = to_rows(cand)
        return sweep(min_ge, 2 ** 31 - 1, lambda x: jnp.min(x, axis=0, keepdims=True))

    def key_value(k):
        return pltpu.bitcast(jnp.where(k < 0, k ^ jnp.int32(0x7FFFFFFF), k), F32)

    def value_key(v):
        bits = pltpu.bitcast(v, I32)
        return jnp.where(bits < 0, bits ^ jnp.int32(0x7FFFFFFF), bits)

    row_l = t0 + lax.broadcasted_iota(I32, (1, tq), 1)
    few = ((row_l >> CHUNK_SHIFT) + 1) * CHUNK <= n_select
    t2_min = jnp.broadcast_to(jnp.min(top_ref[1], axis=-1, keepdims=True), (tq, LANES))
    t1_max = jnp.broadcast_to(jnp.max(top_ref[0], axis=-1, keepdims=True), (tq, LANES))
    lo = to_lanes(t2_min)
    hi = to_lanes(t1_max) + 1
    c_lo, c_pos, c_nn = count_with_signs(lo)
    c_hi = jnp.zeros((1, tq), I32)
    positive = c_pos >= n_select
    negative = c_nn < n_select
    at_zero = jnp.logical_not(jnp.logical_or(positive, negative))
    raise_lo = jnp.logical_and(positive, lo < 1)
    lower_hi = jnp.logical_and(negative, hi > 0)
    lo, c_lo = jnp.where(raise_lo, 1, lo), jnp.where(raise_lo, c_pos, c_lo)
    hi, c_hi = jnp.where(lower_hi, 0, hi), jnp.where(lower_hi, c_nn, c_hi)
    lo, c_lo = jnp.where(at_zero, 0, lo), jnp.where(at_zero, c_nn, c_lo)
    hi, c_hi = jnp.where(at_zero, 1, hi), jnp.where(at_zero, c_pos, c_hi)

    def propose(step, lo, c_lo, hi, c_hi):
        active = jnp.logical_and(jnp.logical_not(few), jnp.logical_and(c_lo > n_select + 1, hi - lo > 1))
        a = jnp.log(c_lo.astype(F32))
        b = jnp.log(c_hi.astype(F32) + 0.5)
        frac = jnp.clip((a - math.log(n_select + 0.5)) / (a - b), 1.0 / 64, 63.0 / 64)
        v_lo, v_hi = key_value(lo), key_value(hi)
        guess = value_key(v_lo + frac * (v_hi - v_lo))
        middle = lo + lax.shift_right_logical(hi - lo, 1)
        cand = jnp.where(step % SEARCH_BISECT_EVERY == SEARCH_BISECT_EVERY - 1, middle, guess)
        cand = jnp.minimum(jnp.maximum(cand, lo + 1), hi - 1)
        cand_ref[...] = to_rows(cand)
        return cand, jnp.where(active, 1, 0), jnp.max(jnp.where(active, 1, 0))

    def search_cond(carry):
        return jnp.logical_and(carry[0] < SEARCH_MAX_STEPS, carry[1] > 0)

    def search_body(carry):
        step, _, cand, active, lo, c_lo, hi, c_hi = carry
        c = count_prepared()
        up = jnp.logical_and(active > 0, c >= n_select)
        down = jnp.logical_and(active > 0, c < n_select)
        lo, c_lo = jnp.where(up, cand, lo), jnp.where(up, c, c_lo)
        hi, c_hi = jnp.where(down, cand, hi), jnp.where(down, c, c_hi)
        cand, active, busy = propose(step + 1, lo, c_lo, hi, c_hi)
        return step + 1, busy, cand, active, lo, c_lo, hi, c_hi

    cand, active, busy = propose(jnp.int32(0), lo, c_lo, hi, c_hi)
    lo, c_lo, hi, c_hi = lax.while_loop(
        search_cond, search_body, (jnp.int32(0), busy, cand, active, lo, c_lo, hi, c_hi))[4:]

    over = jnp.logical_and(jnp.logical_not(few), jnp.logical_and(c_lo == n_select + 1, hi - lo > 1))
    least = smallest_ge(lo)
    c_drop = count_ge(jnp.where(over, least + 1, lo))
    dropped = jnp.logical_and(over, c_drop == n_select)
    theta = jnp.where(dropped, least + 1, jnp.where(over, least, lo))
    theta = jnp.where(few, INT_MIN, theta)
    c_above = jnp.where(over, c_drop, c_hi)
    tied = jnp.logical_and(jnp.logical_not(few), jnp.logical_and(c_lo > n_select, jnp.logical_not(dropped)))

    @pl.when(jnp.max(jnp.where(tied, 1, 0)) > 0)
    def _():
        theta_r = to_rows(theta)[:, 0:1]
        tied_r = to_rows(jnp.where(tied, 1, 0))[:, 0:1] > 0
        need = to_rows((n_select - c_above).astype(F32))[:, 0:1]
        r = lax.broadcasted_iota(I32, (tk, tk), 0)
        c = lax.broadcasted_iota(I32, (tk, tk), 1)
        before = jnp.where(r < c, 1.0, 0.0).astype(BF16)

        def body(j, seen):
            key = key_ref[j]
            eq = jnp.logical_and(key == theta_r, tied_r)
            eqf = jnp.where(eq, 1.0, 0.0)
            rank = seen + _dot(eqf.astype(BF16), before)
            key_ref[j] = jnp.where(jnp.logical_and(eq, rank >= need), INT_MIN, key)
            return seen + jnp.sum(eqf, axis=-1, keepdims=True)

        lax.fori_loop(0, n_blocks, body, jnp.zeros((tq, 1), F32))

    cand_ref[...] = to_rows(jnp.maximum(theta, INT_MIN + 1))

    qd = qd_ref[...]
    q_heads = []
    for h in range(DSA_HEADS):
        blk = qd[:, (h // 2) * LANES:(h // 2 + 1) * LANES]
        head = (lane >= (h % 2) * DSA_DIM) & (lane < (h % 2 + 1) * DSA_DIM)
        q_heads.append(jnp.where(head, blk, jnp.zeros_like(blk)))
    _softmax_init(acc_ref, l_ref, m_ref)

    def attend(j, c):
        start = pl.multiple_of(j * tk, tk)
        sel = key_ref[j] >= jnp.concatenate([cand_ref[...]] * reps, axis=1)
        kb = kd_ref[pl.ds(start, tk), :]
        vb = vd_ref[pl.ds(start, tk), :]
        for h in range(DSA_HEADS):
            p0 = (h // 2) * LANES
            s = jnp.where(sel, _dot_t(q_heads[h], kb[:, p0:p0 + LANES]), NEG_BIG)
            _softmax_step(s, vb[:, p0:p0 + LANES], h, acc_ref, l_ref, m_ref)
        return c

    def attend_pair(j2, c):
        start = pl.multiple_of(j2 * 2 * tk, tk)
        keys = jnp.concatenate([key_ref[2 * j2], key_ref[2 * j2 + 1]], axis=1)
        sel = keys >= jnp.concatenate([cand_ref[...]] * (2 * reps), axis=1)
        kb = kd_ref[pl.ds(start, 2 * tk), :]
        vb = vd_ref[pl.ds(start, 2 * tk), :]
        for h in range(DSA_HEADS):
            p0 = (h // 2) * LANES
            s = jnp.where(sel, _dot_t(q_heads[h], kb[:, p0:p0 + LANES]), NEG_BIG)
            _softmax_step(s, vb[:, p0:p0 + LANES], h, acc_ref, l_ref, m_ref)
        return c

    n_pairs = n_blocks // 2
    lax.fori_loop(0, n_pairs, attend_pair, 0)
    lax.fori_loop(2 * n_pairs, n_blocks, attend, 0)
    for p in range(DSA_HEADS // 2):
        o_ref[:, p * LANES:(p + 1) * LANES] = jnp.where(
            lane < DSA_DIM, _softmax_out(2 * p, acc_ref, l_ref), _softmax_out(2 * p + 1, acc_ref, l_ref)).astype(BF16)


def _dsa_attention(a, rp, w8, batch, seq, n_select):
    t = a.shape[0]
    tq = 256
    tk = min(1024, seq)
    nq = seq // tq
    width = DSA_HEADS * DSA_DIM
    keys = lambda col: pl.BlockSpec((seq, width), lambda b, i: (b, col), pipeline_mode=pl.Buffered(1))
    stat = lambda n: pltpu.VMEM((n, tq, LANES), F32)
    return pl.pallas_call(
        functools.partial(_dsa_kernel, tq=tq, tk=tk, n_select=n_select),
        grid=(batch, nq),
        in_specs=[
            pl.BlockSpec((tq, width), lambda b, i: (b * nq + i, 0)),
            pl.BlockSpec((tq, width), lambda b, i: (b * nq + i, 2)),
            pl.BlockSpec((tq, N_W), lambda b, i: (b * nq + i, 0)),
            keys(3),
            keys(1),
            keys(0),
        ],
        out_specs=pl.BlockSpec((tq, width), lambda b, i: (b * nq + i, 0)),
        out_shape=jax.ShapeDtypeStruct((t, width), BF16),
        scratch_shapes=[
            pltpu.VMEM((seq // tk, tq, tk), I32),
            pltpu.VMEM((IDX_HEADS * tq, width), BF16),
            stat(IDX_HEADS),
            pltpu.VMEM((2, tq, LANES), I32),
            pltpu.VMEM((tq, LANES), I32),
            pltpu.VMEM((tq, LANES), I32),
            pltpu.VMEM((3, tq, LANES), I32),
            stat(DSA_HEADS), stat(DSA_HEADS), stat(DSA_HEADS),
        ],
        compiler_params=_cparams(("arbitrary", "arbitrary")),
        name="dsa_attention",
    )(rp, rp, w8, rp, rp, a)


def _merge_kernel(x_ref, osb_ref, omla_ref, odsa_ref, g_ref, gm_ref, wsb_ref, wmla_ref, wdsa_ref, wout_ref,
                  sc_ref, sh_ref, rw_ref, rb_ref,
                  o_ref, h_ref, ti_ref, tw_ref, tr_ref, cnt_ref, run_ref, *, tm):
    d = D_MODEL
    merged = (g_ref[:, 0:d].astype(F32) * _dot(osb_ref[...], wsb_ref[...])
              + g_ref[:, d:2 * d].astype(F32) * _dot(omla_ref[...], wmla_ref[...])
              + g_ref[:, 2 * d:3 * d].astype(F32) * _dot(odsa_ref[...], wdsa_ref[...]))
    x = x_ref[...] + gm_ref[...] * _dot(merged.astype(BF16), wout_ref[...])
    o_ref[...] = x
    _route(x, sc_ref, sh_ref, rw_ref, rb_ref, h_ref, ti_ref, tw_ref, tr_ref, cnt_ref, run_ref, tm)


def _merge(x, osb, omla, odsa, gates, gm, wsb, wmla, wdsa, wout, scale_f, shift_f, rw, rb, layer, seq):
    t, d = x.shape
    tm = 512
    per = seq // tm
    row = lambda i: (i, 0)
    wspec = lambda k: pl.BlockSpec((None, k, d), lambda i: (layer, 0, 0))
    mod = pl.BlockSpec((None, None, 1, d), lambda i: (layer, i // per, 0, 0))
    return pl.pallas_call(
        functools.partial(_merge_kernel, tm=tm),
        grid=(t // tm,),
        in_specs=[
            pl.BlockSpec((tm, d), row),
            pl.BlockSpec((tm, osb.shape[1]), row),
            pl.BlockSpec((tm, omla.shape[1]), row),
            pl.BlockSpec((tm, odsa.shape[1]), row),
            pl.BlockSpec((tm, N_G), row),
            mod,
            wspec(osb.shape[1]), wspec(omla.shape[1]), wspec(odsa.shape[1]), wspec(d),
            mod, mod,
            pl.BlockSpec((None, d, LANES), lambda i: (layer, 0, 0)),
            pl.BlockSpec((None, 1, LANES), lambda i: (layer, 0, 0)),
        ],
        out_specs=[
            pl.BlockSpec((tm, d), row),
            pl.BlockSpec((tm, d), row),
            pl.BlockSpec((tm, LANES), row),
            pl.BlockSpec((tm, LANES), row),
            pl.BlockSpec((tm, LANES), row),
            pl.BlockSpec((1, LANES), lambda i: (0, 0)),
        ],
        out_shape=[
            jax.ShapeDtypeStruct((t, d), F32),
            jax.ShapeDtypeStruct((t, d), F32),
            jax.ShapeDtypeStruct((t, LANES), I32),
            jax.ShapeDtypeStruct((t, LANES), F32),
            jax.ShapeDtypeStruct((t, LANES), I32),
            jax.ShapeDtypeStruct((1, LANES), F32),
        ],
        scratch_shapes=[pltpu.VMEM((1, LANES), F32)],
        compiler_params=_cparams(("arbitrary",)),
        name="merge",
    )(x, osb, omla, odsa, gates, gm, wsb, wmla, wdsa, wout, scale_f, shift_f, rw, rb)


def _route(x, sc_ref, sh_ref, rw_ref, rb_ref, h_ref, ti_ref, tw_ref, tr_ref, cnt_ref, run_ref, tm):
    @pl.when(pl.program_id(0) == 0)
    def _():
        run_ref[...] = jnp.zeros_like(run_ref)

    h = _rms(x) * sc_ref[...] + sh_ref[...]
    h_ref[...] = h
    logits = jnp.dot(h, rw_ref[...], preferred_element_type=F32, precision=lax.Precision.HIGHEST) + rb_ref[...]
    lane = lax.broadcasted_iota(I32, (tm, LANES), 1)
    work = logits
    vals, hots = [], []
    for _ in range(TOP_K):
        m = jnp.max(work, axis=-1, keepdims=True)
        first = jnp.min(jnp.where(work == m, lane, LANES), axis=-1, keepdims=True)
        hot = lane == first
        vals.append(m)
        hots.append(hot)
        work = jnp.where(hot, -jnp.inf, work)
    exps = [jnp.exp(v - vals[0]) for v in vals]
    denom = exps[0] + exps[1] + exps[2] + exps[3]
    chosen = jnp.zeros((tm, LANES), F32)
    for hot in hots:
        chosen = chosen + jnp.where(hot, 1.0, 0.0)
    r = lax.broadcasted_iota(I32, (tm, tm), 0)
    c = lax.broadcasted_iota(I32, (tm, tm), 1)
    earlier = jnp.where(c < r, 1.0, 0.0).astype(BF16)
    rank_all = _dot(earlier, chosen.astype(BF16)) + run_ref[...]
    ti = jnp.zeros((tm, LANES), I32)
    tw = jnp.zeros((tm, LANES), F32)
    tr = jnp.zeros((tm, LANES), I32)
    for k in range(TOP_K):
        e_k = jnp.sum(jnp.where(hots[k], lane, 0), axis=-1, keepdims=True)
        r_k = jnp.sum(jnp.where(hots[k], rank_all, 0.0), axis=-1, keepdims=True).astype(I32)
        ti = jnp.where(lane == k, e_k, ti)
        tw = jnp.where(lane == k, exps[k] / denom, tw)
        tr = jnp.where(lane == k, r_k, tr)
    ti_ref[...] = ti
    tw_ref[...] = tw
    tr_ref[...] = tr
    run_ref[...] = run_ref[...] + jnp.sum(chosen, axis=0, keepdims=True)
    cnt_ref[...] = run_ref[...]


def _expert_kernel(te_ref, tf_ref, nv_ref, x_ref, wgu_ref, bgu_ref, wd_ref, bd_ref, o_ref,
                   wgu_bf, wd_bf):
    i = pl.program_id(0)

    @pl.when(i >= nv_ref[0])
    def _():
        o_ref[...] = jnp.zeros_like(o_ref)

    @pl.when(i < nv_ref[0])
    def _():
        @pl.when(tf_ref[i] == 1)
        def _():
            wgu_bf[...] = wgu_ref[...].astype(BF16)
            wd_bf[...] = wd_ref[...].astype(BF16)

        gu = _dot(x_ref[...].astype(BF16), wgu_bf[...]) + bgu_ref[...]
        gate = jnp.minimum(gu[:, :D_EXPERT], SWIGLU_LIMIT)
        up = jnp.clip(gu[:, D_EXPERT:], -SWIGLU_LIMIT, SWIGLU_LIMIT)
        act = (up + 1.0) * (gate * (1.0 / (1.0 + jnp.exp(-SWIGLU_ALPHA * gate))))
        o_ref[...] = _dot(act.astype(BF16), wd_bf[...]) + bd_ref[...]


def _experts(xs, tile_e, tile_first, n_valid, wgu, bgu, wd, bd, layer, tm):
    p, d = xs.shape
    n_tiles = p // tm
    grid_spec = pltpu.PrefetchScalarGridSpec(
        num_scalar_prefetch=3,
        grid=(n_tiles,),
        in_specs=[
            pl.BlockSpec((tm, d), lambda i, te, tf, nv: (i, 0)),
            pl.BlockSpec((None, None, d, 2 * D_EXPERT), lambda i, te, tf, nv: (layer, te[i], 0, 0)),
            pl.BlockSpec((None, None, 1, 2 * D_EXPERT), lambda i, te, tf, nv: (layer, te[i], 0, 0)),
            pl.BlockSpec((None, None, D_EXPERT, d), lambda i, te, tf, nv: (layer, te[i], 0, 0)),
            pl.BlockSpec((None, None, 1, d), lambda i, te, tf, nv: (layer, te[i], 0, 0)),
        ],
        out_specs=pl.BlockSpec((tm, d), lambda i, te, tf, nv: (i, 0)),
        scratch_shapes=[pltpu.VMEM((d, 2 * D_EXPERT), BF16), pltpu.VMEM((D_EXPERT, d), BF16)],
    )
    return pl.pallas_call(
        _expert_kernel,
        grid_spec=grid_spec,
        out_shape=jax.ShapeDtypeStruct((p, d), F32),
        compiler_params=_cparams(("arbitrary",)),
        name="experts",
    )(tile_e, tile_first, n_valid, xs, wgu, bgu, wd, bd)


def _combine_kernel(x_ref, y0_ref, y1_ref, y2_ref, y3_ref, tw_ref, g_ref, fg_ref, o_ref, *, final):
    tw = tw_ref[...]
    y = tw[:, 0:1] * y0_ref[...]
    for k, y_ref in enumerate((y1_ref, y2_ref, y3_ref), start=1):
        y = y + tw[:, k:k + 1] * y_ref[...]
    x = x_ref[...] + g_ref[...] * y
    if final:
        x = _rms(x) * fg_ref[...]
    o_ref[...] = x


def _combine(x, yg, tw, gf, final_g, layer, seq, final):
    t, d = x.shape
    tm = 256
    nt = t // tm
    per = seq // tm
    slot = lambda k: pl.BlockSpec((tm, d), lambda i: (k * nt + i, 0))
    return pl.pallas_call(
        functools.partial(_combine_kernel, final=final),
        grid=(nt,),
        in_specs=[
            pl.BlockSpec((tm, d), lambda i: (i, 0)),
            slot(0), slot(1), slot(2), slot(3),
            pl.BlockSpec((tm, LANES), lambda i: (i, 0)),
            pl.BlockSpec((None, None, 1, d), lambda i: (layer, i // per, 0, 0)),
            pl.BlockSpec((1, d), lambda i: (0, 0)),
        ],
        out_specs=pl.BlockSpec((tm, d), lambda i: (i, 0)),
        out_shape=jax.ShapeDtypeStruct((t, d), F32),
        compiler_params=_cparams(("arbitrary",)),
        name="combine",
    )(x, yg, yg, yg, yg, tw, gf, final_g)


def _rope_tables(positions):
    pos = positions.reshape(-1).astype(F32)

    def cs(dim):
        inv_freq = ROPE_THETA ** (-jnp.arange(0, dim, 2, dtype=F32) / dim)
        ang = pos[:, None] * inv_freq
        return jnp.cos(ang), jnp.sin(ang)

    c_d, s_d = cs(DSA_DIM)
    c_i, s_i = cs(IDX_DIM)
    c_m, s_m = cs(MLA_ROPE)
    t = pos.shape[0]
    ones, zeros = jnp.ones((t, 64), F32), jnp.zeros((t, 64), F32)
    cos_t = jnp.concatenate([jnp.tile(c_d, (1, 4)), jnp.tile(c_i, (1, 8)),
                             ones, c_m, c_m, ones[:, :32]], axis=1)
    sin_t = jnp.concatenate([jnp.tile(s_d, (1, 4)), jnp.tile(s_i, (1, 8)),
                             zeros, s_m, s_m, zeros[:, :32]], axis=1)
    return cos_t, sin_t


def _permute_cols(w, idx, sgn):
    return (w.at[..., jnp.asarray(idx)].get(mode="promise_in_bounds") * jnp.asarray(sgn)).astype(BF16)


def _dispatch_tables(ti, tr, counts, tm):
    t = ti.shape[0]
    p = t * TOP_K + N_EXPERTS * tm
    n_tiles = p // tm
    cnt = counts.astype(I32)
    padded = ((cnt + tm - 1) // tm) * tm
    ends = jnp.cumsum(padded)
    starts = ends - padded
    pos = starts[ti] + tr
    tok = jnp.repeat(jnp.arange(t, dtype=I32), TOP_K)
    row_src = jnp.zeros((p,), I32).at[pos.reshape(-1)].set(tok, unique_indices=True, mode="promise_in_bounds")
    tile_start = jnp.arange(n_tiles, dtype=I32) * tm
    n_valid = ends[-1] // tm
    tile_e = jnp.sum((ends[None, :] <= tile_start[:, None]).astype(I32), axis=1)
    tile_e = jnp.minimum(tile_e, N_EXPERTS - 1)
    last_e = jnp.max(jnp.where(tile_start < ends[-1], tile_e, 0))
    tile_e = jnp.where(tile_start < ends[-1], tile_e, last_e)
    tile_first = jnp.concatenate([jnp.ones((1,), I32), (tile_e[1:] != tile_e[:-1]).astype(I32)])
    return pos, row_src, tile_e, tile_first, n_valid.astype(I32).reshape(1)


def kernel(x, c, positions, ada_w, ada_b, norm_mix_g, w_in, mla_q_norm_g, mla_kv_norm_g, mla_w_uq, mla_w_ukv,
           w_sb_out, w_mla_out, w_dsa_out, w_out, norm_ffn_g, router_w, router_b, expert_w_gu, expert_b_gu,
           expert_w_down, expert_b_down, final_norm_g):
    batch, seq, d = x.shape
    depth = ada_w.shape[0]
    t = batch * seq
    n_select = min(DSA_TOPK_MAX, seq // 4)
    tm_e = 256

    mod = _ada_mod(c, ada_w, ada_b)
    sh_m, sc_m, g_m, sh_f, sc_f, g_f = [m[:, :, None, :] for m in jnp.split(mod, 6, axis=-1)]
    scale_m = norm_mix_g[:, None, None, :] * (1.0 + sc_m)
    scale_f = norm_ffn_g[:, None, None, :] * (1.0 + sc_f)

    cos_t, sin_t = _rope_tables(positions)
    w_in_p = _permute_cols(w_in, _IN_IDX, _IN_SGN)
    w_uq_p = _permute_cols(mla_w_uq, _UQ_IDX, _UQ_SGN)
    w_ukv_p = _permute_cols(mla_w_ukv, _UKV_IDX, _UKV_SGN)
    w_sb_b, w_mla_b, w_dsa_b, w_out_b = [w.astype(BF16) for w in (w_sb_out, w_mla_out, w_dsa_out, w_out)]
    gq = mla_q_norm_g[:, None, :]
    gkv = mla_kv_norm_g[:, None, :]
    rw_p = jnp.zeros((depth, d, LANES), F32).at[:, :, :N_EXPERTS].set(router_w)
    rb_p = jnp.full((depth, 1, LANES), -jnp.inf, F32).at[:, 0, :N_EXPERTS].set(router_b)
    bgu = expert_b_gu[:, :, None, :]
    bd = expert_b_down[:, :, None, :]
    fg = final_norm_g[None, :]

    xf = x.reshape(t, d)
    for l in range(depth):
        a, cl, rp, w8, gates = _inproj(xf, scale_m, sh_m, w_in_p, cos_t, sin_t, l, seq)
        o_sb = _sb_attention(a, batch, seq)
        qm, km, vm = _mla_up(cl, gq, gkv, w_uq_p, w_ukv_p, cos_t, sin_t, rp, l)
        o_mla = _mla_attention(qm, km, vm, batch, seq)
        o_dsa = _dsa_attention(a, rp, w8, batch, seq, n_select)
        xf, hf, ti, tw, tr, counts = _merge(xf, o_sb, o_mla, o_dsa, gates, g_m, w_sb_b, w_mla_b, w_dsa_b, w_out_b,
                                            scale_f, sh_f, rw_p, rb_p, l, seq)
        pos, row_src, tile_e, tile_first, n_valid = _dispatch_tables(
            ti[:, :TOP_K], tr[:, :TOP_K], counts[0, :N_EXPERTS], tm_e)
        xs = hf.at[row_src].get(mode="promise_in_bounds")
        ys = _experts(xs, tile_e, tile_first, n_valid, expert_w_gu, bgu, expert_w_down, bd, l, tm_e)
        yg = ys.at[pos.T.reshape(-1)].get(mode="promise_in_bounds")
        xf = _combine(xf, yg, tw, g_f, fg, l, seq, l == depth - 1)
    return xf.reshape(batch, seq, d)
```
